```python
import jax, jax.numpy as jnp
from jax import lax
import numpy as np

D_MODEL = 1024
BATCH = 8
SEQ = 8192
DEPTH = 2

CHUNK = 64
N_MIXERS = 2
N_POOL_LAYERS = (DEPTH + 1) // 2
N_ATT_LAYERS = DEPTH // 2

POOL_WIDTH = 2 * D_MODEL
POOL_WINDOWS = (2, 4, 8, 16)
N_POOL_GROUPS = len(POOL_WINDOWS)
POOL_GROUP = POOL_WIDTH // N_POOL_GROUPS

HEAD_DIM = 64
ATT_WIDTH = D_MODEL
N_HEADS = ATT_WIDTH // HEAD_DIM
LEFT_CHUNKS = 8
BAND = (LEFT_CHUNKS + 1) * CHUNK
MAX_REL = 256

RMS_EPS = 1e-6

kernel_name = "hybrid_pool_chunkattn_sandwich"


def rms_norm(x, g):
    xf = x.astype(jnp.float32)
    y = xf * lax.rsqrt(jnp.mean(xf * xf, axis=-1, keepdims=True) + RMS_EPS)
    return (y * g.astype(jnp.float32)).astype(x.dtype)


def pool_mixer(h, w_in, w_group, scale, w_out):
    b, s, _ = h.shape
    u = h @ w_in
    a, z = jnp.split(u, 2, axis=-1)
    ag = a.astype(jnp.float32).reshape(b, s, N_POOL_GROUPS, POOL_GROUP)
    cs = jnp.cumsum(ag, axis=1)
    cs0 = jnp.concatenate([jnp.zeros((b, 1, N_POOL_GROUPS, POOL_GROUP), jnp.float32), cs], axis=1)
    pos = jnp.arange(s)
    pooled = []
    for gi, w in enumerate(POOL_WINDOWS):
        cg = cs0[:, :, gi]
        lagged = jnp.concatenate([jnp.zeros((b, w, POOL_GROUP), jnp.float32), cg], axis=1)[:, 1:s + 1]
        cnt = jnp.minimum(pos + 1, w).astype(jnp.float32)[None, :, None]
        pooled.append((cg[:, 1:] - lagged) / cnt)
    mixed = (jnp.stack(pooled, axis=2) - ag).astype(a.dtype)
    mixed = jnp.einsum('bsgc,gcd->bsgd', mixed, w_group).reshape(b, s, POOL_WIDTH) * scale
    return (mixed * jax.nn.silu(z)) @ w_out


def chunk_attention(h, w_in, rel_bias, w_out):
    b, s, _ = h.shape
    nc = s // CHUNK
    pad = LEFT_CHUNKS * CHUNK
    u = h @ w_in
    q, k, v, z = jnp.split(u, 4, axis=-1)
    q = q.reshape(b, s, N_HEADS, HEAD_DIM)
    k = k.reshape(b, s, N_HEADS, HEAD_DIM)
    v = v.reshape(b, s, N_HEADS, HEAD_DIM)
    kp = jnp.pad(k, ((0, 0), (pad, 0), (0, 0), (0, 0)))
    vp = jnp.pad(v, ((0, 0), (pad, 0), (0, 0), (0, 0)))
    qc = q.reshape(b, nc, CHUNK, N_HEADS, HEAD_DIM).transpose(1, 0, 2, 3, 4)
    rel = jnp.arange(CHUNK)[:, None] + pad - jnp.arange(BAND)[None, :]
    idx = jnp.clip(rel, -MAX_REL, MAX_REL) + MAX_REL
    bias = rel_bias.astype(jnp.float32)[:, idx]
    qk_scale = HEAD_DIM ** -0.5

    def one_chunk(args):
        c, qb = args
        start = c * CHUNK
        kb = lax.dynamic_slice_in_dim(kp, start, BAND, axis=1)
        vb = lax.dynamic_slice_in_dim(vp, start, BAND, axis=1)
        sc = jnp.einsum('bqhd,bkhd->bhqk', qb, kb).astype(jnp.float32) * qk_scale + bias
        valid = (start - pad + jnp.arange(BAND)) >= 0
        sc = jnp.where(valid[None, None, None, :], sc, -jnp.inf)
        p = jax.nn.softmax(sc, axis=-1).astype(vb.dtype)
        return jnp.einsum('bhqk,bkhd->bqhd', p, vb)

    o = lax.map(one_chunk, (jnp.arange(nc), qc))
    o = o.transpose(1, 0, 2, 3, 4).reshape(b, s, ATT_WIDTH)
    return (o * jax.nn.silu(z)) @ w_out


def _fwd_setup_inputs(seed: int = 0) -> dict:
    key = jax.random.key(seed)
    ks = jax.random.split(key, 12)
    f32 = jnp.float32
    x = jax.random.normal(ks[0], (BATCH, SEQ, D_MODEL), f32)
    norm_pre = 1.0 + 0.05 * jax.random.normal(ks[1], (DEPTH, D_MODEL), f32)
    norm_post = 1.0 + 0.05 * jax.random.normal(ks[2], (DEPTH, D_MODEL), f32)
    pool_w_in = jax.random.normal(ks[3], (N_POOL_LAYERS, D_MODEL, 2 * POOL_WIDTH), f32) * D_MODEL ** -0.5
    pool_w_group = jax.random.normal(ks[4], (N_POOL_LAYERS, N_POOL_GROUPS, POOL_GROUP, POOL_GROUP), f32) * POOL_GROUP ** -0.5
    pool_scale = 1.0 + 0.1 * jax.random.normal(ks[5], (N_POOL_LAYERS, POOL_WIDTH), f32)
    pool_w_out = jax.random.normal(ks[6], (N_POOL_LAYERS, POOL_WIDTH, D_MODEL), f32) * POOL_WIDTH ** -0.5
    att_w_in = jax.random.normal(ks[7], (N_ATT_LAYERS, D_MODEL, 4 * ATT_WIDTH), f32) * D_MODEL ** -0.5
    att_rel_bias = 0.5 * jax.random.normal(ks[8], (N_ATT_LAYERS, N_HEADS, 2 * MAX_REL + 1), f32)
    att_w_out = jax.random.normal(ks[9], (N_ATT_LAYERS, ATT_WIDTH, D_MODEL), f32) * ATT_WIDTH ** -0.5
    return {"x": x, "norm_pre": norm_pre, "norm_post": norm_post,
            "pool_w_in": pool_w_in, "pool_w_group": pool_w_group, "pool_scale": pool_scale,
            "pool_w_out": pool_w_out, "att_w_in": att_w_in, "att_rel_bias": att_rel_bias,
            "att_w_out": att_w_out}


def _fwd_reference(x, norm_pre, norm_post, pool_w_in, pool_w_group, pool_scale, pool_w_out,
              att_w_in, att_rel_bias, att_w_out):
    for i in range(DEPTH):
        h = rms_norm(x, norm_pre[i])
        j = i // N_MIXERS
        if i % N_MIXERS == 0:
            y = pool_mixer(h, pool_w_in[j], pool_w_group[j], pool_scale[j], pool_w_out[j])
        else:
            y = chunk_attention(h, att_w_in[j], att_rel_bias[j], att_w_out[j])
        x = x + rms_norm(y, norm_post[i])
    return x


import jax as _jax
import jax.numpy as _jnp

TWIN_FORMAT = 'train_step'
FWD_PARAMS = ['x', 'norm_pre', 'norm_post', 'pool_w_in', 'pool_w_group', 'pool_scale', 'pool_w_out', 'att_w_in', 'att_rel_bias', 'att_w_out']
TWIN_WEIGHTS = ['norm_pre', 'norm_post', 'pool_w_in', 'pool_w_group', 'pool_scale', 'pool_w_out', 'att_w_in', 'att_rel_bias', 'att_w_out']
TWIN_DIFF_INPUT = 'x'
TWIN_INPUTS = ['x', 'norm_pre', 'norm_post', 'pool_w_in', 'pool_w_group', 'pool_scale', 'pool_w_out', 'att_w_in', 'att_rel_bias', 'att_w_out', 'loss_target', 'm_norm_pre', 'm_norm_post', 'm_pool_w_in', 'm_pool_w_group', 'm_pool_scale', 'm_pool_w_out', 'm_att_w_in', 'm_att_rel_bias', 'm_att_w_out', 'v_norm_pre', 'v_norm_post', 'v_pool_w_in', 'v_pool_w_group', 'v_pool_scale', 'v_pool_w_out', 'v_att_w_in', 'v_att_rel_bias', 'v_att_w_out']
TWIN_OUTPUTS = ['loss', 'grad_x', 'grad_norm_pre', 'grad_norm_post', 'grad_pool_w_in', 'grad_pool_w_group', 'grad_pool_scale', 'grad_pool_w_out', 'grad_att_w_in', 'grad_att_rel_bias', 'grad_att_w_out', 'delta_norm_pre', 'delta_norm_post', 'delta_pool_w_in', 'delta_pool_w_group', 'delta_pool_scale', 'delta_pool_w_out', 'delta_att_w_in', 'delta_att_rel_bias', 'delta_att_w_out', 'new_m_norm_pre', 'new_m_norm_post', 'new_m_pool_w_in', 'new_m_pool_w_group', 'new_m_pool_scale', 'new_m_pool_w_out', 'new_m_att_w_in', 'new_m_att_rel_bias', 'new_m_att_w_out', 'new_v_norm_pre', 'new_v_norm_post', 'new_v_pool_w_in', 'new_v_pool_w_group', 'new_v_pool_scale', 'new_v_pool_w_out', 'new_v_att_w_in', 'new_v_att_rel_bias', 'new_v_att_w_out']
TWIN_LEAF_KINDS = {'loss': 'loss', 'grad_x': 'grad_x', 'grad_norm_pre': 'grad_w', 'grad_norm_post': 'grad_w', 'grad_pool_w_in': 'grad_w', 'grad_pool_w_group': 'grad_w', 'grad_pool_scale': 'grad_w', 'grad_pool_w_out': 'grad_w', 'grad_att_w_in': 'grad_w', 'grad_att_rel_bias': 'grad_w', 'grad_att_w_out': 'grad_w', 'delta_norm_pre': 'delta_w', 'delta_norm_post': 'delta_w', 'delta_pool_w_in': 'delta_w', 'delta_pool_w_group': 'delta_w', 'delta_pool_scale': 'delta_w', 'delta_pool_w_out': 'delta_w', 'delta_att_w_in': 'delta_w', 'delta_att_rel_bias': 'delta_w', 'delta_att_w_out': 'delta_w', 'new_m_norm_pre': 'new_m', 'new_m_norm_post': 'new_m', 'new_m_pool_w_in': 'new_m', 'new_m_pool_w_group': 'new_m', 'new_m_pool_scale': 'new_m', 'new_m_pool_w_out': 'new_m', 'new_m_att_w_in': 'new_m', 'new_m_att_rel_bias': 'new_m', 'new_m_att_w_out': 'new_m', 'new_v_norm_pre': 'new_v', 'new_v_norm_post': 'new_v', 'new_v_pool_w_in': 'new_v', 'new_v_pool_w_group': 'new_v', 'new_v_pool_scale': 'new_v', 'new_v_pool_w_out': 'new_v', 'new_v_att_w_in': 'new_v', 'new_v_att_rel_bias': 'new_v', 'new_v_att_w_out': 'new_v'}


def _forward(args):
    return _fwd_reference(*[args[k] for k in FWD_PARAMS])


def _output_shape():
    def fwd():
        inp = _fwd_setup_inputs(0)
        return _fwd_reference(*[inp[k] for k in FWD_PARAMS])
    out = _jax.eval_shape(fwd)
    return out.shape, out.dtype

N_MICROBATCH = 1
ADAM_LR = 0.001
ADAM_B1 = 0.9
ADAM_B2 = 0.999
ADAM_EPS = 1e-08
ADAM_WD = 0.01
ADAM_STEP = 10
PER_EXAMPLE_BATCH_AXIS = {'x': 0, 'loss_target': 0}
SHARED_INPUTS = []
_WEIGHT_DTYPES = {'norm_pre': _jnp.float32, 'norm_post': _jnp.float32, 'pool_w_in': _jnp.float32, 'pool_w_group': _jnp.float32, 'pool_scale': _jnp.float32, 'pool_w_out': _jnp.float32, 'att_w_in': _jnp.float32, 'att_rel_bias': _jnp.float32, 'att_w_out': _jnp.float32}
MOMENT_SCALE = {'norm_pre': 1.103123e+00, 'norm_post': 6.414676e+01, 'pool_w_in': 5.867444e-01, 'pool_w_group': 6.057581e-01, 'pool_scale': 6.565469e-01, 'pool_w_out': 8.963055e-01, 'att_w_in': 5.297978e-01, 'att_rel_bias': 1.451760e-01, 'att_w_out': 5.958003e-01}


def _to_microbatches(a, axis):
    t = _jnp.moveaxis(a, axis, 0)
    t = t.reshape((N_MICROBATCH, t.shape[0] // N_MICROBATCH) + t.shape[1:])
    return _jnp.moveaxis(t, 1, axis + 1)


def setup_inputs(seed: int = 0) -> dict:
    inp = _fwd_setup_inputs(seed)
    key = _jax.random.fold_in(_jax.random.key(seed), 7919)
    shape, _ = _output_shape()
    out = dict(inp)
    out["loss_target"] = _jax.random.normal(_jax.random.fold_in(key, 0), shape, _jnp.float32)
    for i, name in enumerate(TWIN_WEIGHTS):
        w = inp[name].astype(_jnp.float32)
        if MOMENT_SCALE is None:
            s = _jnp.sqrt(_jnp.mean(_jnp.square(w)) + 1e-30)
        else:
            s = MOMENT_SCALE[name]
        km, kv = _jax.random.split(_jax.random.fold_in(key, i + 1))
        out[name] = w
        out["m_" + name] = s * _jax.random.normal(km, w.shape, _jnp.float32)
        out["v_" + name] = (s * s) * _jax.random.uniform(kv, w.shape, _jnp.float32, 0.5, 1.5)
    if N_MICROBATCH > 1:
        for name, axis in PER_EXAMPLE_BATCH_AXIS.items():
            out[name] = _to_microbatches(out[name], axis)
    return {'x': out['x'], 'norm_pre': out['norm_pre'], 'norm_post': out['norm_post'], 'pool_w_in': out['pool_w_in'], 'pool_w_group': out['pool_w_group'], 'pool_scale': out['pool_scale'], 'pool_w_out': out['pool_w_out'], 'att_w_in': out['att_w_in'], 'att_rel_bias': out['att_rel_bias'], 'att_w_out': out['att_w_out'], 'loss_target': out['loss_target'], 'm_norm_pre': out['m_norm_pre'], 'm_norm_post': out['m_norm_post'], 'm_pool_w_in': out['m_pool_w_in'], 'm_pool_w_group': out['m_pool_w_group'], 'm_pool_scale': out['m_pool_scale'], 'm_pool_w_out': out['m_pool_w_out'], 'm_att_w_in': out['m_att_w_in'], 'm_att_rel_bias': out['m_att_rel_bias'], 'm_att_w_out': out['m_att_w_out'], 'v_norm_pre': out['v_norm_pre'], 'v_norm_post': out['v_norm_post'], 'v_pool_w_in': out['v_pool_w_in'], 'v_pool_w_group': out['v_pool_w_group'], 'v_pool_scale': out['v_pool_scale'], 'v_pool_w_out': out['v_pool_w_out'], 'v_att_w_in': out['v_att_w_in'], 'v_att_rel_bias': out['v_att_rel_bias'], 'v_att_w_out': out['v_att_w_out']}


def _loss(weights, diff, rest, loss_target):
    with _jax.named_scope("forward"):
        args = {**rest, TWIN_DIFF_INPUT: diff, **{k: w.astype(_WEIGHT_DTYPES[k]) for k, w in weights.items()}}
        y = _forward(args)
    with _jax.named_scope("loss_head"):
        err = _jnp.square(y.astype(_jnp.float32) - loss_target)
        return 0.5 * _jnp.sum(_jnp.mean(err, axis=-1)) if err.ndim else 0.5 * err


def _adamw(w, g, m, v):
    m = ADAM_B1 * m + (1.0 - ADAM_B1) * g
    v = ADAM_B2 * v + (1.0 - ADAM_B2) * _jnp.square(g)
    m_hat = m / (1.0 - ADAM_B1 ** ADAM_STEP)
    v_hat = v / (1.0 - ADAM_B2 ** ADAM_STEP)
    delta = -ADAM_LR * (m_hat / (_jnp.sqrt(v_hat) + ADAM_EPS) + ADAM_WD * w)
    return delta, m, v


def reference(x, norm_pre, norm_post, pool_w_in, pool_w_group, pool_scale, pool_w_out, att_w_in, att_rel_bias, att_w_out, loss_target, m_norm_pre, m_norm_post, m_pool_w_in, m_pool_w_group, m_pool_scale, m_pool_w_out, m_att_w_in, m_att_rel_bias, m_att_w_out, v_norm_pre, v_norm_post, v_pool_w_in, v_pool_w_group, v_pool_scale, v_pool_w_out, v_att_w_in, v_att_rel_bias, v_att_w_out):
    given = dict(x=x, norm_pre=norm_pre, norm_post=norm_post, pool_w_in=pool_w_in, pool_w_group=pool_w_group, pool_scale=pool_scale, pool_w_out=pool_w_out, att_w_in=att_w_in, att_rel_bias=att_rel_bias, att_w_out=att_w_out, loss_target=loss_target, m_norm_pre=m_norm_pre, m_norm_post=m_norm_post, m_pool_w_in=m_pool_w_in, m_pool_w_group=m_pool_w_group, m_pool_scale=m_pool_scale, m_pool_w_out=m_pool_w_out, m_att_w_in=m_att_w_in, m_att_rel_bias=m_att_rel_bias, m_att_w_out=m_att_w_out, v_norm_pre=v_norm_pre, v_norm_post=v_norm_post, v_pool_w_in=v_pool_w_in, v_pool_w_group=v_pool_w_group, v_pool_scale=v_pool_scale, v_pool_w_out=v_pool_w_out, v_att_w_in=v_att_w_in, v_att_rel_bias=v_att_rel_bias, v_att_w_out=v_att_w_out)
    weights = {n: given[n] for n in TWIN_WEIGHTS}
    shared = {n: given[n] for n in SHARED_INPUTS}
    per_example = {n: given[n] for n in ['x']}
    grad_fn = _jax.value_and_grad(_loss, argnums=(0, 1))

    def one_microbatch(ex, loss_target):
        ex = dict(ex)
        diff = ex.pop(TWIN_DIFF_INPUT)
        return grad_fn(weights, diff, {**shared, **ex}, loss_target)

    if N_MICROBATCH == 1:
        loss, (grad_w, grad_x) = one_microbatch(per_example, given["loss_target"])
    else:
        def body(carry, xs):
            loss_sum, grad_sum = carry
            l_k, (gw_k, gx_k) = one_microbatch(xs[0], xs[1])
            with _jax.named_scope("update"):
                return (loss_sum + l_k, _jax.tree.map(_jnp.add, grad_sum, gw_k)), gx_k

        init = (_jnp.zeros((), _jnp.float32), _jax.tree.map(_jnp.zeros_like, weights))
        (loss, grad_w), grad_x = _jax.lax.scan(body, init, (per_example, given["loss_target"]))
    with _jax.named_scope("update"):
        delta_w, new_m, new_v = {}, {}, {}
        for n in TWIN_WEIGHTS:
            delta_w[n], new_m[n], new_v[n] = _adamw(weights[n], grad_w[n], given["m_" + n], given["v_" + n])
    return (loss, grad_x, *[grad_w[n] for n in TWIN_WEIGHTS], *[delta_w[n] for n in TWIN_WEIGHTS],
            *[new_m[n] for n in TWIN_WEIGHTS], *[new_v[n] for n in TWIN_WEIGHTS])
```

```python
import functools

import jax
import jax.numpy as jnp
from jax import lax
from jax.experimental import pallas as pl
from jax.experimental.pallas import tpu as pltpu

F32 = jnp.float32
BF16 = jnp.bfloat16

RMS_EPS = 1e-6
CHUNK = 64
HEAD_DIM = 64
LEFT_CHUNKS = 8
PAD = LEFT_CHUNKS * CHUNK
MAX_REL = 256
POOL_WINDOWS = (2, 4, 8, 16)
HALO = 16
TS = 256
QB = 256
KB = QB + PAD
HEAD_PAIR = 2 * HEAD_DIM
NEG = -1e30
N_CHIPS = 4
N_DEV = 8

ADAM_LR = 0.001
ADAM_B1 = 0.9
ADAM_B2 = 0.999
ADAM_EPS = 1e-08
ADAM_WD = 0.01
ADAM_STEP = 10

VMEM_LIMIT = 56 * 1024 * 1024
MESH = pl.DeviceIdType.MESH
ANY = pl.BlockSpec(memory_space=pl.ANY)


def _dot(a, b):
    return jnp.dot(a, b, preferred_element_type=F32)


def _dot_nt(a, b):
    return lax.dot_general(a, b, (((1,), (1,)), ((), ())), preferred_element_type=F32)


def _dot_tn(a, b):
    return lax.dot_general(a, b, (((0,), (0,)), ((), ())), preferred_element_type=F32)


def _params(sem, limit=VMEM_LIMIT):
    return pltpu.CompilerParams(dimension_semantics=sem, vmem_limit_bytes=limit)


def _rms(x):
    return lax.rsqrt(jnp.mean(x * x, axis=-1, keepdims=True) + RMS_EPS)


def _rms_bwd(dyh, xh, r):
    return r * (dyh - xh * jnp.mean(dyh * xh, axis=-1, keepdims=True))


def _silu_parts(z):
    sig = jax.nn.sigmoid(z)
    return z * sig, sig * (1.0 + z * (1.0 - sig))


def _inv_count(tile, rows, w):
    t = tile * rows + lax.broadcasted_iota(jnp.int32, (rows, 1), 0)
    return 1.0 / jnp.minimum(t + 1, w).astype(F32)


def _start_weight_copies(copies):
    for c in copies:
        c.start()
    for c in copies:
        c.wait()


def _l0_fwd(x, g_pre, g_post, scale, wa, wg):
    s_len, d = x.shape
    pw, gw = 2 * d, d // 2
    q = gw // N_CHIPS
    nt = s_len // TS

    def body(x_ref, gpre_ref, gpost_ref, sc_ref, wa_ref, wg_ref,
             x1_ref, h_ref, mixed_ref, z_ref, y_ref,
             win_s, wout_s, wg_s, halo_s, sems):
        i = pl.program_id(0)

        @pl.when(i == 0)
        def _():
            copies = [pltpu.make_async_copy(wa_ref.at[:, pl.ds(0, d), :], win_s, sems.at[0]),
                      pltpu.make_async_copy(wa_ref.at[:, pl.ds(2 * d, gw), :], wout_s, sems.at[1])]
            copies += [pltpu.make_async_copy(wg_ref.at[j], wg_s.at[:, pl.ds(j * q, q), :], sems.at[2 + j])
                       for j in range(N_CHIPS)]
            _start_weight_copies(copies)
            halo_s[...] = jnp.zeros_like(halo_s)

        xt = x_ref[...]
        h = ((xt * _rms(xt)) * gpre_ref[...]).astype(BF16)
        h_ref[...] = h
        a_blocks = [_dot(h, win_s[0]), _dot(h, win_s[1])]
        y = jnp.zeros((TS, d), F32)
        for g, w in enumerate(POOL_WINDOWS):
            a_g = a_blocks[g // 2][:, (g % 2) * gw:(g % 2 + 1) * gw]
            ext = jnp.concatenate([halo_s[g], a_g], axis=0)
            shift = 1
            while shift < w:
                ext = ext + pltpu.roll(ext, shift, 0)
                shift *= 2
            mixed = (ext[HALO:] * _inv_count(i, TS, w) - a_g).astype(BF16)
            halo_s[g] = a_g[TS - HALO:]
            mixed_ref[:, g * gw:(g + 1) * gw] = mixed
            z_g = _dot(h, win_s[2 + g // 2, :, (g % 2) * gw:(g % 2 + 1) * gw])
            z_ref[:, g * gw:(g + 1) * gw] = z_g
            ms = _dot(mixed, wg_s[g]) * sc_ref[:, g * gw:(g + 1) * gw]
            gated = (ms * _silu_parts(z_g)[0]).astype(BF16)
            y = y + _dot(gated, wout_s[g])
        y_ref[...] = y
        x1_ref[...] = xt + (y * _rms(y)) * gpost_ref[...]

    tile = lambda wdt: pl.BlockSpec((TS, wdt), lambda i: (i, 0))
    row = lambda wdt: pl.BlockSpec((1, wdt), lambda i: (0, 0))
    return pl.pallas_call(
        body, name="l0_fwd", grid=(nt,),
        in_specs=[tile(d), row(d), row(d), row(pw), ANY, ANY],
        out_specs=[tile(d), tile(d), tile(pw), tile(pw), tile(d)],
        out_shape=[jax.ShapeDtypeStruct((s_len, d), F32), jax.ShapeDtypeStruct((s_len, d), BF16),
                   jax.ShapeDtypeStruct((s_len, pw), BF16), jax.ShapeDtypeStruct((s_len, pw), F32),
                   jax.ShapeDtypeStruct((s_len, d), F32)],
        scratch_shapes=[pltpu.VMEM((N_CHIPS, d, d), BF16), pltpu.VMEM((N_CHIPS, gw, d), BF16),
                        pltpu.VMEM((N_CHIPS, gw, gw), BF16), pltpu.VMEM((N_CHIPS, HALO, gw), F32),
                        pltpu.SemaphoreType.DMA((2 + N_CHIPS,))],
        compiler_params=_params(("arbitrary",)),
    )(x, g_pre, g_post, scale, wa, wg)


def _l0_bwd(dx1, x, y, mixed, z, g_pre, g_post, scale, wa, wg):
    s_len, d = x.shape
    pw, gw = 2 * d, d // 2
    q = gw // N_CHIPS
    nt = s_len // TS

    def body(dx1_ref, x_ref, y_ref, mixed_ref, z_ref, gpre_ref, gpost_ref, sc_ref, wa_ref, wg_ref,
             gx_ref, du_ref, gated_ref, dy_ref, dmm_ref, dgpre_ref, dgpost_ref, dsc_ref,
             win_s, wout_s, wg_s, halo_s, sems):
        i = pl.program_id(0)
        tile = nt - 1 - i

        @pl.when(i == 0)
        def _():
            copies = [pltpu.make_async_copy(wa_ref.at[:, pl.ds(0, d), :], win_s, sems.at[0]),
                      pltpu.make_async_copy(wa_ref.at[:, pl.ds(2 * d, gw), :], wout_s, sems.at[1])]
            copies += [pltpu.make_async_copy(wg_ref.at[j], wg_s.at[:, pl.ds(j * q, q), :], sems.at[2 + j])
                       for j in range(N_CHIPS)]
            _start_weight_copies(copies)
            halo_s[...] = jnp.zeros_like(halo_s)
            dgpre_ref[...] = jnp.zeros_like(dgpre_ref)
            dgpost_ref[...] = jnp.zeros_like(dgpost_ref)
            dsc_ref[...] = jnp.zeros_like(dsc_ref)

        g_in = dx1_ref[...]
        yt = y_ref[...]
        r_y = _rms(yt)
        yh = yt * r_y
        dgpost_ref[...] += jnp.sum(g_in * yh, axis=0, keepdims=True)
        dy = _rms_bwd(g_in * gpost_ref[...], yh, r_y).astype(BF16)
        dy_ref[...] = dy
        dh = jnp.zeros((TS, d), F32)
        for g, w in enumerate(POOL_WINDOWS):
            cols = slice(g * gw, (g + 1) * gw)
            dgated = _dot_nt(dy, wout_s[g])
            mm = _dot(mixed_ref[:, cols], wg_s[g])
            sc = sc_ref[:, cols]
            ms = mm * sc
            z_g = z_ref[:, cols]
            sz, dsz = _silu_parts(z_g)
            gated_ref[:, cols] = (ms * sz).astype(BF16)
            dms = dgated * sz
            dz = (dgated * ms * dsz).astype(BF16)
            dsc_ref[:, cols] += jnp.sum(dms * mm, axis=0, keepdims=True)
            dmm = (dms * sc).astype(BF16)
            dmm_ref[:, cols] = dmm
            dmixed = _dot_nt(dmm, wg_s[g])
            e = dmixed * _inv_count(tile, TS, w)
            ext = jnp.concatenate([e, halo_s[g]], axis=0)
            shift = 1
            while shift < w:
                ext = ext + pltpu.roll(ext, TS + HALO - shift, 0)
                shift *= 2
            da = (ext[:TS] - dmixed).astype(BF16)
            halo_s[g] = e[:HALO]
            du_ref[:, cols] = da
            du_ref[:, pw + g * gw:pw + (g + 1) * gw] = dz
            wa_blk = win_s[g // 2, :, (g % 2) * gw:(g % 2 + 1) * gw]
            wz_blk = win_s[2 + g // 2, :, (g % 2) * gw:(g % 2 + 1) * gw]
            dh = dh + _dot_nt(da, wa_blk) + _dot_nt(dz, wz_blk)
        xt = x_ref[...]
        r_x = _rms(xt)
        xh = xt * r_x
        dgpre_ref[...] += jnp.sum(dh * xh, axis=0, keepdims=True)
        gx_ref[...] = g_in + _rms_bwd(dh * gpre_ref[...], xh, r_x)

    tile_spec = lambda wdt: pl.BlockSpec((TS, wdt), lambda i: (nt - 1 - i, 0))
    row = lambda wdt: pl.BlockSpec((1, wdt), lambda i: (0, 0))
    return pl.pallas_call(
        body, name="l0_bwd", grid=(nt,),
        in_specs=[tile_spec(d), tile_spec(d), tile_spec(d), tile_spec(pw), tile_spec(pw),
                  row(d), row(d), row(pw), ANY, ANY],
        out_specs=[tile_spec(d), tile_spec(2 * pw), tile_spec(pw), tile_spec(d), tile_spec(pw),
                   row(d), row(d), row(pw)],
        out_shape=[jax.ShapeDtypeStruct((s_len, d), F32), jax.ShapeDtypeStruct((s_len, 2 * pw), BF16),
                   jax.ShapeDtypeStruct((s_len, pw), BF16), jax.ShapeDtypeStruct((s_len, d), BF16),
                   jax.ShapeDtypeStruct((s_len, pw), BF16),
                   jax.ShapeDtypeStruct((1, d), F32), jax.ShapeDtypeStruct((1, d), F32),
                   jax.ShapeDtypeStruct((1, pw), F32)],
        scratch_shapes=[pltpu.VMEM((N_CHIPS, d, d), BF16), pltpu.VMEM((N_CHIPS, gw, d), BF16),
                        pltpu.VMEM((N_CHIPS, gw, gw), BF16), pltpu.VMEM((N_CHIPS, HALO, gw), F32),
                        pltpu.SemaphoreType.DMA((2 + N_CHIPS,))],
        compiler_params=_params(("arbitrary",)),
    )(dx1, x, y, mixed, z, g_pre, g_post, scale, wa, wg)


def _l1_inproj(x1, g_pre, wa):
    s_len, d = x1.shape
    nt = s_len // TS
    npad = PAD // TS

    def body(x_ref, gpre_ref, wa_ref, h_ref, q_ref, k_ref, v_ref, z_ref, win_s, sem):
        i = pl.program_id(0)

        @pl.when(i == 0)
        def _():
            _start_weight_copies([pltpu.make_async_copy(wa_ref.at[:, pl.ds(d, d), :], win_s, sem.at[0])])

        @pl.when(i < npad)
        def _():
            k_ref[...] = jnp.zeros_like(k_ref)
            v_ref[...] = jnp.zeros_like(v_ref)

        @pl.when(i >= npad)
        def _():
            xt = x_ref[...]
            h = ((xt * _rms(xt)) * gpre_ref[...]).astype(BF16)
            h_ref[...] = h
            q_ref[...] = _dot(h, win_s[0]).astype(BF16)
            k_ref[...] = _dot(h, win_s[1]).astype(BF16)
            v_ref[...] = _dot(h, win_s[2]).astype(BF16)
            z_ref[...] = _dot(h, win_s[3])

    tile = pl.BlockSpec((TS, d), lambda i: (jnp.maximum(i - npad, 0), 0))
    padded = pl.BlockSpec((TS, d), lambda i: (i, 0))
    return pl.pallas_call(
        body, name="l1_inproj", grid=(nt + npad,),
        in_specs=[tile, pl.BlockSpec((1, d), lambda i: (0, 0)), ANY],
        out_specs=[tile, tile, padded, padded, tile],
        out_shape=[jax.ShapeDtypeStruct((s_len, d), BF16), jax.ShapeDtypeStruct((s_len, d), BF16),
                   jax.ShapeDtypeStruct((PAD + s_len, d), BF16), jax.ShapeDtypeStruct((PAD + s_len, d), BF16),
                   jax.ShapeDtypeStruct((s_len, d), F32)],
        scratch_shapes=[pltpu.VMEM((N_CHIPS, d, d), BF16), pltpu.SemaphoreType.DMA((1,))],
        compiler_params=_params(("arbitrary",)),
    )(x1, g_pre, wa)


def _bias_tiles(rel_bias):
    i = jnp.arange(QB)[:, None]
    j = jnp.arange(KB)[None, :]
    idx = jnp.clip(i - j + PAD, -MAX_REL, MAX_REL) + MAX_REL
    first = (i // CHUNK) * CHUNK
    inside = (j >= first) & (j < first + PAD + CHUNK)
    return jnp.where(inside[None], rel_bias.astype(F32)[:, idx], NEG)


def _softmax_rows(q_m, kb, bias, colvalid):
    s = _dot_nt(q_m, kb) * (HEAD_DIM ** -0.5) + bias
    s = jnp.where(colvalid, s, NEG)
    p = jnp.exp(s - jnp.max(s, axis=-1, keepdims=True))
    return p * (1.0 / jnp.sum(p, axis=-1, keepdims=True))


def _attn_fwd(q, k, v, bias):
    s_len, d = q.shape
    nhp = d // HEAD_PAIR
    nq = s_len // QB

    def body(q_ref, k_ref, v_ref, b_ref, o_ref):
        start = pl.multiple_of(pl.program_id(1) * QB, QB)
        qt = q_ref[...]
        kb = k_ref[pl.ds(start, KB), :]
        vb = v_ref[pl.ds(start, KB), :]
        head_of_lane = lax.broadcasted_iota(jnp.int32, (1, HEAD_PAIR), 1) // HEAD_DIM
        colvalid = (start + lax.broadcasted_iota(jnp.int32, (1, KB), 1)) >= PAD
        outs = []
        for hh in range(2):
            q_m = jnp.where(head_of_lane == hh, qt, jnp.zeros_like(qt))
            p = _softmax_rows(q_m, kb, b_ref[hh], colvalid)
            outs.append(_dot(p.astype(BF16), vb))
        o_ref[...] = jnp.where(head_of_lane == 0, outs[0], outs[1])

    return pl.pallas_call(
        body, name="attn_fwd", grid=(nhp, nq),
        in_specs=[pl.BlockSpec((QB, HEAD_PAIR), lambda hp, b: (b, hp)),
                  pl.BlockSpec((PAD + s_len, HEAD_PAIR), lambda hp, b: (0, hp)),
                  pl.BlockSpec((PAD + s_len, HEAD_PAIR), lambda hp, b: (0, hp)),
                  pl.BlockSpec((2, QB, KB), lambda hp, b: (hp, 0, 0))],
        out_specs=pl.BlockSpec((QB, HEAD_PAIR), lambda hp, b: (b, hp)),
        out_shape=jax.ShapeDtypeStruct((s_len, d), F32),
        compiler_params=_params(("arbitrary", "arbitrary")),
    )(q, k, v, bias)


def _attn_bwd(q, k, v, bias, d_o):
    s_len, d = q.shape
    nhp = d // HEAD_PAIR
    nq = s_len // QB

    def body(q_ref, k_ref, v_ref, b_ref, do_ref, dq_ref, dk_ref, dv_ref, db_ref):
        b = pl.program_id(1)
        start = pl.multiple_of(b * QB, QB)

        @pl.when(b == 0)
        def _():
            dk_ref[...] = jnp.zeros_like(dk_ref)
            dv_ref[...] = jnp.zeros_like(dv_ref)
            db_ref[...] = jnp.zeros_like(db_ref)

        qt = q_ref[...]
        dot_ = do_ref[...]
        kb = k_ref[pl.ds(start, KB), :]
        vb = v_ref[pl.ds(start, KB), :]
        head_of_lane = lax.broadcasted_iota(jnp.int32, (1, HEAD_PAIR), 1) // HEAD_DIM
        colvalid = (start + lax.broadcasted_iota(jnp.int32, (1, KB), 1)) >= PAD
        dq_heads = []
        dk_band = jnp.zeros((KB, HEAD_PAIR), F32)
        dv_band = jnp.zeros((KB, HEAD_PAIR), F32)
        for hh in range(2):
            mine = head_of_lane == hh
            q_m = jnp.where(mine, qt, jnp.zeros_like(qt))
            do_m = jnp.where(mine, dot_, jnp.zeros_like(dot_))
            p = _softmax_rows(q_m, kb, b_ref[hh], colvalid)
            dv_band = dv_band + _dot_tn(p.astype(BF16), do_m)
            dp = _dot_nt(do_m, vb)
            ds = p * (dp - jnp.sum(p * dp, axis=-1, keepdims=True))
            db_ref[hh] += ds
            ds_b = (ds * (HEAD_DIM ** -0.5)).astype(BF16)
            dq_heads.append(_dot(ds_b, kb))
            dk_band = dk_band + _dot_tn(ds_b, q_m)
        dq_ref[...] = jnp.where(head_of_lane == 0, dq_heads[0], dq_heads[1]).astype(BF16)
        dk_ref[pl.ds(start, KB), :] += dk_band
        dv_ref[pl.ds(start, KB), :] += dv_band

    blk = pl.BlockSpec((QB, HEAD_PAIR), lambda hp, b: (b, hp))
    whole = pl.BlockSpec((PAD + s_len, HEAD_PAIR), lambda hp, b: (0, hp))
    btile = pl.BlockSpec((2, QB, KB), lambda hp, b: (hp, 0, 0))
    return pl.pallas_call(
        body, name="attn_bwd", grid=(nhp, nq),
        in_specs=[blk, whole, whole, btile, blk],
        out_specs=[blk, whole, whole, btile],
        out_shape=[jax.ShapeDtypeStruct((s_len, d), BF16),
                   jax.ShapeDtypeStruct((PAD + s_len, d), F32), jax.ShapeDtypeStruct((PAD + s_len, d), F32),
                   jax.ShapeDtypeStruct(bias.shape, F32)],
        compiler_params=_params(("arbitrary", "arbitrary")),
    )(q, k, v, bias, d_o)


def _bias_diagonals(db):
    nh = db.shape[0]
    width = 1024
    assert QB + KB <= width

    def body(db_ref, out_ref):
        row = lax.broadcasted_iota(jnp.int32, (QB, 1), 0)
        amount = QB - 1 - row
        for h in range(nh):
            m = jnp.concatenate([db_ref[h], jnp.zeros((QB, width - KB), F32)], axis=1)
            m = pltpu.roll(m, 1, 1)
            bit = 1
            while bit < QB:
                m = jnp.where((amount & bit) != 0, pltpu.roll(m, bit, 1), m)
                bit *= 2
            out_ref[pl.ds(h, 1), :] = jnp.sum(m, axis=0, keepdims=True)

    return pl.pallas_call(
        body, name="bias_diagonals",
        in_specs=[pl.BlockSpec(memory_space=pltpu.VMEM)],
        out_specs=pl.BlockSpec(memory_space=pltpu.VMEM),
        out_shape=jax.ShapeDtypeStruct((nh, width), F32),
        compiler_params=pltpu.CompilerParams(vmem_limit_bytes=VMEM_LIMIT),
    )(db)


def _rel_bias_grad(diag):
    nh, width = diag.shape
    top = QB + PAD
    r = jnp.arange(1, 2 * MAX_REL)
    mid = diag[:, top + MAX_REL - r]
    last = jnp.sum(diag[:, :top - MAX_REL + 1], axis=1, keepdims=True)
    first = jnp.sum(diag[:, top + MAX_REL:], axis=1, keepdims=True)
    return jnp.concatenate([first, mid, last], axis=1)


def _l1_out(o, z, x1, target, g_post, wa):
    s_len, d = o.shape
    nt = s_len // TS
    slab = d // N_CHIPS

    def body(o_ref, z_ref, x1_ref, t_ref, gpost_ref, wa_ref,
             g2_ref, gated_ref, dy_ref, do_ref, dz_ref, dgpost_ref, sq_ref, wout_s, sems):
        i = pl.program_id(0)

        @pl.when(i == 0)
        def _():
            _start_weight_copies([
                pltpu.make_async_copy(wa_ref.at[j, pl.ds(2 * d + d // 2, slab), :],
                                      wout_s.at[pl.ds(j * slab, slab), :], sems.at[j])
                for j in range(N_CHIPS)])
            dgpost_ref[...] = jnp.zeros_like(dgpost_ref)
            sq_ref[...] = jnp.zeros_like(sq_ref)

        ot = o_ref[...]
        sz, dsz = _silu_parts(z_ref[...])
        gated = (ot * sz).astype(BF16)
        gated_ref[...] = gated
        y = _dot(gated, wout_s[...])
        r_y = _rms(y)
        yh = y * r_y
        err = x1_ref[...] + yh * gpost_ref[...] - t_ref[...]
        sq_ref[...] += jnp.sum(err * err, axis=0, keepdims=True)
        g2 = err * (1.0 / d)
        g2_ref[...] = g2
        dgpost_ref[...] += jnp.sum(g2 * yh, axis=0, keepdims=True)
        dy = _rms_bwd(g2 * gpost_ref[...], yh, r_y).astype(BF16)
        dy_ref[...] = dy
        dgated = _dot_nt(dy, wout_s[...])
        do_ref[...] = (dgated * sz).astype(BF16)
        dz_ref[...] = (dgated * ot * dsz).astype(BF16)

    tile = pl.BlockSpec((TS, d), lambda i: (i, 0))
    row = pl.BlockSpec((1, d), lambda i: (0, 0))
    return pl.pallas_call(
        body, name="l1_out", grid=(nt,),
        in_specs=[tile, tile, tile, tile, row, ANY],
        out_specs=[tile, tile, tile, tile, tile, row, row],
        out_shape=[jax.ShapeDtypeStruct((s_len, d), F32)] + [jax.ShapeDtypeStruct((s_len, d), BF16)] * 4
                  + [jax.ShapeDtypeStruct((1, d), F32)] * 2,
        scratch_shapes=[pltpu.VMEM((d, d), BF16), pltpu.SemaphoreType.DMA((N_CHIPS,))],
        compiler_params=_params(("arbitrary",)),
    )(o, z, x1, target, g_post, wa)


def _l1_in_bwd(dq, dk, dv, dz, x1, g2, g_pre, wa):
    s_len, d = x1.shape
    nt = s_len // TS
    npad = PAD // TS

    def body(dq_ref, dk_ref, dv_ref, dz_ref, x_ref, g2_ref, gpre_ref, wa_ref,
             dx_ref, du_ref, dgpre_ref, win_s, sem):
        i = pl.program_id(0)

        @pl.when(i == 0)
        def _():
            _start_weight_copies([pltpu.make_async_copy(wa_ref.at[:, pl.ds(d, d), :], win_s, sem.at[0])])
            dgpre_ref[...] = jnp.zeros_like(dgpre_ref)

        parts = [dq_ref[...], dk_ref[...].astype(BF16), dv_ref[...].astype(BF16), dz_ref[...]]
        dh = jnp.zeros((TS, d), F32)
        for j, part in enumerate(parts):
            du_ref[:, j * d:(j + 1) * d] = part
            dh = dh + _dot_nt(part, win_s[j])
        xt = x_ref[...]
        r_x = _rms(xt)
        xh = xt * r_x
        dgpre_ref[...] += jnp.sum(dh * xh, axis=0, keepdims=True)
        dx_ref[...] = g2_ref[...] + _rms_bwd(dh * gpre_ref[...], xh, r_x)

    tile = pl.BlockSpec((TS, d), lambda i: (i, 0))
    padded = pl.BlockSpec((TS, d), lambda i: (i + npad, 0))
    row = pl.BlockSpec((1, d), lambda i: (0, 0))
    return pl.pallas_call(
        body, name="l1_in_bwd", grid=(nt,),
        in_specs=[tile, padded, padded, tile, tile, tile, row, ANY],
        out_specs=[tile, pl.BlockSpec((TS, 4 * d), lambda i: (i, 0)), row],
        out_shape=[jax.ShapeDtypeStruct((s_len, d), F32), jax.ShapeDtypeStruct((s_len, 4 * d), BF16),
                   jax.ShapeDtypeStruct((1, d), F32)],
        scratch_shapes=[pltpu.VMEM((N_CHIPS, d, d), BF16), pltpu.SemaphoreType.DMA((1,))],
        compiler_params=_params(("arbitrary",)),
    )(dq, dk, dv, dz, x1, g2, g_pre, wa)


def _wgrad(a, b, name, n_out, a_width, b_width, a_block, b_block, out_rows=None):
    s_len = a.shape[0]
    tk = min(1024, s_len)
    nk = s_len // tk

    def body(a_ref, b_ref, out_ref):
        @pl.when(pl.program_id(1) == 0)
        def _():
            out_ref[...] = jnp.zeros_like(out_ref)

        out_ref[...] += _dot_tn(a_ref[...], b_ref[...]).reshape(out_ref.shape)

    if out_rows is None:
        out_shape = (n_out, a_width, b_width)
        out_spec = pl.BlockSpec((None, a_width, b_width), lambda n, kk: (n, 0, 0))
    else:
        out_shape = (a_width // out_rows, n_out, out_rows, b_width)
        out_spec = pl.BlockSpec((a_width // out_rows, None, out_rows, b_width), lambda n, kk: (0, n, 0, 0))
    return pl.pallas_call(
        body, name=name, grid=(n_out, nk),
        in_specs=[pl.BlockSpec((tk, a_width), lambda n, kk: (kk, a_block(n))),
                  pl.BlockSpec((tk, b_width), lambda n, kk: (kk, b_block(n)))],
        out_specs=out_spec,
        out_shape=jax.ShapeDtypeStruct(out_shape, F32),
        compiler_params=_params(("arbitrary", "arbitrary")),
    )(a, b)


def _place():
    x, y, c = lax.axis_index("x"), lax.axis_index("y"), lax.axis_index("c")
    others = [(1 - x, y), (x, 1 - y), (1 - x, 1 - y)]
    return x, y, c, others


def _gather_weights(shards):
    n = len(shards)

    def body(*refs):
        src, dst = refs[:n], refs[n:2 * n]
        send, recv, fwd_send, fwd_recv, local = refs[2 * n:]
        x, y, c, others = _place()
        me = 2 * x + y
        sibling = (x, y, 1 - c)
        mine = [pltpu.make_async_copy(src[a], dst[a].at[me], local.at[a]) for a in range(n)]
        for cp in mine:
            cp.start()

        def half(a, ref, cc):
            rows = ref.shape[0] // 2
            return ref.at[pl.ds(cc * rows, rows), :]

        first, passed, arrive, arrive_fwd = [], [], [], []
        for a in range(n):
            for k, (ox, oy) in enumerate(others):
                sem = a * 3 + k
                first.append(pltpu.make_async_remote_copy(
                    src_ref=half(a, src[a], c), dst_ref=half(a, dst[a].at[me], c),
                    send_sem=send.at[sem], recv_sem=recv.at[sem], device_id=(ox, oy, c), device_id_type=MESH))
                theirs = half(a, dst[a].at[2 * ox + oy], c)
                arrive.append(pltpu.make_async_remote_copy(
                    src_ref=theirs, dst_ref=theirs, send_sem=send.at[sem], recv_sem=recv.at[sem],
                    device_id=(ox, oy, c), device_id_type=MESH))
                passed.append(pltpu.make_async_remote_copy(
                    src_ref=theirs, dst_ref=theirs, send_sem=fwd_send.at[sem], recv_sem=fwd_recv.at[sem],
                    device_id=sibling, device_id_type=MESH))
                other_half = half(a, dst[a].at[2 * ox + oy], 1 - c)
                arrive_fwd.append(pltpu.make_async_remote_copy(
                    src_ref=other_half, dst_ref=other_half, send_sem=fwd_send.at[sem], recv_sem=fwd_recv.at[sem],
                    device_id=sibling, device_id_type=MESH))
        for cp in first:
            cp.start()
        for got, onward in zip(arrive, passed):
            got.wait_recv()
            onward.start()
        for got in arrive_fwd:
            got.wait_recv()
        for cp in first + passed:
            cp.wait_send()
        for cp in mine:
            cp.wait()

    return pl.pallas_call(
        body, name="gather_weights",
        in_specs=[ANY] * n, out_specs=[ANY] * n,
        out_shape=[jax.ShapeDtypeStruct((N_CHIPS,) + s.shape, s.dtype) for s in shards],
        scratch_shapes=[pltpu.SemaphoreType.DMA((3 * n,)), pltpu.SemaphoreType.DMA((3 * n,)),
                        pltpu.SemaphoreType.DMA((3 * n,)), pltpu.SemaphoreType.DMA((3 * n,)),
                        pltpu.SemaphoreType.DMA((n,))],
    )(*shards)


def _swap_halves(arrays, name, send_half, in_place):
    n = len(arrays)

    def body(*refs):
        src, dst = refs[:n], refs[n:2 * n]
        if in_place:
            src = dst
        send, recv = refs[2 * n:]
        x, y, c, _ = _place()
        sibling = (x, y, 1 - c)
        out_copies, in_copies = [], []
        for a in range(n):
            rows = src[a].shape[-2] // 2
            lead = (slice(None),) * (len(src[a].shape) - 2)
            going = lead + (pl.ds(send_half(c) * rows, rows), slice(None))
            coming = lead + (pl.ds((1 - send_half(c)) * rows, rows), slice(None))
            out_copies.append(pltpu.make_async_remote_copy(
                src_ref=src[a].at[going], dst_ref=dst[a].at[going], send_sem=send.at[a], recv_sem=recv.at[a],
                device_id=sibling, device_id_type=MESH))
            in_copies.append(pltpu.make_async_remote_copy(
                src_ref=src[a].at[coming], dst_ref=dst[a].at[coming], send_sem=send.at[a], recv_sem=recv.at[a],
                device_id=sibling, device_id_type=MESH))
        for cp in out_copies:
            cp.start()
        for cp in in_copies:
            cp.wait_recv()
        for cp in out_copies:
            cp.wait_send()

    return pl.pallas_call(
        body, name=name,
        in_specs=[ANY] * n, out_specs=[ANY] * n,
        out_shape=[jax.ShapeDtypeStruct(a.shape, a.dtype) for a in arrays],
        scratch_shapes=[pltpu.SemaphoreType.DMA((n,)), pltpu.SemaphoreType.DMA((n,))],
        input_output_aliases={a: a for a in range(n)} if in_place else {},
    )(*arrays)


def _scatter_to_chips(arrays):
    n = len(arrays)

    def body(*refs):
        src, dst = refs[:n], refs[n:2 * n]
        send, recv = refs[2 * n:]
        x, y, c, others = _place()
        out_copies, in_copies = [], []
        for a in range(n):
            rows = src[a].shape[1] // 2
            mine = pl.ds(c * rows, rows)
            for k, (ox, oy) in enumerate(others):
                sem = a * 3 + k
                out_copies.append(pltpu.make_async_remote_copy(
                    src_ref=src[a].at[2 * ox + oy, mine, :], dst_ref=dst[a].at[k, mine, :],
                    send_sem=send.at[sem], recv_sem=recv.at[sem], device_id=(ox, oy, c), device_id_type=MESH))
                in_copies.append(pltpu.make_async_remote_copy(
                    src_ref=dst[a].at[k, mine, :], dst_ref=dst[a].at[k, mine, :],
                    send_sem=send.at[sem], recv_sem=recv.at[sem], device_id=(ox, oy, c), device_id_type=MESH))
        for cp in out_copies:
            cp.start()
        for cp in in_copies:
            cp.wait_recv()
        for cp in out_copies:
            cp.wait_send()

    return pl.pallas_call(
        body, name="scatter_to_chips",
        in_specs=[ANY] * n, out_specs=[ANY] * n,
        out_shape=[jax.ShapeDtypeStruct((3,) + a.shape[1:], a.dtype) for a in arrays],
        scratch_shapes=[pltpu.SemaphoreType.DMA((3 * n,)), pltpu.SemaphoreType.DMA((3 * n,))],
    )(*arrays)


def _allreduce_small(part):
    rows, width = part.shape

    def body(p_ref, out_ref, all_ref, send, recv):
        x, y, c, _ = _place()
        me = 4 * x + 2 * y + c
        all_ref[me] = p_ref[...]
        copies = []
        for k in range(1, N_DEV):
            px, py, pc = x ^ (k >> 2), y ^ ((k >> 1) & 1), c ^ (k & 1)
            copies.append(pltpu.make_async_remote_copy(
                src_ref=p_ref, dst_ref=all_ref.at[me], send_sem=send.at[k - 1], recv_sem=recv.at[k - 1],
                device_id=(px, py, pc), device_id_type=MESH))
        for cp in copies:
            cp.start()
        for cp in copies:
            cp.wait()
        total = all_ref[0]
        for k in range(1, N_DEV):
            total = total + all_ref[k]
        out_ref[...] = total

    return pl.pallas_call(
        body, name="allreduce_small",
        in_specs=[pl.BlockSpec(memory_space=pltpu.VMEM)],
        out_specs=pl.BlockSpec(memory_space=pltpu.VMEM),
        out_shape=jax.ShapeDtypeStruct((rows, width), F32),
        scratch_shapes=[pltpu.VMEM((N_DEV, rows, width), F32),
                        pltpu.SemaphoreType.DMA((N_DEV - 1,)), pltpu.SemaphoreType.DMA((N_DEV - 1,))],
    )(part)


def _row_tile(rows):
    t = min(rows, 256)
    while rows % t:
        t //= 2
    return t


def _core_and_chip():
    return jnp.stack([lax.axis_index("c"), 2 * lax.axis_index("x") + lax.axis_index("y")]).astype(jnp.int32)


def _sum_siblings(own, got, name):
    _, rows, cols = own.shape
    half = rows // 2
    t = _row_tile(half)
    nb = half // t

    def body(place_ref, own_ref, got_ref, mine_ref, out_ref):
        total = own_ref[...] + got_ref[...]
        out_ref[...] = total.astype(BF16)

        @pl.when(pl.program_id(1) == place_ref[1])
        def _():
            mine_ref[...] = total

    slab_blk = pl.BlockSpec((None, t, cols), lambda r, j, place: (j, place[0] * nb + r, 0))
    return pl.pallas_call(
        body, name=name,
        grid_spec=pltpu.PrefetchScalarGridSpec(
            num_scalar_prefetch=1, grid=(nb, N_CHIPS),
            in_specs=[slab_blk, slab_blk],
            out_specs=[pl.BlockSpec((t, cols), lambda r, j, place: (r, 0)), slab_blk]),
        out_shape=[jax.ShapeDtypeStruct((half, cols), F32), jax.ShapeDtypeStruct(own.shape, BF16)],
        compiler_params=_params(("arbitrary", "arbitrary")),
    )(_core_and_chip(), own, got)


def _sum_chips(mine, got, name):
    half, cols = mine.shape
    t = _row_tile(half)
    nb = half // t

    def body(place_ref, mine_ref, got_ref, out_ref):
        total = mine_ref[...]
        for k in range(3):
            total = total + got_ref[k].astype(F32)
        out_ref[...] = total

    return pl.pallas_call(
        body, name=name,
        grid_spec=pltpu.PrefetchScalarGridSpec(
            num_scalar_prefetch=1, grid=(nb,),
            in_specs=[pl.BlockSpec((t, cols), lambda r, place: (r, 0)),
                      pl.BlockSpec((3, t, cols), lambda r, place: (0, place[0] * nb + r, 0))],
            out_specs=pl.BlockSpec((t, cols), lambda r, place: (place[0] * nb + r, 0))),
        out_shape=jax.ShapeDtypeStruct((2 * half, cols), F32),
        compiler_params=_params(("arbitrary",)),
    )(_core_and_chip(), mine, got)


def _adamw(w, g, m, v, name, g_row0=0):
    rows, cols = w.shape
    t = _row_tile(rows)
    assert g_row0 % t == 0
    off = g_row0 // t

    def body(w_ref, g_ref, m_ref, v_ref, go_ref, d_ref, mo_ref, vo_ref):
        g_t = g_ref[...]
        m_new = ADAM_B1 * m_ref[...] + (1.0 - ADAM_B1) * g_t
        v_new = ADAM_B2 * v_ref[...] + (1.0 - ADAM_B2) * (g_t * g_t)
        m_hat = m_new / (1.0 - ADAM_B1 ** ADAM_STEP)
        v_hat = v_new / (1.0 - ADAM_B2 ** ADAM_STEP)
        go_ref[...] = g_t
        d_ref[...] = -ADAM_LR * (m_hat / (jnp.sqrt(v_hat) + ADAM_EPS) + ADAM_WD * w_ref[...])
        mo_ref[...] = m_new
        vo_ref[...] = v_new

    blk = pl.BlockSpec((t, cols), lambda r: (r, 0))
    return pl.pallas_call(
        body, name=name, grid=(rows // t,),
        in_specs=[blk, pl.BlockSpec((t, cols), lambda r: (r + off, 0)), blk, blk],
        out_specs=[blk] * 4,
        out_shape=[jax.ShapeDtypeStruct((rows, cols), F32)] * 4,
        compiler_params=_params(("arbitrary",)),
    )(w, g, m, v)


def _pack_small(d, vectors):
    rows = []
    for vec in vectors:
        flat = vec.reshape(-1)
        n_rows = -(-flat.shape[0] // d)
        rows.append(jnp.pad(flat, (0, n_rows * d - flat.shape[0])).reshape(n_rows, d))
    return jnp.concatenate(rows, axis=0)


def _unpack_small(packed, d, shapes):
    out, row = [], 0
    for shape in shapes:
        size = 1
        for s in shape:
            size *= s
        n_rows = -(-size // d)
        out.append(packed[row:row + n_rows].reshape(-1)[:size].reshape(shape))
        row += n_rows
    return out


def kernel(x, norm_pre, norm_post, pool_w_in, pool_w_group, pool_scale, pool_w_out, att_w_in, att_rel_bias, att_w_out, loss_target, m_norm_pre, m_norm_post, m_pool_w_in, m_pool_w_group, m_pool_scale, m_pool_w_out, m_att_w_in, m_att_rel_bias, m_att_w_out, v_norm_pre, v_norm_post, v_pool_w_in, v_pool_w_group, v_pool_scale, v_pool_w_out, v_att_w_in, v_att_rel_bias, v_att_w_out):
    _, s_len, d = x.shape
    gw = d // 2
    q = gw // N_CHIPS
    x2d = x.reshape(s_len, d)
    target = loss_target.reshape(s_len, d)

    def pack_a(p_in, a_in, p_out, a_out):
        return jnp.concatenate([p_in[0], a_in[0], p_out[0], a_out[0]], axis=0)

    def pack_g(p_group):
        return p_group.reshape(N_CHIPS * q, gw)

    wa_shard = pack_a(pool_w_in, att_w_in, pool_w_out, att_w_out)
    wg_shard = pack_g(pool_w_group)
    wa, wg = _gather_weights([wa_shard.astype(BF16), wg_shard.astype(BF16)])
    wg = wg.reshape(N_CHIPS, N_CHIPS, q, gw)

    x1, h0, mixed, z0, y0 = _l0_fwd(x2d, norm_pre[0:1], norm_post[0:1], pool_scale, wa, wg)
    h1, q_, k_, v_, z1 = _l1_inproj(x1, norm_pre[1:2], wa)
    bias = _bias_tiles(att_rel_bias[0])
    o = _attn_fwd(q_, k_, v_, bias)

    g2, gated1, dy1, d_o, dz1, dgpost1, sq = _l1_out(o, z1, x1, target, norm_post[1:2], wa)
    dq, dk, dv, dbias = _attn_bwd(q_, k_, v_, bias, d_o)
    dx1, du1, dgpre1 = _l1_in_bwd(dq, dk, dv, dz1, x1, g2, norm_pre[1:2], wa)
    grad_x, du0, gated0, dy0, dmm, dgpre0, dgpost0, dscale = _l0_bwd(
        dx1, x2d, y0, mixed, z0, norm_pre[0:1], norm_post[0:1], pool_scale, wa, wg)
    d_rel = _rel_bias_grad(_bias_diagonals(dbias))

    blk = lambda n: n
    zero = lambda n: 0
    g_pin = _wgrad(h0, du0, "wgrad_pool_in", N_CHIPS, d, d, zero, blk)
    g_ain = _wgrad(h1, du1, "wgrad_att_in", N_CHIPS, d, d, zero, blk)
    g_pout = _wgrad(gated0, dy0, "wgrad_pool_out", N_CHIPS, gw, d, blk, zero)
    g_aout = _wgrad(gated1, dy1, "wgrad_att_out", 1, d, d, zero, zero).reshape(N_CHIPS, d // N_CHIPS, d)
    g_pg = _wgrad(mixed, dmm, "wgrad_pool_group", N_CHIPS, gw, gw, blk, blk, out_rows=q)
    ga = jnp.concatenate([g_pin, g_ain, g_pout, g_aout], axis=1)
    gg = g_pg.reshape(N_CHIPS, N_CHIPS * q, gw)

    got_a, got_g = _swap_halves([ga, gg], "swap_core_partials", lambda c: 1 - c, in_place=False)
    mine_a, send_a = _sum_siblings(ga, got_a, "sum_siblings_a")
    mine_g, send_g = _sum_siblings(gg, got_g, "sum_siblings_g")
    recv_a, recv_g = _scatter_to_chips([send_a, send_g])
    red_a = _sum_chips(mine_a, recv_a, "sum_chips_a")
    red_g = _sum_chips(mine_g, recv_g, "sum_chips_g")
    grad_a, grad_g = _swap_halves([red_a, red_g], "swap_reduced_halves", lambda c: c, in_place=True)

    small_shapes = [norm_pre.shape, norm_post.shape, pool_scale.shape, att_rel_bias.shape]
    part = _pack_small(d, [jnp.concatenate([dgpre0, dgpre1], axis=0), jnp.concatenate([dgpost0, dgpost1], axis=0),
                           dscale, d_rel, sq])
    total = _allreduce_small(part)
    loss = (0.5 / d) * jnp.sum(total[-1])
    w_small = _pack_small(d, [norm_pre, norm_post, pool_scale, att_rel_bias, jnp.zeros((d,), F32)])
    m_small = _pack_small(d, [m_norm_pre, m_norm_post, m_pool_scale, m_att_rel_bias, jnp.zeros((d,), F32)])
    v_small = _pack_small(d, [v_norm_pre, v_norm_post, v_pool_scale, v_att_rel_bias, jnp.ones((d,), F32)])
    small_out = [_unpack_small(a, d, small_shapes) for a in _adamw(w_small, total, m_small, v_small, "adamw_small")]

    big = {}
    row0 = 0
    for name, w, m, v in [("pool_w_in", pool_w_in, m_pool_w_in, v_pool_w_in),
                          ("att_w_in", att_w_in, m_att_w_in, v_att_w_in),
                          ("pool_w_out", pool_w_out, m_pool_w_out, v_pool_w_out),
                          ("att_w_out", att_w_out, m_att_w_out, v_att_w_out)]:
        outs = _adamw(w[0], grad_a, m[0], v[0], "adamw_" + name, g_row0=row0)
        big[name] = [a.reshape(w.shape) for a in outs]
        row0 += w.shape[1]
    outs = _adamw(pack_g(pool_w_group), grad_g, pack_g(m_pool_w_group), pack_g(v_pool_w_group), "adamw_pool_w_group")
    big["pool_w_group"] = [a.reshape(pool_w_group.shape) for a in outs]

    def leaf(kind):
        return (small_out[kind][0], small_out[kind][1], big["pool_w_in"][kind], big["pool_w_group"][kind],
                small_out[kind][2], big["pool_w_out"][kind], big["att_w_in"][kind], small_out[kind][3],
                big["att_w_out"][kind])

    return (loss, grad_x.reshape(x.shape), *leaf(0), *leaf(1), *leaf(2), *leaf(3))
```

```python
import functools

import jax
import jax.numpy as jnp
from jax import lax
from jax.experimental import pallas as pl
from jax.experimental.pallas import tpu as pltpu

F32 = jnp.float32
BF16 = jnp.bfloat16

RMS_EPS = 1e-6
CHUNK = 64
HEAD_DIM = 64
LEFT_CHUNKS = 8
PAD = LEFT_CHUNKS * CHUNK
MAX_REL = 256
POOL_WINDOWS = (2, 4, 8, 16)
HALO = 16
TS = 256
QB = 256
KB = QB + PAD
HEAD_PAIR = 2 * HEAD_DIM
NEG = -1e30
N_CHIPS = 4
N_DEV = 8

ADAM_LR = 0.001
ADAM_B1 = 0.9
ADAM_B2 = 0.999
ADAM_EPS = 1e-08
ADAM_WD = 0.01
ADAM_STEP = 10

VMEM_LIMIT = 56 * 1024 * 1024
MESH = pl.DeviceIdType.MESH
ANY = pl.BlockSpec(memory_space=pl.ANY)


def _dot(a, b):
    return jnp.dot(a, b, preferred_element_type=F32)


def _dot_nt(a, b):
    return lax.dot_general(a, b, (((1,), (1,)), ((), ())), preferred_element_type=F32)


def _dot_tn(a, b):
    return lax.dot_general(a, b, (((0,), (0,)), ((), ())), preferred_element_type=F32)


def _params(sem, limit=VMEM_LIMIT):
    return pltpu.CompilerParams(dimension_semantics=sem, vmem_limit_bytes=limit)


def _rms(x):
    return lax.rsqrt(jnp.mean(x * x, axis=-1, keepdims=True) + RMS_EPS)


def _rms_bwd(dyh, xh, r):
    return r * (dyh - xh * jnp.mean(dyh * xh, axis=-1, keepdims=True))


def _silu_parts(z):
    sig = jax.nn.sigmoid(z)
    return z * sig, sig * (1.0 + z * (1.0 - sig))


def _inv_count(tile, rows, w):
    t = tile * rows + lax.broadcasted_iota(jnp.int32, (rows, 1), 0)
    return 1.0 / jnp.minimum(t + 1, w).astype(F32)


def _start_weight_copies(copies):
    for c in copies:
        c.start()
    for c in copies:
        c.wait()


def _l0_fwd(x, g_pre, g_post, scale, wa, wg):
    s_len, d = x.shape
    pw, gw = 2 * d, d // 2
    q = gw // N_CHIPS
    nt = s_len // TS

    def body(x_ref, gpre_ref, gpost_ref, sc_ref, wa_ref, wg_ref,
             x1_ref, h_ref, mixed_ref, z_ref, y_ref,
             win_s, wout_s, wg_s, halo_s, sems):
        i = pl.program_id(0)

        @pl.when(i == 0)
        def _():
            copies = [pltpu.make_async_copy(wa_ref.at[:, pl.ds(0, d), :], win_s, sems.at[0]),
                      pltpu.make_async_copy(wa_ref.at[:, pl.ds(2 * d, gw), :], wout_s, sems.at[1])]
            copies += [pltpu.make_async_copy(wg_ref.at[j], wg_s.at[:, pl.ds(j * q, q), :], sems.at[2 + j])
                       for j in range(N_CHIPS)]
            _start_weight_copies(copies)
            halo_s[...] = jnp.zeros_like(halo_s)

        xt = x_ref[...]
        h = ((xt * _rms(xt)) * gpre_ref[...]).astype(BF16)
        h_ref[...] = h
        a_blocks = [_dot(h, win_s[0]), _dot(h, win_s[1])]
        y = jnp.zeros((TS, d), F32)
        for g, w in enumerate(POOL_WINDOWS):
            a_g = a_blocks[g // 2][:, (g % 2) * gw:(g % 2 + 1) * gw]
            ext = jnp.concatenate([halo_s[g], a_g], axis=0)
            shift = 1
            while shift < w:
                ext = ext + pltpu.roll(ext, shift, 0)
                shift *= 2
            mixed = (ext[HALO:] * _inv_count(i, TS, w) - a_g).astype(BF16)
            halo_s[g] = a_g[TS - HALO:]
            mixed_ref[:, g * gw:(g + 1) * gw] = mixed
            z_g = _dot(h, win_s[2 + g // 2, :, (g % 2) * gw:(g % 2 + 1) * gw])
            z_ref[:, g * gw:(g + 1) * gw] = z_g
            ms = _dot(mixed, wg_s[g]) * sc_ref[:, g * gw:(g + 1) * gw]
            gated = (ms * _silu_parts(z_g)[0]).astype(BF16)
            y = y + _dot(gated, wout_s[g])
        y_ref[...] = y
        x1_ref[...] = xt + (y * _rms(y)) * gpost_ref[...]

    tile = lambda wdt: pl.BlockSpec((TS, wdt), lambda i: (i, 0))
    row = lambda wdt: pl.BlockSpec((1, wdt), lambda i: (0, 0))
    return pl.pallas_call(
        body, name="l0_fwd", grid=(nt,),
        in_specs=[tile(d), row(d), row(d), row(pw), ANY, ANY],
        out_specs=[tile(d), tile(d), tile(pw), tile(pw), tile(d)],
        out_shape=[jax.ShapeDtypeStruct((s_len, d), F32), jax.ShapeDtypeStruct((s_len, d), BF16),
                   jax.ShapeDtypeStruct((s_len, pw), BF16), jax.ShapeDtypeStruct((s_len, pw), F32),
                   jax.ShapeDtypeStruct((s_len, d), F32)],
        scratch_shapes=[pltpu.VMEM((N_CHIPS, d, d), BF16), pltpu.VMEM((N_CHIPS, gw, d), BF16),
                        pltpu.VMEM((N_CHIPS, gw, gw), BF16), pltpu.VMEM((N_CHIPS, HALO, gw), F32),
                        pltpu.SemaphoreType.DMA((2 + N_CHIPS,))],
        compiler_params=_params(("arbitrary",)),
    )(x, g_pre, g_post, scale, wa, wg)


def _l0_bwd(dx1, x, y, mixed, z, g_pre, g_post, scale, wa, wg):
    s_len, d = x.shape
    pw, gw = 2 * d, d // 2
    q = gw // N_CHIPS
    nt = s_len // TS

    def body(dx1_ref, x_ref, y_ref, mixed_ref, z_ref, gpre_ref, gpost_ref, sc_ref, wa_ref, wg_ref,
             gx_ref, du_ref, gated_ref, dy_ref, dmm_ref, dgpre_ref, dgpost_ref, dsc_ref,
             win_s, wout_s, wg_s, halo_s, sems):
        i = pl.program_id(0)
        tile = nt - 1 - i

        @pl.when(i == 0)
        def _():
            copies = [pltpu.make_async_copy(wa_ref.at[:, pl.ds(0, d), :], win_s, sems.at[0]),
                      pltpu.make_async_copy(wa_ref.at[:, pl.ds(2 * d, gw), :], wout_s, sems.at[1])]
            copies += [pltpu.make_async_copy(wg_ref.at[j], wg_s.at[:, pl.ds(j * q, q), :], sems.at[2 + j])
                       for j in range(N_CHIPS)]
            _start_weight_copies(copies)
            halo_s[...] = jnp.zeros_like(halo_s)
            dgpre_ref[...] = jnp.zeros_like(dgpre_ref)
            dgpost_ref[...] = jnp.zeros_like(dgpost_ref)
            dsc_ref[...] = jnp.zeros_like(dsc_ref)

        g_in = dx1_ref[...]
        yt = y_ref[...]
        r_y = _rms(yt)
        yh = yt * r_y
        dgpost_ref[...] += jnp.sum(g_in * yh, axis=0, keepdims=True)
        dy = _rms_bwd(g_in * gpost_ref[...], yh, r_y).astype(BF16)
        dy_ref[...] = dy
        dh = jnp.zeros((TS, d), F32)
        for g, w in enumerate(POOL_WINDOWS):
            cols = slice(g * gw, (g + 1) * gw)
            dgated = _dot_nt(dy, wout_s[g])
            mm = _dot(mixed_ref[:, cols], wg_s[g])
            sc = sc_ref[:, cols]
            ms = mm * sc
            z_g = z_ref[:, cols]
            sz, dsz = _silu_parts(z_g)
            gated_ref[:, cols] = (ms * sz).astype(BF16)
            dms = dgated * sz
            dz = (dgated * ms * dsz).astype(BF16)
            dsc_ref[:, cols] += jnp.sum(dms * mm, axis=0, keepdims=True)
            dmm = (dms * sc).astype(BF16)
            dmm_ref[:, cols] = dmm
            dmixed = _dot_nt(dmm, wg_s[g])
            e = dmixed * _inv_count(tile, TS, w)
            ext = jnp.concatenate([e, halo_s[g]], axis=0)
            shift = 1
            while shift < w:
                ext = ext + pltpu.roll(ext, TS + HALO - shift, 0)
                shift *= 2
            da = (ext[:TS] - dmixed).astype(BF16)
            halo_s[g] = e[:HALO]
            du_ref[:, cols] = da
            du_ref[:, pw + g * gw:pw + (g + 1) * gw] = dz
            wa_blk = win_s[g // 2, :, (g % 2) * gw:(g % 2 + 1) * gw]
            wz_blk = win_s[2 + g // 2, :, (g % 2) * gw:(g % 2 + 1) * gw]
            dh = dh + _dot_nt(da, wa_blk) + _dot_nt(dz, wz_blk)
        xt = x_ref[...]
        r_x = _rms(xt)
        xh = xt * r_x
        dgpre_ref[...] += jnp.sum(dh * xh, axis=0, keepdims=True)
        gx_ref[...] = g_in + _rms_bwd(dh * gpre_ref[...], xh, r_x)

    tile_spec = lambda wdt: pl.BlockSpec((TS, wdt), lambda i: (nt - 1 - i, 0))
    row = lambda wdt: pl.BlockSpec((1, wdt), lambda i: (0, 0))
    return pl.pallas_call(
        body, name="l0_bwd", grid=(nt,),
        in_specs=[tile_spec(d), tile_spec(d), tile_spec(d), tile_spec(pw), tile_spec(pw),
                  row(d), row(d), row(pw), ANY, ANY],
        out_specs=[tile_spec(d), tile_spec(2 * pw), tile_spec(pw), tile_spec(d), tile_spec(pw),
                   row(d), row(d), row(pw)],
        out_shape=[jax.ShapeDtypeStruct((s_len, d), F32), jax.ShapeDtypeStruct((s_len, 2 * pw), BF16),
                   jax.ShapeDtypeStruct((s_len, pw), BF16), jax.ShapeDtypeStruct((s_len, d), BF16),
                   jax.ShapeDtypeStruct((s_len, pw), BF16),
                   jax.ShapeDtypeStruct((1, d), F32), jax.ShapeDtypeStruct((1, d), F32),
                   jax.ShapeDtypeStruct((1, pw), F32)],
        scratch_shapes=[pltpu.VMEM((N_CHIPS, d, d), BF16), pltpu.VMEM((N_CHIPS, gw, d), BF16),
                        pltpu.VMEM((N_CHIPS, gw, gw), BF16), pltpu.VMEM((N_CHIPS, HALO, gw), F32),
                        pltpu.SemaphoreType.DMA((2 + N_CHIPS,))],
        compiler_params=_params(("arbitrary",)),
    )(dx1, x, y, mixed, z, g_pre, g_post, scale, wa, wg)


def _l1_inproj(x1, g_pre, wa):
    s_len, d = x1.shape
    nt = s_len // TS
    npad = PAD // TS

    def body(x_ref, gpre_ref, wa_ref, h_ref, q_ref, k_ref, v_ref, z_ref, win_s, sem):
        i = pl.program_id(0)

        @pl.when(i == 0)
        def _():
            _start_weight_copies([pltpu.make_async_copy(wa_ref.at[:, pl.ds(d, d), :], win_s, sem.at[0])])

        @pl.when(i < npad)
        def _():
            k_ref[...] = jnp.zeros_like(k_ref)
            v_ref[...] = jnp.zeros_like(v_ref)

        @pl.when(i >= npad)
        def _():
            xt = x_ref[...]
            h = ((xt * _rms(xt)) * gpre_ref[...]).astype(BF16)
            h_ref[...] = h
            q_ref[...] = _dot(h, win_s[0]).astype(BF16)
            k_ref[...] = _dot(h, win_s[1]).astype(BF16)
            v_ref[...] = _dot(h, win_s[2]).astype(BF16)
            z_ref[...] = _dot(h, win_s[3])

    tile = pl.BlockSpec((TS, d), lambda i: (jnp.maximum(i - npad, 0), 0))
    padded = pl.BlockSpec((TS, d), lambda i: (i, 0))
    return pl.pallas_call(
        body, name="l1_inproj", grid=(nt + npad,),
        in_specs=[tile, pl.BlockSpec((1, d), lambda i: (0, 0)), ANY],
        out_specs=[tile, tile, padded, padded, tile],
        out_shape=[jax.ShapeDtypeStruct((s_len, d), BF16), jax.ShapeDtypeStruct((s_len, d), BF16),
                   jax.ShapeDtypeStruct((PAD + s_len, d), BF16), jax.ShapeDtypeStruct((PAD + s_len, d), BF16),
                   jax.ShapeDtypeStruct((s_len, d), F32)],
        scratch_shapes=[pltpu.VMEM((N_CHIPS, d, d), BF16), pltpu.SemaphoreType.DMA((1,))],
        compiler_params=_params(("arbitrary",)),
    )(x1, g_pre, wa)


SKEW = QB + KB


def _bias_tiles(rel_bias):
    nh = rel_bias.shape[0]
    assert QB == MAX_REL
    by_column = jnp.concatenate([jnp.broadcast_to(rel_bias[:, 2 * MAX_REL:], (nh, PAD + 1)),
                                 jnp.flip(rel_bias[:, 1:2 * MAX_REL], axis=1)], axis=1).astype(F32)

    def body(col_ref, out_ref):
        rows = jnp.broadcast_to(col_ref[pl.ds(pl.program_id(0), 1), :], (QB, SKEW))
        tile = pltpu.roll(rows, SKEW - QB, 1, stride=1, stride_axis=0)[:, :KB]
        i = lax.broadcasted_iota(jnp.int32, (QB, KB), 0)
        j = lax.broadcasted_iota(jnp.int32, (QB, KB), 1)
        first = (i // CHUNK) * CHUNK
        out_ref[...] = jnp.where((j >= first) & (j < first + PAD + CHUNK), tile, NEG)

    return pl.pallas_call(
        body, name="bias_tiles", grid=(nh,),
        in_specs=[pl.BlockSpec((nh, SKEW), lambda h: (0, 0))],
        out_specs=pl.BlockSpec((None, QB, KB), lambda h: (h, 0, 0)),
        out_shape=jax.ShapeDtypeStruct((nh, QB, KB), F32),
        compiler_params=_params(("arbitrary",)),
    )(by_column)


ROWS = 16


def _row_blocks():
    return [pl.ds(r * ROWS, ROWS) for r in range(QB // ROWS)]


def _attn_fwd(q, k, v, bias):
    s_len, d = q.shape
    nhp = d // HEAD_PAIR
    nq = s_len // QB

    def body(q_ref, k_ref, v_ref, b_ref, o_ref, lse_ref, s_s, p_s, l_s):
        start = pl.multiple_of(pl.program_id(1) * QB, QB)
        qt = q_ref[...] * (HEAD_DIM ** -0.5)
        kb = k_ref[pl.ds(start, KB), :]
        vb = v_ref[pl.ds(start, KB), :]
        head_of_lane = lax.broadcasted_iota(jnp.int32, (1, HEAD_PAIR), 1) // HEAD_DIM
        colvalid = (start + lax.broadcasted_iota(jnp.int32, (1, KB), 1)) >= PAD
        outs, lses = [], []
        for hh in range(2):
            s_s[...] = _dot_nt(jnp.where(head_of_lane == hh, qt, jnp.zeros_like(qt)), kb)
            for rows in _row_blocks():
                s = jnp.where(colvalid, s_s[rows, :] + b_ref[hh, rows, :], NEG)
                m = jnp.max(s, axis=-1, keepdims=True)
                e = jnp.exp(s - m)
                l = jnp.sum(e, axis=-1, keepdims=True)
                p_s[rows, :] = (e * (1.0 / l)).astype(BF16)
                l_s[rows, :] = m + jnp.log(l)
            outs.append(_dot(p_s[...], vb))
            lses.append(l_s[...])
        o_ref[...] = jnp.where(head_of_lane == 0, outs[0], outs[1])
        lse_ref[...] = jnp.where(head_of_lane == 0, lses[0], lses[1])

    blk = pl.BlockSpec((QB, HEAD_PAIR), lambda hp, b: (b, hp))
    whole = pl.BlockSpec((PAD + s_len, HEAD_PAIR), lambda hp, b: (0, hp))
    return pl.pallas_call(
        body, name="attn_fwd", grid=(nhp, nq),
        in_specs=[blk, whole, whole, pl.BlockSpec((2, QB, KB), lambda hp, b: (hp, 0, 0))],
        out_specs=[blk, blk],
        out_shape=[jax.ShapeDtypeStruct((s_len, d), F32), jax.ShapeDtypeStruct((s_len, d), F32)],
        scratch_shapes=[pltpu.VMEM((QB, KB), F32), pltpu.VMEM((QB, KB), BF16), pltpu.VMEM((QB, 1), F32)],
        compiler_params=_params(("arbitrary", "arbitrary")),
    )(q, k, v, bias)


def _attn_bwd(q, k, v, bias, d_o, o, lse):
    s_len, d = q.shape
    nhp = d // HEAD_PAIR
    nq = s_len // QB
    qk_scale = HEAD_DIM ** -0.5

    def body(q_ref, k_ref, v_ref, b_ref, do_ref, o_ref, lse_ref, dq_ref, dk_ref, dv_ref, db_ref,
             s_s, dp_s, p_s, ds_s, lse_s, delta_s):
        b = pl.program_id(1)
        start = pl.multiple_of(b * QB, QB)

        @pl.when(b == 0)
        def _():
            dk_ref[...] = jnp.zeros_like(dk_ref)
            dv_ref[...] = jnp.zeros_like(dv_ref)
            db_ref[...] = jnp.zeros_like(db_ref)

        qt = q_ref[...] * qk_scale
        dot_ = do_ref[...]
        kb = k_ref[pl.ds(start, KB), :]
        vb = v_ref[pl.ds(start, KB), :]
        head_of_lane = lax.broadcasted_iota(jnp.int32, (1, HEAD_PAIR), 1) // HEAD_DIM
        colvalid = (start + lax.broadcasted_iota(jnp.int32, (1, KB), 1)) >= PAD
        do_o = dot_.astype(F32) * o_ref[...]
        dq_heads = []
        dk_band = jnp.zeros((KB, HEAD_PAIR), F32)
        dv_band = jnp.zeros((KB, HEAD_PAIR), F32)
        for hh in range(2):
            mine = head_of_lane == hh
            q_m = jnp.where(mine, qt, jnp.zeros_like(qt))
            do_m = jnp.where(mine, dot_, jnp.zeros_like(dot_))
            delta_s[...] = jnp.sum(jnp.where(mine, do_o, 0.0), axis=-1, keepdims=True)
            lse_s[...] = lse_ref[:, hh * HEAD_DIM:hh * HEAD_DIM + 1]
            s_s[...] = _dot_nt(q_m, kb)
            dp_s[...] = _dot_nt(do_m, vb)
            for rows in _row_blocks():
                t = jnp.where(colvalid, s_s[rows, :] + b_ref[hh, rows, :] - lse_s[rows, :], NEG)
                p = jnp.exp(t)
                ds = p * (dp_s[rows, :] - delta_s[rows, :])
                db_ref[hh, rows, :] += ds
                p_s[rows, :] = p.astype(BF16)
                ds_s[rows, :] = ds.astype(BF16)
            dv_band = dv_band + _dot_tn(p_s[...], do_m)
            dq_heads.append(_dot(ds_s[...], kb) * qk_scale)
            dk_band = dk_band + _dot_tn(ds_s[...], q_m)
        dq_ref[...] = jnp.where(head_of_lane == 0, dq_heads[0], dq_heads[1]).astype(BF16)
        dk_ref[pl.ds(start, KB), :] += dk_band
        dv_ref[pl.ds(start, KB), :] += dv_band

    blk = pl.BlockSpec((QB, HEAD_PAIR), lambda hp, b: (b, hp))
    whole = pl.BlockSpec((PAD + s_len, HEAD_PAIR), lambda hp, b: (0, hp))
    btile = pl.BlockSpec((2, QB, KB), lambda hp, b: (hp, 0, 0))
    return pl.pallas_call(
        body, name="attn_bwd", grid=(nhp, nq),
        in_specs=[blk, whole, whole, btile, blk, blk, blk],
        out_specs=[blk, whole, whole, btile],
        out_shape=[jax.ShapeDtypeStruct((s_len, d), BF16),
                   jax.ShapeDtypeStruct((PAD + s_len, d), F32), jax.ShapeDtypeStruct((PAD + s_len, d), F32),
                   jax.ShapeDtypeStruct(bias.shape, F32)],
        scratch_shapes=[pltpu.VMEM((QB, KB), F32), pltpu.VMEM((QB, KB), F32),
                        pltpu.VMEM((QB, KB), BF16), pltpu.VMEM((QB, KB), BF16),
                        pltpu.VMEM((QB, 1), F32), pltpu.VMEM((QB, 1), F32)],
        compiler_params=_params(("arbitrary", "arbitrary")),
    )(q, k, v, bias, d_o, o, lse)


def _rel_bias_grad(db):
    nh = db.shape[0]
    assert QB == MAX_REL

    def body(db_ref, out_ref):
        m = jnp.concatenate([db_ref[...], jnp.zeros((QB, SKEW - KB), F32)], axis=1)
        amount = QB - 1 - lax.broadcasted_iota(jnp.int32, (QB, 1), 0)
        m = pltpu.roll(m, 1, 1)
        bit = 1
        while bit < QB:
            m = jnp.where((amount & bit) != 0, pltpu.roll(m, bit, 1), m)
            bit *= 2
        diag = jnp.sum(m, axis=0, keepdims=True)
        c = lax.broadcasted_iota(jnp.int32, (1, SKEW), 1)
        clipped = jnp.sum(jnp.where(c <= PAD, diag, 0.0), axis=1, keepdims=True)
        out_ref[...] = jnp.where(c == 0, clipped, diag)

    diag = pl.pallas_call(
        body, name="bias_diagonals", grid=(nh,),
        in_specs=[pl.BlockSpec((None, QB, KB), lambda h: (h, 0, 0))],
        out_specs=pl.BlockSpec((None, 1, SKEW), lambda h: (h, 0, 0)),
        out_shape=jax.ShapeDtypeStruct((nh, 1, SKEW), F32),
        compiler_params=_params(("arbitrary",)),
    )(db)[:, 0]
    return jnp.concatenate([jnp.zeros((nh, 1), F32), jnp.flip(diag[:, PAD + 1:], axis=1), diag[:, :1]], axis=1)


def _l1_out(o, z, x1, target, g_post, wa):
    s_len, d = o.shape
    nt = s_len // TS
    slab = d // N_CHIPS

    def body(o_ref, z_ref, x1_ref, t_ref, gpost_ref, wa_ref,
             g2_ref, gated_ref, dy_ref, do_ref, dz_ref, dgpost_ref, sq_ref, wout_s, sems):
        i = pl.program_id(0)

        @pl.when(i == 0)
        def _():
            _start_weight_copies([
                pltpu.make_async_copy(wa_ref.at[j, pl.ds(2 * d + d // 2, slab), :],
                                      wout_s.at[pl.ds(j * slab, slab), :], sems.at[j])
                for j in range(N_CHIPS)])
            dgpost_ref[...] = jnp.zeros_like(dgpost_ref)
            sq_ref[...] = jnp.zeros_like(sq_ref)

        ot = o_ref[...]
        sz, dsz = _silu_parts(z_ref[...])
        gated = (ot * sz).astype(BF16)
        gated_ref[...] = gated
        y = _dot(gated, wout_s[...])
        r_y = _rms(y)
        yh = y * r_y
        err = x1_ref[...] + yh * gpost_ref[...] - t_ref[...]
        sq_ref[...] += jnp.sum(err * err, axis=0, keepdims=True)
        g2 = err * (1.0 / d)
        g2_ref[...] = g2
        dgpost_ref[...] += jnp.sum(g2 * yh, axis=0, keepdims=True)
        dy = _rms_bwd(g2 * gpost_ref[...], yh, r_y).astype(BF16)
        dy_ref[...] = dy
        dgated = _dot_nt(dy, wout_s[...])
        do_ref[...] = (dgated * sz).astype(BF16)
        dz_ref[...] = (dgated * ot * dsz).astype(BF16)

    tile = pl.BlockSpec((TS, d), lambda i: (i, 0))
    row = pl.BlockSpec((1, d), lambda i: (0, 0))
    return pl.pallas_call(
        body, name="l1_out", grid=(nt,),
        in_specs=[tile, tile, tile, tile, row, ANY],
        out_specs=[tile, tile, tile, tile, tile, row, row],
        out_shape=[jax.ShapeDtypeStruct((s_len, d), F32)] + [jax.ShapeDtypeStruct((s_len, d), BF16)] * 4
                  + [jax.ShapeDtypeStruct((1, d), F32)] * 2,
        scratch_shapes=[pltpu.VMEM((d, d), BF16), pltpu.SemaphoreType.DMA((N_CHIPS,))],
        compiler_params=_params(("arbitrary",)),
    )(o, z, x1, target, g_post, wa)


def _l1_in_bwd(dq, dk, dv, dz, x1, g2, g_pre, wa):
    s_len, d = x1.shape
    nt = s_len // TS
    npad = PAD // TS

    def body(dq_ref, dk_ref, dv_ref, dz_ref, x_ref, g2_ref, gpre_ref, wa_ref,
             dx_ref, du_ref, dgpre_ref, win_s, sem):
        i = pl.program_id(0)

        @pl.when(i == 0)
        def _():
            _start_weight_copies([pltpu.make_async_copy(wa_ref.at[:, pl.ds(d, d), :], win_s, sem.at[0])])
            dgpre_ref[...] = jnp.zeros_like(dgpre_ref)

        parts = [dq_ref[...], dk_ref[...].astype(BF16), dv_ref[...].astype(BF16), dz_ref[...]]
        dh = jnp.zeros((TS, d), F32)
        for j, part in enumerate(parts):
            du_ref[:, j * d:(j + 1) * d] = part
            dh = dh + _dot_nt(part, win_s[j])
        xt = x_ref[...]
        r_x = _rms(xt)
        xh = xt * r_x
        dgpre_ref[...] += jnp.sum(dh * xh, axis=0, keepdims=True)
        dx_ref[...] = g2_ref[...] + _rms_bwd(dh * gpre_ref[...], xh, r_x)

    tile = pl.BlockSpec((TS, d), lambda i: (i, 0))
    padded = pl.BlockSpec((TS, d), lambda i: (i + npad, 0))
    row = pl.BlockSpec((1, d), lambda i: (0, 0))
    return pl.pallas_call(
        body, name="l1_in_bwd", grid=(nt,),
        in_specs=[tile, padded, padded, tile, tile, tile, row, ANY],
        out_specs=[tile, pl.BlockSpec((TS, 4 * d), lambda i: (i, 0)), row],
        out_shape=[jax.ShapeDtypeStruct((s_len, d), F32), jax.ShapeDtypeStruct((s_len, 4 * d), BF16),
                   jax.ShapeDtypeStruct((1, d), F32)],
        scratch_shapes=[pltpu.VMEM((N_CHIPS, d, d), BF16), pltpu.SemaphoreType.DMA((1,))],
        compiler_params=_params(("arbitrary",)),
    )(dq, dk, dv, dz, x1, g2, g_pre, wa)


def _wgrad(a, b, name, n_out, a_width, b_width, a_block, b_block, out_rows=None):
    s_len = a.shape[0]
    tk = min(1024, s_len)
    nk = s_len // tk

    def body(a_ref, b_ref, out_ref):
        @pl.when(pl.program_id(1) == 0)
        def _():
            out_ref[...] = jnp.zeros_like(out_ref)

        out_ref[...] += _dot_tn(a_ref[...], b_ref[...]).reshape(out_ref.shape)

    if out_rows is None:
        out_shape = (n_out, a_width, b_width)
        out_spec = pl.BlockSpec((None, a_width, b_width), lambda n, kk: (n, 0, 0))
    else:
        out_shape = (a_width // out_rows, n_out, out_rows, b_width)
        out_spec = pl.BlockSpec((a_width // out_rows, None, out_rows, b_width), lambda n, kk: (0, n, 0, 0))
    return pl.pallas_call(
        body, name=name, grid=(n_out, nk),
        in_specs=[pl.BlockSpec((tk, a_width), lambda n, kk: (kk, a_block(n))),
                  pl.BlockSpec((tk, b_width), lambda n, kk: (kk, b_block(n)))],
        out_specs=out_spec,
        out_shape=jax.ShapeDtypeStruct(out_shape, F32),
        compiler_params=_params(("arbitrary", "arbitrary")),
    )(a, b)


def _place():
    x, y, c = lax.axis_index("x"), lax.axis_index("y"), lax.axis_index("c")
    others = [(1 - x, y), (x, 1 - y), (1 - x, 1 - y)]
    return x, y, c, others


def _gather_weights(shards):
    n = len(shards)

    def body(*refs):
        src, dst = refs[:n], refs[n:2 * n]
        send, recv, fwd_send, fwd_recv, local = refs[2 * n:]
        x, y, c, others = _place()
        me = 2 * x + y
        sibling = (x, y, 1 - c)
        mine = [pltpu.make_async_copy(src[a], dst[a].at[me], local.at[a]) for a in range(n)]
        for cp in mine:
            cp.start()

        def half(a, ref, cc):
            rows = ref.shape[0] // 2
            return ref.at[pl.ds(cc * rows, rows), :]

        first, passed, arrive, arrive_fwd = [], [], [], []
        for a in range(n):
            for k, (ox, oy) in enumerate(others):
                sem = a * 3 + k
                first.append(pltpu.make_async_remote_copy(
                    src_ref=half(a, src[a], c), dst_ref=half(a, dst[a].at[me], c),
                    send_sem=send.at[sem], recv_sem=recv.at[sem], device_id=(ox, oy, c), device_id_type=MESH))
                theirs = half(a, dst[a].at[2 * ox + oy], c)
                arrive.append(pltpu.make_async_remote_copy(
                    src_ref=theirs, dst_ref=theirs, send_sem=send.at[sem], recv_sem=recv.at[sem],
                    device_id=(ox, oy, c), device_id_type=MESH))
                passed.append(pltpu.make_async_remote_copy(
                    src_ref=theirs, dst_ref=theirs, send_sem=fwd_send.at[sem], recv_sem=fwd_recv.at[sem],
                    device_id=sibling, device_id_type=MESH))
                other_half = half(a, dst[a].at[2 * ox + oy], 1 - c)
                arrive_fwd.append(pltpu.make_async_remote_copy(
                    src_ref=other_half, dst_ref=other_half, send_sem=fwd_send.at[sem], recv_sem=fwd_recv.at[sem],
                    device_id=sibling, device_id_type=MESH))
        for cp in first:
            cp.start()
        for got, onward in zip(arrive, passed):
            got.wait_recv()
            onward.start()
        for got in arrive_fwd:
            got.wait_recv()
        for cp in first + passed:
            cp.wait_send()
        for cp in mine:
            cp.wait()

    return pl.pallas_call(
        body, name="gather_weights",
        in_specs=[ANY] * n, out_specs=[ANY] * n,
        out_shape=[jax.ShapeDtypeStruct((N_CHIPS,) + s.shape, s.dtype) for s in shards],
        scratch_shapes=[pltpu.SemaphoreType.DMA((3 * n,)), pltpu.SemaphoreType.DMA((3 * n,)),
                        pltpu.SemaphoreType.DMA((3 * n,)), pltpu.SemaphoreType.DMA((3 * n,)),
                        pltpu.SemaphoreType.DMA((n,))],
    )(*shards)


def _swap_halves(arrays, name, send_half, in_place):
    n = len(arrays)

    def body(*refs):
        src, dst = refs[:n], refs[n:2 * n]
        if in_place:
            src = dst
        send, recv = refs[2 * n:]
        x, y, c, _ = _place()
        sibling = (x, y, 1 - c)
        out_copies, in_copies = [], []
        for a in range(n):
            rows = src[a].shape[-2] // 2
            lead = (slice(None),) * (len(src[a].shape) - 2)
            going = lead + (pl.ds(send_half(c) * rows, rows), slice(None))
            coming = lead + (pl.ds((1 - send_half(c)) * rows, rows), slice(None))
            out_copies.append(pltpu.make_async_remote_copy(
                src_ref=src[a].at[going], dst_ref=dst[a].at[going], send_sem=send.at[a], recv_sem=recv.at[a],
                device_id=sibling, device_id_type=MESH))
            in_copies.append(pltpu.make_async_remote_copy(
                src_ref=src[a].at[coming], dst_ref=dst[a].at[coming], send_sem=send.at[a], recv_sem=recv.at[a],
                device_id=sibling, device_id_type=MESH))
        for cp in out_copies:
            cp.start()
        for cp in in_copies:
            cp.wait_recv()
        for cp in out_copies:
            cp.wait_send()

    return pl.pallas_call(
        body, name=name,
        in_specs=[ANY] * n, out_specs=[ANY] * n,
        out_shape=[jax.ShapeDtypeStruct(a.shape, a.dtype) for a in arrays],
        scratch_shapes=[pltpu.SemaphoreType.DMA((n,)), pltpu.SemaphoreType.DMA((n,))],
        input_output_aliases={a: a for a in range(n)} if in_place else {},
    )(*arrays)


def _scatter_to_chips(arrays):
    n = len(arrays)

    def body(*refs):
        src, dst = refs[:n], refs[n:2 * n]
        send, recv = refs[2 * n:]
        x, y, c, others = _place()
        out_copies, in_copies = [], []
        for a in range(n):
            rows = src[a].shape[1] // 2
            mine = pl.ds(c * rows, rows)
            for k, (ox, oy) in enumerate(others):
                sem = a * 3 + k
                out_copies.append(pltpu.make_async_remote_copy(
                    src_ref=src[a].at[2 * ox + oy, mine, :], dst_ref=dst[a].at[k, mine, :],
                    send_sem=send.at[sem], recv_sem=recv.at[sem], device_id=(ox, oy, c), device_id_type=MESH))
                in_copies.append(pltpu.make_async_remote_copy(
                    src_ref=dst[a].at[k, mine, :], dst_ref=dst[a].at[k, mine, :],
                    send_sem=send.at[sem], recv_sem=recv.at[sem], device_id=(ox, oy, c), device_id_type=MESH))
        for cp in out_copies:
            cp.start()
        for cp in in_copies:
            cp.wait_recv()
        for cp in out_copies:
            cp.wait_send()

    return pl.pallas_call(
        body, name="scatter_to_chips",
        in_specs=[ANY] * n, out_specs=[ANY] * n,
        out_shape=[jax.ShapeDtypeStruct((3,) + a.shape[1:], a.dtype) for a in arrays],
        scratch_shapes=[pltpu.SemaphoreType.DMA((3 * n,)), pltpu.SemaphoreType.DMA((3 * n,))],
    )(*arrays)


def _allreduce_small(part):
    rows, width = part.shape

    def body(p_ref, out_ref, all_ref, send, recv):
        x, y, c, _ = _place()
        me = 4 * x + 2 * y + c
        all_ref[me] = p_ref[...]
        copies = []
        for k in range(1, N_DEV):
            px, py, pc = x ^ (k >> 2), y ^ ((k >> 1) & 1), c ^ (k & 1)
            copies.append(pltpu.make_async_remote_copy(
                src_ref=p_ref, dst_ref=all_ref.at[me], send_sem=send.at[k - 1], recv_sem=recv.at[k - 1],
                device_id=(px, py, pc), device_id_type=MESH))
        for cp in copies:
            cp.start()
        for cp in copies:
            cp.wait()
        total = all_ref[0]
        for k in range(1, N_DEV):
            total = total + all_ref[k]
        out_ref[...] = total

    return pl.pallas_call(
        body, name="allreduce_small",
        in_specs=[pl.BlockSpec(memory_space=pltpu.VMEM)],
        out_specs=pl.BlockSpec(memory_space=pltpu.VMEM),
        out_shape=jax.ShapeDtypeStruct((rows, width), F32),
        scratch_shapes=[pltpu.VMEM((N_DEV, rows, width), F32),
                        pltpu.SemaphoreType.DMA((N_DEV - 1,)), pltpu.SemaphoreType.DMA((N_DEV - 1,))],
    )(part)


def _row_tile(rows):
    t = min(rows, 256)
    while rows % t:
        t //= 2
    return t


def _core_and_chip():
    return jnp.stack([lax.axis_index("c"), 2 * lax.axis_index("x") + lax.axis_index("y")]).astype(jnp.int32)


def _sum_siblings(own, got, name):
    _, rows, cols = own.shape
    half = rows // 2
    t = _row_tile(half)
    nb = half // t

    def body(place_ref, own_ref, got_ref, mine_ref, out_ref):
        total = own_ref[...] + got_ref[...]
        out_ref[...] = total.astype(BF16)

        @pl.when(pl.program_id(1) == place_ref[1])
        def _():
            mine_ref[...] = total

    slab_blk = pl.BlockSpec((None, t, cols), lambda r, j, place: (j, place[0] * nb + r, 0))
    return pl.pallas_call(
        body, name=name,
        grid_spec=pltpu.PrefetchScalarGridSpec(
            num_scalar_prefetch=1, grid=(nb, N_CHIPS),
            in_specs=[slab_blk, slab_blk],
            out_specs=[pl.BlockSpec((t, cols), lambda r, j, place: (r, 0)), slab_blk]),
        out_shape=[jax.ShapeDtypeStruct((half, cols), F32), jax.ShapeDtypeStruct(own.shape, BF16)],
        compiler_params=_params(("arbitrary", "arbitrary")),
    )(_core_and_chip(), own, got)


def _sum_chips(mine, got, name):
    half, cols = mine.shape
    t = _row_tile(half)
    nb = half // t

    def body(place_ref, mine_ref, got_ref, out_ref):
        total = mine_ref[...]
        for k in range(3):
            total = total + got_ref[k].astype(F32)
        out_ref[...] = total

    return pl.pallas_call(
        body, name=name,
        grid_spec=pltpu.PrefetchScalarGridSpec(
            num_scalar_prefetch=1, grid=(nb,),
            in_specs=[pl.BlockSpec((t, cols), lambda r, place: (r, 0)),
                      pl.BlockSpec((3, t, cols), lambda r, place: (0, place[0] * nb + r, 0))],
            out_specs=pl.BlockSpec((t, cols), lambda r, place: (place[0] * nb + r, 0))),
        out_shape=jax.ShapeDtypeStruct((2 * half, cols), F32),
        compiler_params=_params(("arbitrary",)),
    )(_core_and_chip(), mine, got)


def _adamw(w, g, m, v, name, g_row0=0):
    rows, cols = w.shape
    t = _row_tile(rows)
    assert g_row0 % t == 0
    off = g_row0 // t

    def body(w_ref, g_ref, m_ref, v_ref, go_ref, d_ref, mo_ref, vo_ref):
        g_t = g_ref[...]
        m_new = ADAM_B1 * m_ref[...] + (1.0 - ADAM_B1) * g_t
        v_new = ADAM_B2 * v_ref[...] + (1.0 - ADAM_B2) * (g_t * g_t)
        m_hat = m_new / (1.0 - ADAM_B1 ** ADAM_STEP)
        v_hat = v_new / (1.0 - ADAM_B2 ** ADAM_STEP)
        go_ref[...] = g_t
        d_ref[...] = -ADAM_LR * (m_hat / (jnp.sqrt(v_hat) + ADAM_EPS) + ADAM_WD * w_ref[...])
        mo_ref[...] = m_new
        vo_ref[...] = v_new

    blk = pl.BlockSpec((t, cols), lambda r: (r, 0))
    return pl.pallas_call(
        body, name=name, grid=(rows // t,),
        in_specs=[blk, pl.BlockSpec((t, cols), lambda r: (r + off, 0)), blk, blk],
        out_specs=[blk] * 4,
        out_shape=[jax.ShapeDtypeStruct((rows, cols), F32)] * 4,
        compiler_params=_params(("arbitrary",)),
    )(w, g, m, v)


def _pack_small(d, vectors):
    rows = []
    for vec in vectors:
        flat = vec.reshape(-1)
        n_rows = -(-flat.shape[0] // d)
        rows.append(jnp.pad(flat, (0, n_rows * d - flat.shape[0])).reshape(n_rows, d))
    return jnp.concatenate(rows, axis=0)


def _unpack_small(packed, d, shapes):
    out, row = [], 0
    for shape in shapes:
        size = 1
        for s in shape:
            size *= s
        n_rows = -(-size // d)
        out.append(packed[row:row + n_rows].reshape(-1)[:size].reshape(shape))
        row += n_rows
    return out


def kernel(x, norm_pre, norm_post, pool_w_in, pool_w_group, pool_scale, pool_w_out, att_w_in, att_rel_bias, att_w_out, loss_target, m_norm_pre, m_norm_post, m_pool_w_in, m_pool_w_group, m_pool_scale, m_pool_w_out, m_att_w_in, m_att_rel_bias, m_att_w_out, v_norm_pre, v_norm_post, v_pool_w_in, v_pool_w_group, v_pool_scale, v_pool_w_out, v_att_w_in, v_att_rel_bias, v_att_w_out):
    _, s_len, d = x.shape
    gw = d // 2
    q = gw // N_CHIPS
    x2d = x.reshape(s_len, d)
    target = loss_target.reshape(s_len, d)

    def pack_a(p_in, a_in, p_out, a_out):
        return jnp.concatenate([p_in[0], a_in[0], p_out[0], a_out[0]], axis=0)

    def pack_g(p_group):
        return p_group.reshape(N_CHIPS * q, gw)

    wa_shard = pack_a(pool_w_in, att_w_in, pool_w_out, att_w_out)
    wg_shard = pack_g(pool_w_group)
    wa, wg = _gather_weights([wa_shard.astype(BF16), wg_shard.astype(BF16)])
    wg = wg.reshape(N_CHIPS, N_CHIPS, q, gw)

    x1, h0, mixed, z0, y0 = _l0_fwd(x2d, norm_pre[0:1], norm_post[0:1], pool_scale, wa, wg)
    h1, q_, k_, v_, z1 = _l1_inproj(x1, norm_pre[1:2], wa)
    bias = _bias_tiles(att_rel_bias[0])
    o, lse = _attn_fwd(q_, k_, v_, bias)

    g2, gated1, dy1, d_o, dz1, dgpost1, sq = _l1_out(o, z1, x1, target, norm_post[1:2], wa)
    dq, dk, dv, dbias = _attn_bwd(q_, k_, v_, bias, d_o, o, lse)
    dx1, du1, dgpre1 = _l1_in_bwd(dq, dk, dv, dz1, x1, g2, norm_pre[1:2], wa)
    grad_x, du0, gated0, dy0, dmm, dgpre0, dgpost0, dscale = _l0_bwd(
        dx1, x2d, y0, mixed, z0, norm_pre[0:1], norm_post[0:1], pool_scale, wa, wg)
    d_rel = _rel_bias_grad(dbias)

    blk = lambda n: n
    zero = lambda n: 0
    g_pin = _wgrad(h0, du0, "wgrad_pool_in", N_CHIPS, d, d, zero, blk)
    g_ain = _wgrad(h1, du1, "wgrad_att_in", N_CHIPS, d, d, zero, blk)
    g_pout = _wgrad(gated0, dy0, "wgrad_pool_out", N_CHIPS, gw, d, blk, zero)
    g_aout = _wgrad(gated1, dy1, "wgrad_att_out", 1, d, d, zero, zero).reshape(N_CHIPS, d // N_CHIPS, d)
    g_pg = _wgrad(mixed, dmm, "wgrad_pool_group", N_CHIPS, gw, gw, blk, blk, out_rows=q)
    ga = jnp.concatenate([g_pin, g_ain, g_pout, g_aout], axis=1)
    gg = g_pg.reshape(N_CHIPS, N_CHIPS * q, gw)

    got_a, got_g = _swap_halves([ga, gg], "swap_core_partials", lambda c: 1 - c, in_place=False)
    mine_a, send_a = _sum_siblings(ga, got_a, "sum_siblings_a")
    mine_g, send_g = _sum_siblings(gg, got_g, "sum_siblings_g")
    recv_a, recv_g = _scatter_to_chips([send_a, send_g])
    red_a = _sum_chips(mine_a, recv_a, "sum_chips_a")
    red_g = _sum_chips(mine_g, recv_g, "sum_chips_g")
    grad_a, grad_g = _swap_halves([red_a, red_g], "swap_reduced_halves", lambda c: c, in_place=True)

    small_shapes = [norm_pre.shape, norm_post.shape, pool_scale.shape, att_rel_bias.shape]
    part = _pack_small(d, [jnp.concatenate([dgpre0, dgpre1], axis=0), jnp.concatenate([dgpost0, dgpost1], axis=0),
                           dscale, d_rel, sq])
    total = _allreduce_small(part)
    loss = (0.5 / d) * jnp.sum(total[-1])
    w_small = _pack_small(d, [norm_pre, norm_post, pool_scale, att_rel_bias, jnp.zeros((d,), F32)])
    m_small = _pack_small(d, [m_norm_pre, m_norm_post, m_pool_scale, m_att_rel_bias, jnp.zeros((d,), F32)])
    v_small = _pack_small(d, [v_norm_pre, v_norm_post, v_pool_scale, v_att_rel_bias, jnp.ones((d,), F32)])
    small_out = [_unpack_small(a, d, small_shapes) for a in _adamw(w_small, total, m_small, v_small, "adamw_small")]

    big = {}
    row0 = 0
    for name, w, m, v in [("pool_w_in", pool_w_in, m_pool_w_in, v_pool_w_in),
                          ("att_w_in", att_w_in, m_att_w_in, v_att_w_in),
                          ("pool_w_out", pool_w_out, m_pool_w_out, v_pool_w_out),
                          ("att_w_out", att_w_out, m_att_w_out, v_att_w_out)]:
        outs = _adamw(w[0], grad_a, m[0], v[0], "adamw_" + name, g_row0=row0)
        big[name] = [a.reshape(w.shape) for a in outs]
        row0 += w.shape[1]
    outs = _adamw(pack_g(pool_w_group), grad_g, pack_g(m_pool_w_group), pack_g(v_pool_w_group), "adamw_pool_w_group")
    big["pool_w_group"] = [a.reshape(pool_w_group.shape) for a in outs]

    def leaf(kind):
        return (small_out[kind][0], small_out[kind][1], big["pool_w_in"][kind], big["pool_w_group"][kind],
                small_out[kind][2], big["pool_w_out"][kind], big["att_w_in"][kind], small_out[kind][3],
                big["att_w_out"][kind])

    return (loss, grad_x.reshape(x.shape), *leaf(0), *leaf(1), *leaf(2), *leaf(3))
```

```python
import functools

import jax
import jax.numpy as jnp
from jax import lax
from jax.experimental import pallas as pl
from jax.experimental.pallas import tpu as pltpu

F32 = jnp.float32
BF16 = jnp.bfloat16

RMS_EPS = 1e-6
CHUNK = 64
HEAD_DIM = 64
LEFT_CHUNKS = 8
PAD = LEFT_CHUNKS * CHUNK
MAX_REL = 256
POOL_WINDOWS = (2, 4, 8, 16)
HALO = 16
TS = 256
QB = 256
KB = QB + PAD
HEAD_PAIR = 2 * HEAD_DIM
NEG = -1e30
N_CHIPS = 4
N_DEV = 8

ADAM_LR = 0.001
ADAM_B1 = 0.9
ADAM_B2 = 0.999
ADAM_EPS = 1e-08
ADAM_WD = 0.01
ADAM_STEP = 10

VMEM_LIMIT = 56 * 1024 * 1024
MESH = pl.DeviceIdType.MESH
ANY = pl.BlockSpec(memory_space=pl.ANY)


def _dot(a, b):
    return jnp.dot(a, b, preferred_element_type=F32)


def _dot_nt(a, b):
    return lax.dot_general(a, b, (((1,), (1,)), ((), ())), preferred_element_type=F32)


def _dot_tn(a, b):
    return lax.dot_general(a, b, (((0,), (0,)), ((), ())), preferred_element_type=F32)


def _params(sem, limit=VMEM_LIMIT):
    return pltpu.CompilerParams(dimension_semantics=sem, vmem_limit_bytes=limit)


def _rms(x):
    return lax.rsqrt(jnp.mean(x * x, axis=-1, keepdims=True) + RMS_EPS)


def _rms_bwd(dyh, xh, r):
    return r * (dyh - xh * jnp.mean(dyh * xh, axis=-1, keepdims=True))


def _silu_parts(z):
    sig = jax.nn.sigmoid(z)
    return z * sig, sig * (1.0 + z * (1.0 - sig))


def _inv_count(tile, rows, w):
    t = tile * rows + lax.broadcasted_iota(jnp.int32, (rows, 1), 0)
    return 1.0 / jnp.minimum(t + 1, w).astype(F32)


def _start_weight_copies(copies):
    for c in copies:
        c.start()
    for c in copies:
        c.wait()


def _l0_fwd(x, g_pre, g_post, scale, wa, wg, next_shard):
    s_len, d = x.shape
    pw, gw = 2 * d, d // 2
    q = gw // N_CHIPS
    nt = s_len // TS
    assert nt >= 3

    def body(x_ref, gpre_ref, gpost_ref, sc_ref, wa_ref, wg_ref, shard_ref,
             x1_ref, h_ref, mixed_ref, z_ref, y_ref, next_ref,
             win_s, wout_s, wg_s, halo_s, sems, *gather_sems):
        i = pl.program_id(0)

        @pl.when(i == 0)
        def _():
            _GatherPlan([shard_ref], [next_ref], *gather_sems).send()
            copies = [pltpu.make_async_copy(wa_ref.at[:, pl.ds(0, d), :], win_s, sems.at[0]),
                      pltpu.make_async_copy(wa_ref.at[:, pl.ds(d, gw), :], wout_s, sems.at[1])]
            copies += [pltpu.make_async_copy(wg_ref.at[j], wg_s.at[:, pl.ds(j * q, q), :], sems.at[2 + j])
                       for j in range(N_CHIPS)]
            _start_weight_copies(copies)
            halo_s[...] = jnp.zeros_like(halo_s)

        @pl.when(i == nt // 2)
        def _():
            _GatherPlan([shard_ref], [next_ref], *gather_sems).pass_on()

        @pl.when(i == nt - 1)
        def _():
            _GatherPlan([shard_ref], [next_ref], *gather_sems).finish()

        xt = x_ref[...]
        h = ((xt * _rms(xt)) * gpre_ref[...]).astype(BF16)
        h_ref[...] = h
        a_blocks = [_dot(h, win_s[0]), _dot(h, win_s[1])]
        y = jnp.zeros((TS, d), F32)
        for g, w in enumerate(POOL_WINDOWS):
            a_g = a_blocks[g // 2][:, (g % 2) * gw:(g % 2 + 1) * gw]
            ext = jnp.concatenate([halo_s[g], a_g], axis=0)
            shift = 1
            while shift < w:
                ext = ext + pltpu.roll(ext, shift, 0)
                shift *= 2
            mixed = (ext[HALO:] * _inv_count(i, TS, w) - a_g).astype(BF16)
            halo_s[g] = a_g[TS - HALO:]
            mixed_ref[:, g * gw:(g + 1) * gw] = mixed
            z_g = _dot(h, win_s[2 + g // 2, :, (g % 2) * gw:(g % 2 + 1) * gw])
            z_ref[:, g * gw:(g + 1) * gw] = z_g
            ms = _dot(mixed, wg_s[g]) * sc_ref[:, g * gw:(g + 1) * gw]
            gated = (ms * _silu_parts(z_g)[0]).astype(BF16)
            y = y + _dot(gated, wout_s[g])
        y_ref[...] = y
        x1_ref[...] = xt + (y * _rms(y)) * gpost_ref[...]

    tile = lambda wdt: pl.BlockSpec((TS, wdt), lambda i: (i, 0))
    row = lambda wdt: pl.BlockSpec((1, wdt), lambda i: (0, 0))
    return pl.pallas_call(
        body, name="l0_fwd", grid=(nt,),
        in_specs=[tile(d), row(d), row(d), row(pw), ANY, ANY, ANY],
        out_specs=[tile(d), tile(d), tile(pw), tile(pw), tile(d), ANY],
        out_shape=[jax.ShapeDtypeStruct((s_len, d), F32), jax.ShapeDtypeStruct((s_len, d), BF16),
                   jax.ShapeDtypeStruct((s_len, pw), BF16), jax.ShapeDtypeStruct((s_len, pw), F32),
                   jax.ShapeDtypeStruct((s_len, d), F32),
                   jax.ShapeDtypeStruct((N_CHIPS,) + next_shard.shape, next_shard.dtype)],
        scratch_shapes=[pltpu.VMEM((N_CHIPS, d, d), BF16), pltpu.VMEM((N_CHIPS, gw, d), BF16),
                        pltpu.VMEM((N_CHIPS, gw, gw), BF16), pltpu.VMEM((N_CHIPS, HALO, gw), F32),
                        pltpu.SemaphoreType.DMA((2 + N_CHIPS,))] + _gather_sems(1),
        compiler_params=_params(("arbitrary",)),
    )(x, g_pre, g_post, scale, wa, wg, next_shard)


def _l0_bwd(dx1, x, y, mixed, z, g_pre, g_post, scale, wa, wg, partials):
    s_len, d = x.shape
    pw, gw = 2 * d, d // 2
    q = gw // N_CHIPS
    nt = s_len // TS

    def body(dx1_ref, x_ref, y_ref, mixed_ref, z_ref, gpre_ref, gpost_ref, sc_ref, wa_ref, wg_ref, part_ref,
             gx_ref, du_ref, gated_ref, dy_ref, dmm_ref, dgpre_ref, dgpost_ref, dsc_ref, got_ref,
             win_s, wout_s, wg_s, halo_s, sems, send_sems, recv_sems):
        i = pl.program_id(0)
        tile = nt - 1 - i

        @pl.when(i == nt - 1)
        def _():
            _ScatterPlan([part_ref], [got_ref], send_sems, recv_sems).finish()

        @pl.when(i == 0)
        def _():
            _ScatterPlan([part_ref], [got_ref], send_sems, recv_sems).send()
            copies = [pltpu.make_async_copy(wa_ref.at[:, pl.ds(0, d), :], win_s, sems.at[0]),
                      pltpu.make_async_copy(wa_ref.at[:, pl.ds(d, gw), :], wout_s, sems.at[1])]
            copies += [pltpu.make_async_copy(wg_ref.at[j], wg_s.at[:, pl.ds(j * q, q), :], sems.at[2 + j])
                       for j in range(N_CHIPS)]
            _start_weight_copies(copies)
            halo_s[...] = jnp.zeros_like(halo_s)
            dgpre_ref[...] = jnp.zeros_like(dgpre_ref)
            dgpost_ref[...] = jnp.zeros_like(dgpost_ref)
            dsc_ref[...] = jnp.zeros_like(dsc_ref)

        g_in = dx1_ref[...]
        yt = y_ref[...]
        r_y = _rms(yt)
        yh = yt * r_y
        dgpost_ref[...] += jnp.sum(g_in * yh, axis=0, keepdims=True)
        dy = _rms_bwd(g_in * gpost_ref[...], yh, r_y).astype(BF16)
        dy_ref[...] = dy
        dh = jnp.zeros((TS, d), F32)
        for g, w in enumerate(POOL_WINDOWS):
            cols = slice(g * gw, (g + 1) * gw)
            dgated = _dot_nt(dy, wout_s[g])
            mm = _dot(mixed_ref[:, cols], wg_s[g])
            sc = sc_ref[:, cols]
            ms = mm * sc
            z_g = z_ref[:, cols]
            sz, dsz = _silu_parts(z_g)
            gated_ref[:, cols] = (ms * sz).astype(BF16)
            dms = dgated * sz
            dz = (dgated * ms * dsz).astype(BF16)
            dsc_ref[:, cols] += jnp.sum(dms * mm, axis=0, keepdims=True)
            dmm = (dms * sc).astype(BF16)
            dmm_ref[:, cols] = dmm
            dmixed = _dot_nt(dmm, wg_s[g])
            e = dmixed * _inv_count(tile, TS, w)
            ext = jnp.concatenate([e, halo_s[g]], axis=0)
            shift = 1
            while shift < w:
                ext = ext + pltpu.roll(ext, TS + HALO - shift, 0)
                shift *= 2
            da = (ext[:TS] - dmixed).astype(BF16)
            halo_s[g] = e[:HALO]
            du_ref[:, cols] = da
            du_ref[:, pw + g * gw:pw + (g + 1) * gw] = dz
            wa_blk = win_s[g // 2, :, (g % 2) * gw:(g % 2 + 1) * gw]
            wz_blk = win_s[2 + g // 2, :, (g % 2) * gw:(g % 2 + 1) * gw]
            dh = dh + _dot_nt(da, wa_blk) + _dot_nt(dz, wz_blk)
        xt = x_ref[...]
        r_x = _rms(xt)
        xh = xt * r_x
        dgpre_ref[...] += jnp.sum(dh * xh, axis=0, keepdims=True)
        gx_ref[...] = g_in + _rms_bwd(dh * gpre_ref[...], xh, r_x)

    tile_spec = lambda wdt: pl.BlockSpec((TS, wdt), lambda i: (nt - 1 - i, 0))
    row = lambda wdt: pl.BlockSpec((1, wdt), lambda i: (0, 0))
    return pl.pallas_call(
        body, name="l0_bwd", grid=(nt,),
        in_specs=[tile_spec(d), tile_spec(d), tile_spec(d), tile_spec(pw), tile_spec(pw),
                  row(d), row(d), row(pw), ANY, ANY, ANY],
        out_specs=[tile_spec(d), tile_spec(2 * pw), tile_spec(pw), tile_spec(d), tile_spec(pw),
                   row(d), row(d), row(pw), ANY],
        out_shape=[jax.ShapeDtypeStruct((s_len, d), F32), jax.ShapeDtypeStruct((s_len, 2 * pw), BF16),
                   jax.ShapeDtypeStruct((s_len, pw), BF16), jax.ShapeDtypeStruct((s_len, d), BF16),
                   jax.ShapeDtypeStruct((s_len, pw), BF16),
                   jax.ShapeDtypeStruct((1, d), F32), jax.ShapeDtypeStruct((1, d), F32),
                   jax.ShapeDtypeStruct((1, pw), F32),
                   jax.ShapeDtypeStruct((3,) + partials.shape[1:], partials.dtype)],
        scratch_shapes=[pltpu.VMEM((N_CHIPS, d, d), BF16), pltpu.VMEM((N_CHIPS, gw, d), BF16),
                        pltpu.VMEM((N_CHIPS, gw, gw), BF16), pltpu.VMEM((N_CHIPS, HALO, gw), F32),
                        pltpu.SemaphoreType.DMA((2 + N_CHIPS,)),
                        pltpu.SemaphoreType.DMA((3,)), pltpu.SemaphoreType.DMA((3,))],
        compiler_params=_params(("arbitrary",)),
    )(dx1, x, y, mixed, z, g_pre, g_post, scale, wa, wg, partials)


def _l1_inproj(x1, g_pre, wa):
    s_len, d = x1.shape
    nt = s_len // TS
    npad = PAD // TS

    def body(x_ref, gpre_ref, wa_ref, h_ref, q_ref, k_ref, v_ref, z_ref, win_s, sem):
        i = pl.program_id(0)

        @pl.when(i == 0)
        def _():
            _start_weight_copies([pltpu.make_async_copy(wa_ref.at[:, pl.ds(0, d), :], win_s, sem.at[0])])

        @pl.when(i < npad)
        def _():
            k_ref[...] = jnp.zeros_like(k_ref)
            v_ref[...] = jnp.zeros_like(v_ref)

        @pl.when(i >= npad)
        def _():
            xt = x_ref[...]
            h = ((xt * _rms(xt)) * gpre_ref[...]).astype(BF16)
            h_ref[...] = h
            q_ref[...] = _dot(h, win_s[0]).astype(BF16)
            k_ref[...] = _dot(h, win_s[1]).astype(BF16)
            v_ref[...] = _dot(h, win_s[2]).astype(BF16)
            z_ref[...] = _dot(h, win_s[3])

    tile = pl.BlockSpec((TS, d), lambda i: (jnp.maximum(i - npad, 0), 0))
    padded = pl.BlockSpec((TS, d), lambda i: (i, 0))
    return pl.pallas_call(
        body, name="l1_inproj", grid=(nt + npad,),
        in_specs=[tile, pl.BlockSpec((1, d), lambda i: (0, 0)), ANY],
        out_specs=[tile, tile, padded, padded, tile],
        out_shape=[jax.ShapeDtypeStruct((s_len, d), BF16), jax.ShapeDtypeStruct((s_len, d), BF16),
                   jax.ShapeDtypeStruct((PAD + s_len, d), BF16), jax.ShapeDtypeStruct((PAD + s_len, d), BF16),
                   jax.ShapeDtypeStruct((s_len, d), F32)],
        scratch_shapes=[pltpu.VMEM((N_CHIPS, d, d), BF16), pltpu.SemaphoreType.DMA((1,))],
        compiler_params=_params(("arbitrary",)),
    )(x1, g_pre, wa)


SKEW = QB + KB


def _bias_tiles(rel_bias):
    nh = rel_bias.shape[0]
    assert QB == MAX_REL
    by_column = jnp.concatenate([jnp.broadcast_to(rel_bias[:, 2 * MAX_REL:], (nh, PAD + 1)),
                                 jnp.flip(rel_bias[:, 1:2 * MAX_REL], axis=1)], axis=1).astype(F32)

    def body(col_ref, out_ref):
        rows = jnp.broadcast_to(col_ref[pl.ds(pl.program_id(0), 1), :], (QB, SKEW))
        tile = pltpu.roll(rows, SKEW - QB, 1, stride=1, stride_axis=0)[:, :KB]
        i = lax.broadcasted_iota(jnp.int32, (QB, KB), 0)
        j = lax.broadcasted_iota(jnp.int32, (QB, KB), 1)
        first = (i // CHUNK) * CHUNK
        out_ref[...] = jnp.where((j >= first) & (j < first + PAD + CHUNK), tile, NEG)

    return pl.pallas_call(
        body, name="bias_tiles", grid=(nh,),
        in_specs=[pl.BlockSpec((nh, SKEW), lambda h: (0, 0))],
        out_specs=pl.BlockSpec((None, QB, KB), lambda h: (h, 0, 0)),
        out_shape=jax.ShapeDtypeStruct((nh, QB, KB), F32),
        compiler_params=_params(("arbitrary",)),
    )(by_column)


ROWS = 16


def _row_blocks():
    return [pl.ds(r * ROWS, ROWS) for r in range(QB // ROWS)]


def _attn_fwd(q, k, v, bias):
    s_len, d = q.shape
    nhp = d // HEAD_PAIR
    nq = s_len // QB

    def body(q_ref, k_ref, v_ref, b_ref, o_ref, lse_ref, s_s, p_s, l_s):
        start = pl.multiple_of(pl.program_id(1) * QB, QB)
        qt = q_ref[...] * (HEAD_DIM ** -0.5)
        kb = k_ref[pl.ds(start, KB), :]
        vb = v_ref[pl.ds(start, KB), :]
        head_of_lane = lax.broadcasted_iota(jnp.int32, (1, HEAD_PAIR), 1) // HEAD_DIM
        colvalid = (start + lax.broadcasted_iota(jnp.int32, (1, KB), 1)) >= PAD
        outs, lses = [], []
        for hh in range(2):
            s_s[...] = _dot_nt(jnp.where(head_of_lane == hh, qt, jnp.zeros_like(qt)), kb)
            for rows in _row_blocks():
                s = jnp.where(colvalid, s_s[rows, :] + b_ref[hh, rows, :], NEG)
                m = jnp.max(s, axis=-1, keepdims=True)
                e = jnp.exp(s - m)
                l = jnp.sum(e, axis=-1, keepdims=True)
                p_s[rows, :] = (e * (1.0 / l)).astype(BF16)
                l_s[rows, :] = m + jnp.log(l)
            outs.append(_dot(p_s[...], vb))
            lses.append(l_s[...])
        o_ref[...] = jnp.where(head_of_lane == 0, outs[0], outs[1])
        lse_ref[...] = jnp.where(head_of_lane == 0, lses[0], lses[1])

    blk = pl.BlockSpec((QB, HEAD_PAIR), lambda hp, b: (b, hp))
    whole = pl.BlockSpec((PAD + s_len, HEAD_PAIR), lambda hp, b: (0, hp))
    return pl.pallas_call(
        body, name="attn_fwd", grid=(nhp, nq),
        in_specs=[blk, whole, whole, pl.BlockSpec((2, QB, KB), lambda hp, b: (hp, 0, 0))],
        out_specs=[blk, blk],
        out_shape=[jax.ShapeDtypeStruct((s_len, d), F32), jax.ShapeDtypeStruct((s_len, d), F32)],
        scratch_shapes=[pltpu.VMEM((QB, KB), F32), pltpu.VMEM((QB, KB), BF16), pltpu.VMEM((QB, 1), F32)],
        compiler_params=_params(("arbitrary", "arbitrary")),
    )(q, k, v, bias)


def _attn_bwd(q, k, v, bias, d_o, o, lse):
    s_len, d = q.shape
    nhp = d // HEAD_PAIR
    nq = s_len // QB
    qk_scale = HEAD_DIM ** -0.5

    def body(q_ref, k_ref, v_ref, b_ref, do_ref, o_ref, lse_ref, dq_ref, dk_ref, dv_ref, db_ref,
             s_s, dp_s, p_s, ds_s, lse_s, delta_s):
        b = pl.program_id(1)
        start = pl.multiple_of(b * QB, QB)

        @pl.when(b == 0)
        def _():
            dk_ref[...] = jnp.zeros_like(dk_ref)
            dv_ref[...] = jnp.zeros_like(dv_ref)
            db_ref[...] = jnp.zeros_like(db_ref)

        qt = q_ref[...] * qk_scale
        dot_ = do_ref[...]
        kb = k_ref[pl.ds(start, KB), :]
        vb = v_ref[pl.ds(start, KB), :]
        head_of_lane = lax.broadcasted_iota(jnp.int32, (1, HEAD_PAIR), 1) // HEAD_DIM
        colvalid = (start + lax.broadcasted_iota(jnp.int32, (1, KB), 1)) >= PAD
        do_o = dot_.astype(F32) * o_ref[...]
        dq_heads = []
        dk_band = jnp.zeros((KB, HEAD_PAIR), F32)
        dv_band = jnp.zeros((KB, HEAD_PAIR), F32)
        for hh in range(2):
            mine = head_of_lane == hh
            q_m = jnp.where(mine, qt, jnp.zeros_like(qt))
            do_m = jnp.where(mine, dot_, jnp.zeros_like(dot_))
            delta_s[...] = jnp.sum(jnp.where(mine, do_o, 0.0), axis=-1, keepdims=True)
            lse_s[...] = lse_ref[:, hh * HEAD_DIM:hh * HEAD_DIM + 1]
            s_s[...] = _dot_nt(q_m, kb)
            dp_s[...] = _dot_nt(do_m, vb)
            for rows in _row_blocks():
                t = jnp.where(colvalid, s_s[rows, :] + b_ref[hh, rows, :] - lse_s[rows, :], NEG)
                p = jnp.exp(t)
                ds = p * (dp_s[rows, :] - delta_s[rows, :])
                db_ref[hh, rows, :] += ds
                p_s[rows, :] = p.astype(BF16)
                ds_s[rows, :] = ds.astype(BF16)
            dv_band = dv_band + _dot_tn(p_s[...], do_m)
            dq_heads.append(_dot(ds_s[...], kb) * qk_scale)
            dk_band = dk_band + _dot_tn(ds_s[...], q_m)
        dq_ref[...] = jnp.where(head_of_lane == 0, dq_heads[0], dq_heads[1]).astype(BF16)
        dk_ref[pl.ds(start, KB), :] += dk_band
        dv_ref[pl.ds(start, KB), :] += dv_band

    blk = pl.BlockSpec((QB, HEAD_PAIR), lambda hp, b: (b, hp))
    whole = pl.BlockSpec((PAD + s_len, HEAD_PAIR), lambda hp, b: (0, hp))
    btile = pl.BlockSpec((2, QB, KB), lambda hp, b: (hp, 0, 0))
    return pl.pallas_call(
        body, name="attn_bwd", grid=(nhp, nq),
        in_specs=[blk, whole, whole, btile, blk, blk, blk],
        out_specs=[blk, whole, whole, btile],
        out_shape=[jax.ShapeDtypeStruct((s_len, d), BF16),
                   jax.ShapeDtypeStruct((PAD + s_len, d), F32), jax.ShapeDtypeStruct((PAD + s_len, d), F32),
                   jax.ShapeDtypeStruct(bias.shape, F32)],
        scratch_shapes=[pltpu.VMEM((QB, KB), F32), pltpu.VMEM((QB, KB), F32),
                        pltpu.VMEM((QB, KB), BF16), pltpu.VMEM((QB, KB), BF16),
                        pltpu.VMEM((QB, 1), F32), pltpu.VMEM((QB, 1), F32)],
        compiler_params=_params(("arbitrary", "arbitrary")),
    )(q, k, v, bias, d_o, o, lse)


def _rel_bias_grad(db):
    nh = db.shape[0]
    assert QB == MAX_REL

    def body(db_ref, out_ref):
        m = jnp.concatenate([db_ref[...], jnp.zeros((QB, SKEW - KB), F32)], axis=1)
        amount = QB - 1 - lax.broadcasted_iota(jnp.int32, (QB, 1), 0)
        m = pltpu.roll(m, 1, 1)
        bit = 1
        while bit < QB:
            m = jnp.where((amount & bit) != 0, pltpu.roll(m, bit, 1), m)
            bit *= 2
        diag = jnp.sum(m, axis=0, keepdims=True)
        c = lax.broadcasted_iota(jnp.int32, (1, SKEW), 1)
        clipped = jnp.sum(jnp.where(c <= PAD, diag, 0.0), axis=1, keepdims=True)
        out_ref[...] = jnp.where(c == 0, clipped, diag)

    diag = pl.pallas_call(
        body, name="bias_diagonals", grid=(nh,),
        in_specs=[pl.BlockSpec((None, QB, KB), lambda h: (h, 0, 0))],
        out_specs=pl.BlockSpec((None, 1, SKEW), lambda h: (h, 0, 0)),
        out_shape=jax.ShapeDtypeStruct((nh, 1, SKEW), F32),
        compiler_params=_params(("arbitrary",)),
    )(db)[:, 0]
    return jnp.concatenate([jnp.zeros((nh, 1), F32), jnp.flip(diag[:, PAD + 1:], axis=1), diag[:, :1]], axis=1)


def _l1_out(o, z, x1, target, g_post, wa):
    s_len, d = o.shape
    nt = s_len // TS
    slab = d // N_CHIPS

    def body(o_ref, z_ref, x1_ref, t_ref, gpost_ref, wa_ref,
             g2_ref, gated_ref, dy_ref, do_ref, dz_ref, dgpost_ref, sq_ref, wout_s, sems):
        i = pl.program_id(0)

        @pl.when(i == 0)
        def _():
            _start_weight_copies([
                pltpu.make_async_copy(wa_ref.at[j, pl.ds(d, slab), :],
                                      wout_s.at[pl.ds(j * slab, slab), :], sems.at[j])
                for j in range(N_CHIPS)])
            dgpost_ref[...] = jnp.zeros_like(dgpost_ref)
            sq_ref[...] = jnp.zeros_like(sq_ref)

        ot = o_ref[...]
        sz, dsz = _silu_parts(z_ref[...])
        gated = (ot * sz).astype(BF16)
        gated_ref[...] = gated
        y = _dot(gated, wout_s[...])
        r_y = _rms(y)
        yh = y * r_y
        err = x1_ref[...] + yh * gpost_ref[...] - t_ref[...]
        sq_ref[...] += jnp.sum(err * err, axis=0, keepdims=True)
        g2 = err * (1.0 / d)
        g2_ref[...] = g2
        dgpost_ref[...] += jnp.sum(g2 * yh, axis=0, keepdims=True)
        dy = _rms_bwd(g2 * gpost_ref[...], yh, r_y).astype(BF16)
        dy_ref[...] = dy
        dgated = _dot_nt(dy, wout_s[...])
        do_ref[...] = (dgated * sz).astype(BF16)
        dz_ref[...] = (dgated * ot * dsz).astype(BF16)

    tile = pl.BlockSpec((TS, d), lambda i: (i, 0))
    row = pl.BlockSpec((1, d), lambda i: (0, 0))
    return pl.pallas_call(
        body, name="l1_out", grid=(nt,),
        in_specs=[tile, tile, tile, tile, row, ANY],
        out_specs=[tile, tile, tile, tile, tile, row, row],
        out_shape=[jax.ShapeDtypeStruct((s_len, d), F32)] + [jax.ShapeDtypeStruct((s_len, d), BF16)] * 4
                  + [jax.ShapeDtypeStruct((1, d), F32)] * 2,
        scratch_shapes=[pltpu.VMEM((d, d), BF16), pltpu.SemaphoreType.DMA((N_CHIPS,))],
        compiler_params=_params(("arbitrary",)),
    )(o, z, x1, target, g_post, wa)


def _l1_in_bwd(dq, dk, dv, dz, x1, g2, g_pre, wa):
    s_len, d = x1.shape
    nt = s_len // TS
    npad = PAD // TS

    def body(dq_ref, dk_ref, dv_ref, dz_ref, x_ref, g2_ref, gpre_ref, wa_ref,
             dx_ref, du_ref, dgpre_ref, win_s, sem):
        i = pl.program_id(0)

        @pl.when(i == 0)
        def _():
            _start_weight_copies([pltpu.make_async_copy(wa_ref.at[:, pl.ds(0, d), :], win_s, sem.at[0])])
            dgpre_ref[...] = jnp.zeros_like(dgpre_ref)

        parts = [dq_ref[...], dk_ref[...].astype(BF16), dv_ref[...].astype(BF16), dz_ref[...]]
        dh = jnp.zeros((TS, d), F32)
        for j, part in enumerate(parts):
            du_ref[:, j * d:(j + 1) * d] = part
            dh = dh + _dot_nt(part, win_s[j])
        xt = x_ref[...]
        r_x = _rms(xt)
        xh = xt * r_x
        dgpre_ref[...] += jnp.sum(dh * xh, axis=0, keepdims=True)
        dx_ref[...] = g2_ref[...] + _rms_bwd(dh * gpre_ref[...], xh, r_x)

    tile = pl.BlockSpec((TS, d), lambda i: (i, 0))
    padded = pl.BlockSpec((TS, d), lambda i: (i + npad, 0))
    row = pl.BlockSpec((1, d), lambda i: (0, 0))
    return pl.pallas_call(
        body, name="l1_in_bwd", grid=(nt,),
        in_specs=[tile, padded, padded, tile, tile, tile, row, ANY],
        out_specs=[tile, pl.BlockSpec((TS, 4 * d), lambda i: (i, 0)), row],
        out_shape=[jax.ShapeDtypeStruct((s_len, d), F32), jax.ShapeDtypeStruct((s_len, 4 * d), BF16),
                   jax.ShapeDtypeStruct((1, d), F32)],
        scratch_shapes=[pltpu.VMEM((N_CHIPS, d, d), BF16), pltpu.SemaphoreType.DMA((1,))],
        compiler_params=_params(("arbitrary",)),
    )(dq, dk, dv, dz, x1, g2, g_pre, wa)


def _wgrad(a, b, name, n_out, a_width, b_width, a_block, b_block, out_rows=None):
    s_len = a.shape[0]
    tk = min(1024, s_len)
    nk = s_len // tk

    def body(a_ref, b_ref, out_ref):
        @pl.when(pl.program_id(1) == 0)
        def _():
            out_ref[...] = jnp.zeros_like(out_ref)

        out_ref[...] += _dot_tn(a_ref[...], b_ref[...]).reshape(out_ref.shape)

    if out_rows is None:
        out_shape = (n_out, a_width, b_width)
        out_spec = pl.BlockSpec((None, a_width, b_width), lambda n, kk: (n, 0, 0))
    else:
        out_shape = (a_width // out_rows, n_out, out_rows, b_width)
        out_spec = pl.BlockSpec((a_width // out_rows, None, out_rows, b_width), lambda n, kk: (0, n, 0, 0))
    return pl.pallas_call(
        body, name=name, grid=(n_out, nk),
        in_specs=[pl.BlockSpec((tk, a_width), lambda n, kk: (kk, a_block(n))),
                  pl.BlockSpec((tk, b_width), lambda n, kk: (kk, b_block(n)))],
        out_specs=out_spec,
        out_shape=jax.ShapeDtypeStruct(out_shape, F32),
        compiler_params=_params(("arbitrary", "arbitrary")),
    )(a, b)


def _place():
    x, y, c = lax.axis_index("x"), lax.axis_index("y"), lax.axis_index("c")
    others = [(1 - x, y), (x, 1 - y), (1 - x, 1 - y)]
    return x, y, c, others


class _GatherPlan:
    def __init__(self, src, dst, send, recv, fwd_send, fwd_recv, local):
        x, y, c, others = _place()
        me = 2 * x + y
        sibling = (x, y, 1 - c)

        def half(ref, cc):
            rows = ref.shape[0] // 2
            return ref.at[pl.ds(cc * rows, rows), :]

        self.mine = [pltpu.make_async_copy(src[a], dst[a].at[me], local.at[a]) for a in range(len(src))]
        self.first, self.passed, self.arrive, self.arrive_fwd = [], [], [], []
        for a in range(len(src)):
            for k, (ox, oy) in enumerate(others):
                sem = a * 3 + k
                self.first.append(pltpu.make_async_remote_copy(
                    src_ref=half(src[a], c), dst_ref=half(dst[a].at[me], c),
                    send_sem=send.at[sem], recv_sem=recv.at[sem], device_id=(ox, oy, c), device_id_type=MESH))
                theirs = half(dst[a].at[2 * ox + oy], c)
                self.arrive.append(pltpu.make_async_remote_copy(
                    src_ref=theirs, dst_ref=theirs, send_sem=send.at[sem], recv_sem=recv.at[sem],
                    device_id=(ox, oy, c), device_id_type=MESH))
                self.passed.append(pltpu.make_async_remote_copy(
                    src_ref=theirs, dst_ref=theirs, send_sem=fwd_send.at[sem], recv_sem=fwd_recv.at[sem],
                    device_id=sibling, device_id_type=MESH))
                other_half = half(dst[a].at[2 * ox + oy], 1 - c)
                self.arrive_fwd.append(pltpu.make_async_remote_copy(
                    src_ref=other_half, dst_ref=other_half, send_sem=fwd_send.at[sem], recv_sem=fwd_recv.at[sem],
                    device_id=sibling, device_id_type=MESH))

    def send(self):
        for cp in self.mine + self.first:
            cp.start()

    def pass_on(self):
        for got, onward in zip(self.arrive, self.passed):
            got.wait_recv()
            onward.start()

    def finish(self):
        for got in self.arrive_fwd:
            got.wait_recv()
        for cp in self.first + self.passed:
            cp.wait_send()
        for cp in self.mine:
            cp.wait()


def _gather_sems(n):
    return [pltpu.SemaphoreType.DMA((3 * n,))] * 4 + [pltpu.SemaphoreType.DMA((n,))]


def _gather_weights(shards):
    n = len(shards)

    def body(*refs):
        plan = _GatherPlan(refs[:n], refs[n:2 * n], *refs[2 * n:])
        plan.send()
        plan.pass_on()
        plan.finish()

    return pl.pallas_call(
        body, name="gather_weights",
        in_specs=[ANY] * n, out_specs=[ANY] * n,
        out_shape=[jax.ShapeDtypeStruct((N_CHIPS,) + s.shape, s.dtype) for s in shards],
        scratch_shapes=_gather_sems(n),
    )(*shards)


def _swap_halves(arrays, name, send_half, in_place):
    n = len(arrays)

    def body(*refs):
        src, dst = refs[:n], refs[n:2 * n]
        if in_place:
            src = dst
        send, recv = refs[2 * n:]
        x, y, c, _ = _place()
        sibling = (x, y, 1 - c)
        out_copies, in_copies = [], []
        for a in range(n):
            rows = src[a].shape[-2] // 2
            lead = (slice(None),) * (len(src[a].shape) - 2)
            going = lead + (pl.ds(send_half(c) * rows, rows), slice(None))
            coming = lead + (pl.ds((1 - send_half(c)) * rows, rows), slice(None))
            out_copies.append(pltpu.make_async_remote_copy(
                src_ref=src[a].at[going], dst_ref=dst[a].at[going], send_sem=send.at[a], recv_sem=recv.at[a],
                device_id=sibling, device_id_type=MESH))
            in_copies.append(pltpu.make_async_remote_copy(
                src_ref=src[a].at[coming], dst_ref=dst[a].at[coming], send_sem=send.at[a], recv_sem=recv.at[a],
                device_id=sibling, device_id_type=MESH))
        for cp in out_copies:
            cp.start()
        for cp in in_copies:
            cp.wait_recv()
        for cp in out_copies:
            cp.wait_send()

    return pl.pallas_call(
        body, name=name,
        in_specs=[ANY] * n, out_specs=[ANY] * n,
        out_shape=[jax.ShapeDtypeStruct(a.shape, a.dtype) for a in arrays],
        scratch_shapes=[pltpu.SemaphoreType.DMA((n,)), pltpu.SemaphoreType.DMA((n,))],
        input_output_aliases={a: a for a in range(n)} if in_place else {},
    )(*arrays)


class _ScatterPlan:
    def __init__(self, src, dst, send, recv):
        x, y, c, others = _place()
        self.out_copies, self.in_copies = [], []
        for a in range(len(src)):
            rows = src[a].shape[1] // 2
            mine = pl.ds(c * rows, rows)
            for k, (ox, oy) in enumerate(others):
                sem = a * 3 + k
                self.out_copies.append(pltpu.make_async_remote_copy(
                    src_ref=src[a].at[2 * ox + oy, mine, :], dst_ref=dst[a].at[k, mine, :],
                    send_sem=send.at[sem], recv_sem=recv.at[sem], device_id=(ox, oy, c), device_id_type=MESH))
                self.in_copies.append(pltpu.make_async_remote_copy(
                    src_ref=dst[a].at[k, mine, :], dst_ref=dst[a].at[k, mine, :],
                    send_sem=send.at[sem], recv_sem=recv.at[sem], device_id=(ox, oy, c), device_id_type=MESH))

    def send(self):
        for cp in self.out_copies:
            cp.start()

    def finish(self):
        for cp in self.in_copies:
            cp.wait_recv()
        for cp in self.out_copies:
            cp.wait_send()


def _scatter_to_chips(arrays):
    n = len(arrays)

    def body(*refs):
        plan = _ScatterPlan(refs[:n], refs[n:2 * n], *refs[2 * n:])
        plan.send()
        plan.finish()

    return pl.pallas_call(
        body, name="scatter_to_chips",
        in_specs=[ANY] * n, out_specs=[ANY] * n,
        out_shape=[jax.ShapeDtypeStruct((3,) + a.shape[1:], a.dtype) for a in arrays],
        scratch_shapes=[pltpu.SemaphoreType.DMA((3 * n,)), pltpu.SemaphoreType.DMA((3 * n,))],
    )(*arrays)


def _allreduce_small(part):
    rows, width = part.shape

    def body(p_ref, out_ref, all_ref, send, recv):
        x, y, c, _ = _place()
        me = 4 * x + 2 * y + c
        all_ref[me] = p_ref[...]
        copies = []
        for k in range(1, N_DEV):
            px, py, pc = x ^ (k >> 2), y ^ ((k >> 1) & 1), c ^ (k & 1)
            copies.append(pltpu.make_async_remote_copy(
                src_ref=p_ref, dst_ref=all_ref.at[me], send_sem=send.at[k - 1], recv_sem=recv.at[k - 1],
                device_id=(px, py, pc), device_id_type=MESH))
        for cp in copies:
            cp.start()
        for cp in copies:
            cp.wait()
        total = all_ref[0]
        for k in range(1, N_DEV):
            total = total + all_ref[k]
        out_ref[...] = total

    return pl.pallas_call(
        body, name="allreduce_small",
        in_specs=[pl.BlockSpec(memory_space=pltpu.VMEM)],
        out_specs=pl.BlockSpec(memory_space=pltpu.VMEM),
        out_shape=jax.ShapeDtypeStruct((rows, width), F32),
        scratch_shapes=[pltpu.VMEM((N_DEV, rows, width), F32),
                        pltpu.SemaphoreType.DMA((N_DEV - 1,)), pltpu.SemaphoreType.DMA((N_DEV - 1,))],
    )(part)


def _row_tile(rows):
    t = min(rows, 256)
    while rows % t:
        t //= 2
    return t


def _core_and_chip():
    return jnp.stack([lax.axis_index("c"), 2 * lax.axis_index("x") + lax.axis_index("y")]).astype(jnp.int32)


def _sum_siblings(own, got, name):
    _, rows, cols = own.shape
    half = rows // 2
    t = _row_tile(half)
    nb = half // t

    def body(place_ref, own_ref, got_ref, mine_ref, out_ref):
        total = own_ref[...] + got_ref[...]
        out_ref[...] = total.astype(BF16)

        @pl.when(pl.program_id(1) == place_ref[1])
        def _():
            mine_ref[...] = total

    slab_blk = pl.BlockSpec((None, t, cols), lambda r, j, place: (j, place[0] * nb + r, 0))
    return pl.pallas_call(
        body, name=name,
        grid_spec=pltpu.PrefetchScalarGridSpec(
            num_scalar_prefetch=1, grid=(nb, N_CHIPS),
            in_specs=[slab_blk, slab_blk],
            out_specs=[pl.BlockSpec((t, cols), lambda r, j, place: (r, 0)), slab_blk]),
        out_shape=[jax.ShapeDtypeStruct((half, cols), F32), jax.ShapeDtypeStruct(own.shape, BF16)],
        compiler_params=_params(("arbitrary", "arbitrary")),
    )(_core_and_chip(), own, got)


def _sum_chips(mine, got, name):
    half, cols = mine.shape
    t = _row_tile(half)
    nb = half // t

    def body(place_ref, mine_ref, got_ref, out_ref):
        total = mine_ref[...]
        for k in range(3):
            total = total + got_ref[k].astype(F32)
        out_ref[...] = total

    return pl.pallas_call(
        body, name=name,
        grid_spec=pltpu.PrefetchScalarGridSpec(
            num_scalar_prefetch=1, grid=(nb,),
            in_specs=[pl.BlockSpec((t, cols), lambda r, place: (r, 0)),
                      pl.BlockSpec((3, t, cols), lambda r, place: (0, place[0] * nb + r, 0))],
            out_specs=pl.BlockSpec((t, cols), lambda r, place: (place[0] * nb + r, 0))),
        out_shape=jax.ShapeDtypeStruct((2 * half, cols), F32),
        compiler_params=_params(("arbitrary",)),
    )(_core_and_chip(), mine, got)


def _adamw(w, g, m, v, name, g_row0=0):
    rows, cols = w.shape
    t = _row_tile(rows)
    assert g_row0 % t == 0
    off = g_row0 // t

    def body(w_ref, g_ref, m_ref, v_ref, go_ref, d_ref, mo_ref, vo_ref):
        g_t = g_ref[...]
        m_new = ADAM_B1 * m_ref[...] + (1.0 - ADAM_B1) * g_t
        v_new = ADAM_B2 * v_ref[...] + (1.0 - ADAM_B2) * (g_t * g_t)
        m_hat = m_new / (1.0 - ADAM_B1 ** ADAM_STEP)
        v_hat = v_new / (1.0 - ADAM_B2 ** ADAM_STEP)
        go_ref[...] = g_t
        d_ref[...] = -ADAM_LR * (m_hat / (jnp.sqrt(v_hat) + ADAM_EPS) + ADAM_WD * w_ref[...])
        mo_ref[...] = m_new
        vo_ref[...] = v_new

    blk = pl.BlockSpec((t, cols), lambda r: (r, 0))
    return pl.pallas_call(
        body, name=name, grid=(rows // t,),
        in_specs=[blk, pl.BlockSpec((t, cols), lambda r: (r + off, 0)), blk, blk],
        out_specs=[blk] * 4,
        out_shape=[jax.ShapeDtypeStruct((rows, cols), F32)] * 4,
        compiler_params=_params(("arbitrary",)),
    )(w, g, m, v)


def _pack_small(d, vectors):
    rows = []
    for vec in vectors:
        flat = vec.reshape(-1)
        n_rows = -(-flat.shape[0] // d)
        rows.append(jnp.pad(flat, (0, n_rows * d - flat.shape[0])).reshape(n_rows, d))
    return jnp.concatenate(rows, axis=0)


def _unpack_small(packed, d, shapes):
    out, row = [], 0
    for shape in shapes:
        size = 1
        for s in shape:
            size *= s
        n_rows = -(-size // d)
        out.append(packed[row:row + n_rows].reshape(-1)[:size].reshape(shape))
        row += n_rows
    return out


def kernel(x, norm_pre, norm_post, pool_w_in, pool_w_group, pool_scale, pool_w_out, att_w_in, att_rel_bias, att_w_out, loss_target, m_norm_pre, m_norm_post, m_pool_w_in, m_pool_w_group, m_pool_scale, m_pool_w_out, m_att_w_in, m_att_rel_bias, m_att_w_out, v_norm_pre, v_norm_post, v_pool_w_in, v_pool_w_group, v_pool_scale, v_pool_w_out, v_att_w_in, v_att_rel_bias, v_att_w_out):
    _, s_len, d = x.shape
    gw = d // 2
    q = gw // N_CHIPS
    x2d = x.reshape(s_len, d)
    target = loss_target.reshape(s_len, d)

    def pack_g(p_group):
        return p_group.reshape(N_CHIPS * q, gw)

    wp_shard = jnp.concatenate([pool_w_in[0], pool_w_out[0]], axis=0).astype(BF16)
    wt_shard = jnp.concatenate([att_w_in[0], att_w_out[0]], axis=0).astype(BF16)
    wp, wg = _gather_weights([wp_shard, pack_g(pool_w_group).astype(BF16)])
    wg = wg.reshape(N_CHIPS, N_CHIPS, q, gw)

    x1, h0, mixed, z0, y0, wt = _l0_fwd(x2d, norm_pre[0:1], norm_post[0:1], pool_scale, wp, wg, wt_shard)
    h1, q_, k_, v_, z1 = _l1_inproj(x1, norm_pre[1:2], wt)
    bias = _bias_tiles(att_rel_bias[0])
    o, lse = _attn_fwd(q_, k_, v_, bias)

    g2, gated1, dy1, d_o, dz1, dgpost1, sq = _l1_out(o, z1, x1, target, norm_post[1:2], wt)
    dq, dk, dv, dbias = _attn_bwd(q_, k_, v_, bias, d_o, o, lse)
    dx1, du1, dgpre1 = _l1_in_bwd(dq, dk, dv, dz1, x1, g2, norm_pre[1:2], wt)
    d_rel = _rel_bias_grad(dbias)

    blk = lambda n: n
    zero = lambda n: 0
    g_ain = _wgrad(h1, du1, "wgrad_att_in", N_CHIPS, d, d, zero, blk)
    g_aout = _wgrad(gated1, dy1, "wgrad_att_out", 1, d, d, zero, zero).reshape(N_CHIPS, d // N_CHIPS, d)
    gt = jnp.concatenate([g_ain, g_aout], axis=1)
    (got_t,) = _swap_halves([gt], "swap_core_partials_att", lambda c: 1 - c, in_place=False)
    mine_t, send_t = _sum_siblings(gt, got_t, "sum_siblings_att")

    grad_x, du0, gated0, dy0, dmm, dgpre0, dgpost0, dscale, recv_t = _l0_bwd(
        dx1, x2d, y0, mixed, z0, norm_pre[0:1], norm_post[0:1], pool_scale, wp, wg, send_t)

    g_pin = _wgrad(h0, du0, "wgrad_pool_in", N_CHIPS, d, d, zero, blk)
    g_pout = _wgrad(gated0, dy0, "wgrad_pool_out", N_CHIPS, gw, d, blk, zero)
    g_pg = _wgrad(mixed, dmm, "wgrad_pool_group", N_CHIPS, gw, gw, blk, blk, out_rows=q)
    gp = jnp.concatenate([g_pin, g_pout], axis=1)
    gg = g_pg.reshape(N_CHIPS, N_CHIPS * q, gw)
    got_p, got_g = _swap_halves([gp, gg], "swap_core_partials_pool", lambda c: 1 - c, in_place=False)
    mine_p, send_p = _sum_siblings(gp, got_p, "sum_siblings_pool")
    mine_g, send_g = _sum_siblings(gg, got_g, "sum_siblings_group")
    recv_p, recv_g = _scatter_to_chips([send_p, send_g])
    red_t = _sum_chips(mine_t, recv_t, "sum_chips_att")
    red_p = _sum_chips(mine_p, recv_p, "sum_chips_pool")
    red_g = _sum_chips(mine_g, recv_g, "sum_chips_group")
    grad_t, grad_p, grad_g = _swap_halves([red_t, red_p, red_g], "swap_reduced_halves", lambda c: c, in_place=True)

    small_shapes = [norm_pre.shape, norm_post.shape, pool_scale.shape, att_rel_bias.shape]
    part = _pack_small(d, [jnp.concatenate([dgpre0, dgpre1], axis=0), jnp.concatenate([dgpost0, dgpost1], axis=0),
                           dscale, d_rel, sq])
    total = _allreduce_small(part)
    loss = (0.5 / d) * jnp.sum(total[-1])
    w_small = _pack_small(d, [norm_pre, norm_post, pool_scale, att_rel_bias, jnp.zeros((d,), F32)])
    m_small = _pack_small(d, [m_norm_pre, m_norm_post, m_pool_scale, m_att_rel_bias, jnp.zeros((d,), F32)])
    v_small = _pack_small(d, [v_norm_pre, v_norm_post, v_pool_scale, v_att_rel_bias, jnp.ones((d,), F32)])
    small_out = [_unpack_small(a, d, small_shapes) for a in _adamw(w_small, total, m_small, v_small, "adamw_small")]

    big = {}
    for name, w, m, v, grad, row0 in [("pool_w_in", pool_w_in, m_pool_w_in, v_pool_w_in, grad_p, 0),
                                      ("pool_w_out", pool_w_out, m_pool_w_out, v_pool_w_out, grad_p, d),
                                      ("att_w_in", att_w_in, m_att_w_in, v_att_w_in, grad_t, 0),
                                      ("att_w_out", att_w_out, m_att_w_out, v_att_w_out, grad_t, d)]:
        outs = _adamw(w[0], grad, m[0], v[0], "adamw_" + name, g_row0=row0)
        big[name] = [a.reshape(w.shape) for a in outs]
    outs = _adamw(pack_g(pool_w_group), grad_g, pack_g(m_pool_w_group), pack_g(v_pool_w_group), "adamw_pool_w_group")
    big["pool_w_group"] = [a.reshape(pool_w_group.shape) for a in outs]

    def leaf(kind):
        return (small_out[kind][0], small_out[kind][1], big["pool_w_in"][kind], big["pool_w_group"][kind],
                small_out[kind][2], big["pool_w_out"][kind], big["att_w_in"][kind], small_out[kind][3],
                big["att_w_out"][kind])

    return (loss, grad_x.reshape(x.shape), *leaf(0), *leaf(1), *leaf(2), *leaf(3))
```

```python
import functools

import jax
import jax.numpy as jnp
from jax import lax
from jax.experimental import pallas as pl
from jax.experimental.pallas import tpu as pltpu

F32 = jnp.float32
BF16 = jnp.bfloat16

RMS_EPS = 1e-6
CHUNK = 64
HEAD_DIM = 64
LEFT_CHUNKS = 8
PAD = LEFT_CHUNKS * CHUNK
MAX_REL = 256
POOL_WINDOWS = (2, 4, 8, 16)
HALO = 16
TS = 256
QB = 256
KB = QB + PAD
HEAD_PAIR = 2 * HEAD_DIM
NEG = -1e30
N_CHIPS = 4
N_DEV = 8

ADAM_LR = 0.001
ADAM_B1 = 0.9
ADAM_B2 = 0.999
ADAM_EPS = 1e-08
ADAM_WD = 0.01
ADAM_STEP = 10

VMEM_LIMIT = 56 * 1024 * 1024
MESH = pl.DeviceIdType.MESH
ANY = pl.BlockSpec(memory_space=pl.ANY)


def _dot(a, b):
    return jnp.dot(a, b, preferred_element_type=F32)


def _dot_nt(a, b):
    return lax.dot_general(a, b, (((1,), (1,)), ((), ())), preferred_element_type=F32)


def _dot_tn(a, b):
    return lax.dot_general(a, b, (((0,), (0,)), ((), ())), preferred_element_type=F32)


def _params(sem, limit=VMEM_LIMIT):
    return pltpu.CompilerParams(dimension_semantics=sem, vmem_limit_bytes=limit)


def _rms(x):
    return lax.rsqrt(jnp.mean(x * x, axis=-1, keepdims=True) + RMS_EPS)


def _rms_bwd(dyh, xh, r):
    return r * (dyh - xh * jnp.mean(dyh * xh, axis=-1, keepdims=True))


def _silu_parts(z):
    sig = jax.nn.sigmoid(z)
    return z * sig, sig * (1.0 + z * (1.0 - sig))


def _inv_count(tile, rows, w):
    t = tile * rows + lax.broadcasted_iota(jnp.int32, (rows, 1), 0)
    return 1.0 / jnp.minimum(t + 1, w).astype(F32)


def _start_weight_copies(copies):
    for c in copies:
        c.start()
    for c in copies:
        c.wait()


def _l0_fwd(x, g_pre, g_post, scale, wa, wg, next_shard):
    s_len, d = x.shape
    pw, gw = 2 * d, d // 2
    q = gw // N_CHIPS
    nt = s_len // TS
    assert nt >= 3

    def body(x_ref, gpre_ref, gpost_ref, sc_ref, wa_ref, wg_ref, shard_ref,
             x1_ref, h_ref, mixed_ref, z_ref, y_ref, next_ref,
             win_s, wout_s, wg_s, halo_s, sems, *gather_sems):
        i = pl.program_id(0)

        @pl.when(i == 0)
        def _():
            _GatherPlan([shard_ref], [next_ref], *gather_sems).send()
            copies = [pltpu.make_async_copy(wa_ref.at[:, pl.ds(0, d), :], win_s, sems.at[0]),
                      pltpu.make_async_copy(wa_ref.at[:, pl.ds(d, gw), :], wout_s, sems.at[1])]
            copies += [pltpu.make_async_copy(wg_ref.at[j], wg_s.at[:, pl.ds(j * q, q), :], sems.at[2 + j])
                       for j in range(N_CHIPS)]
            _start_weight_copies(copies)
            halo_s[...] = jnp.zeros_like(halo_s)

        @pl.when(i == nt // 2)
        def _():
            _GatherPlan([shard_ref], [next_ref], *gather_sems).pass_on()

        @pl.when(i == nt - 1)
        def _():
            _GatherPlan([shard_ref], [next_ref], *gather_sems).finish()

        xt = x_ref[...]
        h = ((xt * _rms(xt)) * gpre_ref[...]).astype(BF16)
        h_ref[...] = h
        a_blocks = [_dot(h, win_s[0]), _dot(h, win_s[1])]
        y = jnp.zeros((TS, d), F32)
        for g, w in enumerate(POOL_WINDOWS):
            a_g = a_blocks[g // 2][:, (g % 2) * gw:(g % 2 + 1) * gw]
            ext = jnp.concatenate([halo_s[g], a_g], axis=0)
            shift = 1
            while shift < w:
                ext = ext + pltpu.roll(ext, shift, 0)
                shift *= 2
            mixed = (ext[HALO:] * _inv_count(i, TS, w) - a_g).astype(BF16)
            halo_s[g] = a_g[TS - HALO:]
            mixed_ref[:, g * gw:(g + 1) * gw] = mixed
            z_g = _dot(h, win_s[2 + g // 2, :, (g % 2) * gw:(g % 2 + 1) * gw])
            z_ref[:, g * gw:(g + 1) * gw] = z_g
            ms = _dot(mixed, wg_s[g]) * sc_ref[:, g * gw:(g + 1) * gw]
            gated = (ms * _silu_parts(z_g)[0]).astype(BF16)
            y = y + _dot(gated, wout_s[g])
        y_ref[...] = y
        x1_ref[...] = xt + (y * _rms(y)) * gpost_ref[...]

    tile = lambda wdt: pl.BlockSpec((TS, wdt), lambda i: (i, 0))
    row = lambda wdt: pl.BlockSpec((1, wdt), lambda i: (0, 0))
    return pl.pallas_call(
        body, name="l0_fwd", grid=(nt,),
        in_specs=[tile(d), row(d), row(d), row(pw), ANY, ANY, ANY],
        out_specs=[tile(d), tile(d), tile(pw), tile(pw), tile(d), ANY],
        out_shape=[jax.ShapeDtypeStruct((s_len, d), F32), jax.ShapeDtypeStruct((s_len, d), BF16),
                   jax.ShapeDtypeStruct((s_len, pw), BF16), jax.ShapeDtypeStruct((s_len, pw), F32),
                   jax.ShapeDtypeStruct((s_len, d), F32),
                   jax.ShapeDtypeStruct((N_CHIPS,) + next_shard.shape, next_shard.dtype)],
        scratch_shapes=[pltpu.VMEM((N_CHIPS, d, d), BF16), pltpu.VMEM((N_CHIPS, gw, d), BF16),
                        pltpu.VMEM((N_CHIPS, gw, gw), BF16), pltpu.VMEM((N_CHIPS, HALO, gw), F32),
                        pltpu.SemaphoreType.DMA((2 + N_CHIPS,))] + _gather_sems(1),
        compiler_params=_params(("arbitrary",)),
    )(x, g_pre, g_post, scale, wa, wg, next_shard)


def _l0_bwd(dx1, x, y, mixed, z, g_pre, g_post, scale, wa, wg):
    s_len, d = x.shape
    pw, gw = 2 * d, d // 2
    q = gw // N_CHIPS
    nt = s_len // TS

    def body(dx1_ref, x_ref, y_ref, mixed_ref, z_ref, gpre_ref, gpost_ref, sc_ref, wa_ref, wg_ref,
             gx_ref, du_ref, gated_ref, dy_ref, dmm_ref, dgpre_ref, dgpost_ref, dsc_ref,
             win_s, wout_s, wg_s, halo_s, sems):
        i = pl.program_id(0)
        tile = nt - 1 - i

        @pl.when(i == 0)
        def _():
            copies = [pltpu.make_async_copy(wa_ref.at[:, pl.ds(0, d), :], win_s, sems.at[0]),
                      pltpu.make_async_copy(wa_ref.at[:, pl.ds(d, gw), :], wout_s, sems.at[1])]
            copies += [pltpu.make_async_copy(wg_ref.at[j], wg_s.at[:, pl.ds(j * q, q), :], sems.at[2 + j])
                       for j in range(N_CHIPS)]
            _start_weight_copies(copies)
            halo_s[...] = jnp.zeros_like(halo_s)
            dgpre_ref[...] = jnp.zeros_like(dgpre_ref)
            dgpost_ref[...] = jnp.zeros_like(dgpost_ref)
            dsc_ref[...] = jnp.zeros_like(dsc_ref)

        g_in = dx1_ref[...]
        yt = y_ref[...]
        r_y = _rms(yt)
        yh = yt * r_y
        dgpost_ref[...] += jnp.sum(g_in * yh, axis=0, keepdims=True)
        dy = _rms_bwd(g_in * gpost_ref[...], yh, r_y).astype(BF16)
        dy_ref[...] = dy
        dh = jnp.zeros((TS, d), F32)
        for g, w in enumerate(POOL_WINDOWS):
            cols = slice(g * gw, (g + 1) * gw)
            dgated = _dot_nt(dy, wout_s[g])
            mm = _dot(mixed_ref[:, cols], wg_s[g])
            sc = sc_ref[:, cols]
            ms = mm * sc
            z_g = z_ref[:, cols]
            sz, dsz = _silu_parts(z_g)
            gated_ref[:, cols] = (ms * sz).astype(BF16)
            dms = dgated * sz
            dz = (dgated * ms * dsz).astype(BF16)
            dsc_ref[:, cols] += jnp.sum(dms * mm, axis=0, keepdims=True)
            dmm = (dms * sc).astype(BF16)
            dmm_ref[:, cols] = dmm
            dmixed = _dot_nt(dmm, wg_s[g])
            e = dmixed * _inv_count(tile, TS, w)
            ext = jnp.concatenate([e, halo_s[g]], axis=0)
            shift = 1
            while shift < w:
                ext = ext + pltpu.roll(ext, TS + HALO - shift, 0)
                shift *= 2
            da = (ext[:TS] - dmixed).astype(BF16)
            halo_s[g] = e[:HALO]
            du_ref[:, cols] = da
            du_ref[:, pw + g * gw:pw + (g + 1) * gw] = dz
            wa_blk = win_s[g // 2, :, (g % 2) * gw:(g % 2 + 1) * gw]
            wz_blk = win_s[2 + g // 2, :, (g % 2) * gw:(g % 2 + 1) * gw]
            dh = dh + _dot_nt(da, wa_blk) + _dot_nt(dz, wz_blk)
        xt = x_ref[...]
        r_x = _rms(xt)
        xh = xt * r_x
        dgpre_ref[...] += jnp.sum(dh * xh, axis=0, keepdims=True)
        gx_ref[...] = g_in + _rms_bwd(dh * gpre_ref[...], xh, r_x)

    tile_spec = lambda wdt: pl.BlockSpec((TS, wdt), lambda i: (nt - 1 - i, 0))
    row = lambda wdt: pl.BlockSpec((1, wdt), lambda i: (0, 0))
    return pl.pallas_call(
        body, name="l0_bwd", grid=(nt,),
        in_specs=[tile_spec(d), tile_spec(d), tile_spec(d), tile_spec(pw), tile_spec(pw),
                  row(d), row(d), row(pw), ANY, ANY],
        out_specs=[tile_spec(d), tile_spec(2 * pw), tile_spec(pw), tile_spec(d), tile_spec(pw),
                   row(d), row(d), row(pw)],
        out_shape=[jax.ShapeDtypeStruct((s_len, d), F32), jax.ShapeDtypeStruct((s_len, 2 * pw), BF16),
                   jax.ShapeDtypeStruct((s_len, pw), BF16), jax.ShapeDtypeStruct((s_len, d), BF16),
                   jax.ShapeDtypeStruct((s_len, pw), BF16),
                   jax.ShapeDtypeStruct((1, d), F32), jax.ShapeDtypeStruct((1, d), F32),
                   jax.ShapeDtypeStruct((1, pw), F32)],
        scratch_shapes=[pltpu.VMEM((N_CHIPS, d, d), BF16), pltpu.VMEM((N_CHIPS, gw, d), BF16),
                        pltpu.VMEM((N_CHIPS, gw, gw), BF16), pltpu.VMEM((N_CHIPS, HALO, gw), F32),
                        pltpu.SemaphoreType.DMA((2 + N_CHIPS,))],
        compiler_params=_params(("arbitrary",)),
    )(dx1, x, y, mixed, z, g_pre, g_post, scale, wa, wg)


def _l1_inproj(x1, g_pre, wa):
    s_len, d = x1.shape
    nt = s_len // TS
    npad = PAD // TS

    def body(x_ref, gpre_ref, wa_ref, h_ref, q_ref, k_ref, v_ref, z_ref, win_s, sem):
        i = pl.program_id(0)

        @pl.when(i == 0)
        def _():
            _start_weight_copies([pltpu.make_async_copy(wa_ref.at[:, pl.ds(0, d), :], win_s, sem.at[0])])

        @pl.when(i < npad)
        def _():
            k_ref[...] = jnp.zeros_like(k_ref)
            v_ref[...] = jnp.zeros_like(v_ref)

        @pl.when(i >= npad)
        def _():
            xt = x_ref[...]
            h = ((xt * _rms(xt)) * gpre_ref[...]).astype(BF16)
            h_ref[...] = h
            q_ref[...] = _dot(h, win_s[0]).astype(BF16)
            k_ref[...] = _dot(h, win_s[1]).astype(BF16)
            v_ref[...] = _dot(h, win_s[2]).astype(BF16)
            z_ref[...] = _dot(h, win_s[3])

    tile = pl.BlockSpec((TS, d), lambda i: (jnp.maximum(i - npad, 0), 0))
    padded = pl.BlockSpec((TS, d), lambda i: (i, 0))
    return pl.pallas_call(
        body, name="l1_inproj", grid=(nt + npad,),
        in_specs=[tile, pl.BlockSpec((1, d), lambda i: (0, 0)), ANY],
        out_specs=[tile, tile, padded, padded, tile],
        out_shape=[jax.ShapeDtypeStruct((s_len, d), BF16), jax.ShapeDtypeStruct((s_len, d), BF16),
                   jax.ShapeDtypeStruct((PAD + s_len, d), BF16), jax.ShapeDtypeStruct((PAD + s_len, d), BF16),
                   jax.ShapeDtypeStruct((s_len, d), F32)],
        scratch_shapes=[pltpu.VMEM((N_CHIPS, d, d), BF16), pltpu.SemaphoreType.DMA((1,))],
        compiler_params=_params(("arbitrary",)),
    )(x1, g_pre, wa)


SKEW = QB + KB


def _bias_tiles(rel_bias):
    nh = rel_bias.shape[0]
    assert QB == MAX_REL
    by_column = jnp.concatenate([jnp.broadcast_to(rel_bias[:, 2 * MAX_REL:], (nh, PAD + 1)),
                                 jnp.flip(rel_bias[:, 1:2 * MAX_REL], axis=1)], axis=1).astype(F32)

    def body(col_ref, out_ref):
        rows = jnp.broadcast_to(col_ref[pl.ds(pl.program_id(0), 1), :], (QB, SKEW))
        tile = pltpu.roll(rows, SKEW - QB, 1, stride=1, stride_axis=0)[:, :KB]
        i = lax.broadcasted_iota(jnp.int32, (QB, KB), 0)
        j = lax.broadcasted_iota(jnp.int32, (QB, KB), 1)
        first = (i // CHUNK) * CHUNK
        out_ref[...] = jnp.where((j >= first) & (j < first + PAD + CHUNK), tile, NEG)

    return pl.pallas_call(
        body, name="bias_tiles", grid=(nh,),
        in_specs=[pl.BlockSpec((nh, SKEW), lambda h: (0, 0))],
        out_specs=pl.BlockSpec((None, QB, KB), lambda h: (h, 0, 0)),
        out_shape=jax.ShapeDtypeStruct((nh, QB, KB), F32),
        compiler_params=_params(("arbitrary",)),
    )(by_column)


ROWS = 16


LANES = 128


def _row_blocks():
    out = []
    for r in range(QB // ROWS):
        first = (r * ROWS // CHUNK) * CHUNK
        lo = first // LANES * LANES
        hi = -(-(first + PAD + CHUNK) // LANES) * LANES
        out.append((pl.ds(r * ROWS, ROWS), lo, hi - lo))
    return out


def _zero_outside_bands(*refs):
    for rows, lo, width in _row_blocks():
        for ref in refs:
            if lo > 0:
                ref[rows, :lo] = jnp.zeros((ROWS, lo), ref.dtype)
            if lo + width < KB:
                ref[rows, lo + width:] = jnp.zeros((ROWS, KB - lo - width), ref.dtype)


def _attn_fwd(q, k, v, bias):
    s_len, d = q.shape
    nhp = d // HEAD_PAIR
    nq = s_len // QB

    def body(q_ref, k_ref, v_ref, b_ref, o_ref, lse_ref, s_s, p_s, l_s):
        start = pl.multiple_of(pl.program_id(1) * QB, QB)
        qt = q_ref[...] * (HEAD_DIM ** -0.5)
        kb = k_ref[pl.ds(start, KB), :]
        vb = v_ref[pl.ds(start, KB), :]
        head_of_lane = lax.broadcasted_iota(jnp.int32, (1, HEAD_PAIR), 1) // HEAD_DIM
        colvalid = (start + lax.broadcasted_iota(jnp.int32, (1, KB), 1)) >= PAD
        @pl.when(pl.program_id(1) == 0)
        def _():
            _zero_outside_bands(p_s)

        outs, lses = [], []
        for hh in range(2):
            s_s[...] = _dot_nt(jnp.where(head_of_lane == hh, qt, jnp.zeros_like(qt)), kb)
            for rows, lo, width in _row_blocks():
                cols = slice(lo, lo + width)
                s = jnp.where(colvalid[:, cols], s_s[rows, cols] + b_ref[hh, rows, cols], NEG)
                m = jnp.max(s, axis=-1, keepdims=True)
                e = jnp.exp(s - m)
                l = jnp.sum(e, axis=-1, keepdims=True)
                p_s[rows, cols] = (e * (1.0 / l)).astype(BF16)
                l_s[rows, :] = m + jnp.log(l)
            outs.append(_dot(p_s[...], vb))
            lses.append(l_s[...])
        o_ref[...] = jnp.where(head_of_lane == 0, outs[0], outs[1])
        lse_ref[...] = jnp.where(head_of_lane == 0, lses[0], lses[1])

    blk = pl.BlockSpec((QB, HEAD_PAIR), lambda hp, b: (b, hp))
    whole = pl.BlockSpec((PAD + s_len, HEAD_PAIR), lambda hp, b: (0, hp))
    return pl.pallas_call(
        body, name="attn_fwd", grid=(nhp, nq),
        in_specs=[blk, whole, whole, pl.BlockSpec((2, QB, KB), lambda hp, b: (hp, 0, 0))],
        out_specs=[blk, blk],
        out_shape=[jax.ShapeDtypeStruct((s_len, d), F32), jax.ShapeDtypeStruct((s_len, d), F32)],
        scratch_shapes=[pltpu.VMEM((QB, KB), F32), pltpu.VMEM((QB, KB), BF16), pltpu.VMEM((QB, 1), F32)],
        compiler_params=_params(("arbitrary", "arbitrary")),
    )(q, k, v, bias)


def _attn_bwd(q, k, v, bias, d_o, o, lse):
    s_len, d = q.shape
    nhp = d // HEAD_PAIR
    nq = s_len // QB
    qk_scale = HEAD_DIM ** -0.5

    def body(q_ref, k_ref, v_ref, b_ref, do_ref, o_ref, lse_ref, dq_ref, dk_ref, dv_ref, db_ref,
             s_s, dp_s, p_s, ds_s, lse_s, delta_s):
        b = pl.program_id(1)
        start = pl.multiple_of(b * QB, QB)

        @pl.when(b == 0)
        def _():
            dk_ref[...] = jnp.zeros_like(dk_ref)
            dv_ref[...] = jnp.zeros_like(dv_ref)
            db_ref[...] = jnp.zeros_like(db_ref)
            _zero_outside_bands(p_s, ds_s)

        qt = q_ref[...] * qk_scale
        dot_ = do_ref[...]
        kb = k_ref[pl.ds(start, KB), :]
        vb = v_ref[pl.ds(start, KB), :]
        head_of_lane = lax.broadcasted_iota(jnp.int32, (1, HEAD_PAIR), 1) // HEAD_DIM
        colvalid = (start + lax.broadcasted_iota(jnp.int32, (1, KB), 1)) >= PAD
        do_o = dot_.astype(F32) * o_ref[...]
        dq_heads = []
        dk_band = jnp.zeros((KB, HEAD_PAIR), F32)
        dv_band = jnp.zeros((KB, HEAD_PAIR), F32)
        for hh in range(2):
            mine = head_of_lane == hh
            q_m = jnp.where(mine, qt, jnp.zeros_like(qt))
            do_m = jnp.where(mine, dot_, jnp.zeros_like(dot_))
            delta_s[...] = jnp.sum(jnp.where(mine, do_o, 0.0), axis=-1, keepdims=True)
            lse_s[...] = lse_ref[:, hh * HEAD_DIM:hh * HEAD_DIM + 1]
            s_s[...] = _dot_nt(q_m, kb)
            dp_s[...] = _dot_nt(do_m, vb)
            for rows, lo, width in _row_blocks():
                cols = slice(lo, lo + width)
                t = jnp.where(colvalid[:, cols], s_s[rows, cols] + b_ref[hh, rows, cols] - lse_s[rows, :], NEG)
                p = jnp.exp(t)
                ds = p * (dp_s[rows, cols] - delta_s[rows, :])
                db_ref[hh, rows, cols] += ds
                p_s[rows, cols] = p.astype(BF16)
                ds_s[rows, cols] = ds.astype(BF16)
            dv_band = dv_band + _dot_tn(p_s[...], do_m)
            dq_heads.append(_dot(ds_s[...], kb) * qk_scale)
            dk_band = dk_band + _dot_tn(ds_s[...], q_m)
        dq_ref[...] = jnp.where(head_of_lane == 0, dq_heads[0], dq_heads[1]).astype(BF16)
        dk_ref[pl.ds(start, KB), :] += dk_band
        dv_ref[pl.ds(start, KB), :] += dv_band

    blk = pl.BlockSpec((QB, HEAD_PAIR), lambda hp, b: (b, hp))
    whole = pl.BlockSpec((PAD + s_len, HEAD_PAIR), lambda hp, b: (0, hp))
    btile = pl.BlockSpec((2, QB, KB), lambda hp, b: (hp, 0, 0))
    return pl.pallas_call(
        body, name="attn_bwd", grid=(nhp, nq),
        in_specs=[blk, whole, whole, btile, blk, blk, blk],
        out_specs=[blk, whole, whole, btile],
        out_shape=[jax.ShapeDtypeStruct((s_len, d), BF16),
                   jax.ShapeDtypeStruct((PAD + s_len, d), F32), jax.ShapeDtypeStruct((PAD + s_len, d), F32),
                   jax.ShapeDtypeStruct(bias.shape, F32)],
        scratch_shapes=[pltpu.VMEM((QB, KB), F32), pltpu.VMEM((QB, KB), F32),
                        pltpu.VMEM((QB, KB), BF16), pltpu.VMEM((QB, KB), BF16),
                        pltpu.VMEM((QB, 1), F32), pltpu.VMEM((QB, 1), F32)],
        compiler_params=_params(("arbitrary", "arbitrary")),
    )(q, k, v, bias, d_o, o, lse)


def _rel_bias_grad(db):
    nh = db.shape[0]
    assert QB == MAX_REL

    def body(db_ref, out_ref):
        m = jnp.concatenate([db_ref[...], jnp.zeros((QB, SKEW - KB), F32)], axis=1)
        amount = QB - 1 - lax.broadcasted_iota(jnp.int32, (QB, 1), 0)
        m = pltpu.roll(m, 1, 1)
        bit = 1
        while bit < QB:
            m = jnp.where((amount & bit) != 0, pltpu.roll(m, bit, 1), m)
            bit *= 2
        diag = jnp.sum(m, axis=0, keepdims=True)
        c = lax.broadcasted_iota(jnp.int32, (1, SKEW), 1)
        clipped = jnp.sum(jnp.where(c <= PAD, diag, 0.0), axis=1, keepdims=True)
        out_ref[...] = jnp.where(c == 0, clipped, diag)

    diag = pl.pallas_call(
        body, name="bias_diagonals", grid=(nh,),
        in_specs=[pl.BlockSpec((None, QB, KB), lambda h: (h, 0, 0))],
        out_specs=pl.BlockSpec((None, 1, SKEW), lambda h: (h, 0, 0)),
        out_shape=jax.ShapeDtypeStruct((nh, 1, SKEW), F32),
        compiler_params=_params(("arbitrary",)),
    )(db)[:, 0]
    return jnp.concatenate([jnp.zeros((nh, 1), F32), jnp.flip(diag[:, PAD + 1:], axis=1), diag[:, :1]], axis=1)


def _l1_out(o, z, x1, target, g_post, wa):
    s_len, d = o.shape
    nt = s_len // TS
    slab = d // N_CHIPS

    def body(o_ref, z_ref, x1_ref, t_ref, gpost_ref, wa_ref,
             g2_ref, gated_ref, dy_ref, do_ref, dz_ref, dgpost_ref, sq_ref, wout_s, sems):
        i = pl.program_id(0)

        @pl.when(i == 0)
        def _():
            _start_weight_copies([
                pltpu.make_async_copy(wa_ref.at[j, pl.ds(d, slab), :],
                                      wout_s.at[pl.ds(j * slab, slab), :], sems.at[j])
                for j in range(N_CHIPS)])
            dgpost_ref[...] = jnp.zeros_like(dgpost_ref)
            sq_ref[...] = jnp.zeros_like(sq_ref)

        ot = o_ref[...]
        sz, dsz = _silu_parts(z_ref[...])
        gated = (ot * sz).astype(BF16)
        gated_ref[...] = gated
        y = _dot(gated, wout_s[...])
        r_y = _rms(y)
        yh = y * r_y
        err = x1_ref[...] + yh * gpost_ref[...] - t_ref[...]
        sq_ref[...] += jnp.sum(err * err, axis=0, keepdims=True)
        g2 = err * (1.0 / d)
        g2_ref[...] = g2
        dgpost_ref[...] += jnp.sum(g2 * yh, axis=0, keepdims=True)
        dy = _rms_bwd(g2 * gpost_ref[...], yh, r_y).astype(BF16)
        dy_ref[...] = dy
        dgated = _dot_nt(dy, wout_s[...])
        do_ref[...] = (dgated * sz).astype(BF16)
        dz_ref[...] = (dgated * ot * dsz).astype(BF16)

    tile = pl.BlockSpec((TS, d), lambda i: (i, 0))
    row = pl.BlockSpec((1, d), lambda i: (0, 0))
    return pl.pallas_call(
        body, name="l1_out", grid=(nt,),
        in_specs=[tile, tile, tile, tile, row, ANY],
        out_specs=[tile, tile, tile, tile, tile, row, row],
        out_shape=[jax.ShapeDtypeStruct((s_len, d), F32)] + [jax.ShapeDtypeStruct((s_len, d), BF16)] * 4
                  + [jax.ShapeDtypeStruct((1, d), F32)] * 2,
        scratch_shapes=[pltpu.VMEM((d, d), BF16), pltpu.SemaphoreType.DMA((N_CHIPS,))],
        compiler_params=_params(("arbitrary",)),
    )(o, z, x1, target, g_post, wa)


def _l1_in_bwd(dq, dk, dv, dz, x1, g2, g_pre, wa):
    s_len, d = x1.shape
    nt = s_len // TS
    npad = PAD // TS

    def body(dq_ref, dk_ref, dv_ref, dz_ref, x_ref, g2_ref, gpre_ref, wa_ref,
             dx_ref, du_ref, dgpre_ref, win_s, sem):
        i = pl.program_id(0)

        @pl.when(i == 0)
        def _():
            _start_weight_copies([pltpu.make_async_copy(wa_ref.at[:, pl.ds(0, d), :], win_s, sem.at[0])])
            dgpre_ref[...] = jnp.zeros_like(dgpre_ref)

        parts = [dq_ref[...], dk_ref[...].astype(BF16), dv_ref[...].astype(BF16), dz_ref[...]]
        dh = jnp.zeros((TS, d), F32)
        for j, part in enumerate(parts):
            du_ref[:, j * d:(j + 1) * d] = part
            dh = dh + _dot_nt(part, win_s[j])
        xt = x_ref[...]
        r_x = _rms(xt)
        xh = xt * r_x
        dgpre_ref[...] += jnp.sum(dh * xh, axis=0, keepdims=True)
        dx_ref[...] = g2_ref[...] + _rms_bwd(dh * gpre_ref[...], xh, r_x)

    tile = pl.BlockSpec((TS, d), lambda i: (i, 0))
    padded = pl.BlockSpec((TS, d), lambda i: (i + npad, 0))
    row = pl.BlockSpec((1, d), lambda i: (0, 0))
    return pl.pallas_call(
        body, name="l1_in_bwd", grid=(nt,),
        in_specs=[tile, padded, padded, tile, tile, tile, row, ANY],
        out_specs=[tile, pl.BlockSpec((TS, 4 * d), lambda i: (i, 0)), row],
        out_shape=[jax.ShapeDtypeStruct((s_len, d), F32), jax.ShapeDtypeStruct((s_len, 4 * d), BF16),
                   jax.ShapeDtypeStruct((1, d), F32)],
        scratch_shapes=[pltpu.VMEM((N_CHIPS, d, d), BF16), pltpu.SemaphoreType.DMA((1,))],
        compiler_params=_params(("arbitrary",)),
    )(dq, dk, dv, dz, x1, g2, g_pre, wa)


def _wgrad(a, b, name, n_out, a_width, b_width, a_block, b_block, out_shape, out_spec, into=None, rider=None):
    s_len = a.shape[0]
    tk = min(1024, s_len)
    nk = s_len // tk
    n_in = 2 + (into is not None) + (rider is not None)

    def body(*refs):
        a_ref, b_ref = refs[:2]
        out_ref = refs[n_in]
        if rider is not None:
            plan = lambda: rider[0]([refs[n_in - 1]], [refs[n_in + 1]], *refs[n_in + 2:])
            first = (pl.program_id(0) == 0) & (pl.program_id(1) == 0)
            last = (pl.program_id(0) == n_out - 1) & (pl.program_id(1) == nk - 1)

            @pl.when(first)
            def _():
                plan().send()

        @pl.when(pl.program_id(1) == 0)
        def _():
            out_ref[...] = jnp.zeros_like(out_ref)

        out_ref[...] += _dot_tn(a_ref[...], b_ref[...]).reshape(out_ref.shape)

        if rider is not None:
            @pl.when(last)
            def _():
                plan().finish()

    operands = [a, b] + ([into] if into is not None else []) + ([rider[1]] if rider is not None else [])
    results = pl.pallas_call(
        body, name=name, grid=(n_out, nk),
        in_specs=[pl.BlockSpec((tk, a_width), lambda n, kk: (kk, a_block(n))),
                  pl.BlockSpec((tk, b_width), lambda n, kk: (kk, b_block(n)))] + [ANY] * (n_in - 2),
        out_specs=[out_spec] + ([ANY] if rider is not None else []),
        out_shape=[jax.ShapeDtypeStruct(out_shape, F32)]
                  + ([jax.ShapeDtypeStruct(rider[2], rider[1].dtype)] if rider is not None else []),
        scratch_shapes=[pltpu.SemaphoreType.DMA((n,)) for n in rider[3]] if rider is not None else [],
        input_output_aliases={2: 0} if into is not None else {},
        compiler_params=_params(("arbitrary", "arbitrary")),
    )(*operands)
    return results if rider is not None else results[0]


def _place():
    x, y, c = lax.axis_index("x"), lax.axis_index("y"), lax.axis_index("c")
    others = [(1 - x, y), (x, 1 - y), (1 - x, 1 - y)]
    return x, y, c, others


class _GatherPlan:
    def __init__(self, src, dst, send, recv, fwd_send, fwd_recv, local):
        x, y, c, others = _place()
        me = 2 * x + y
        sibling = (x, y, 1 - c)

        def half(ref, cc):
            rows = ref.shape[0] // 2
            return ref.at[pl.ds(cc * rows, rows), :]

        self.mine = [pltpu.make_async_copy(src[a], dst[a].at[me], local.at[a]) for a in range(len(src))]
        self.first, self.passed, self.arrive, self.arrive_fwd = [], [], [], []
        for a in range(len(src)):
            for k, (ox, oy) in enumerate(others):
                sem = a * 3 + k
                self.first.append(pltpu.make_async_remote_copy(
                    src_ref=half(src[a], c), dst_ref=half(dst[a].at[me], c),
                    send_sem=send.at[sem], recv_sem=recv.at[sem], device_id=(ox, oy, c), device_id_type=MESH))
                theirs = half(dst[a].at[2 * ox + oy], c)
                self.arrive.append(pltpu.make_async_remote_copy(
                    src_ref=theirs, dst_ref=theirs, send_sem=send.at[sem], recv_sem=recv.at[sem],
                    device_id=(ox, oy, c), device_id_type=MESH))
                self.passed.append(pltpu.make_async_remote_copy(
                    src_ref=theirs, dst_ref=theirs, send_sem=fwd_send.at[sem], recv_sem=fwd_recv.at[sem],
                    device_id=sibling, device_id_type=MESH))
                other_half = half(dst[a].at[2 * ox + oy], 1 - c)
                self.arrive_fwd.append(pltpu.make_async_remote_copy(
                    src_ref=other_half, dst_ref=other_half, send_sem=fwd_send.at[sem], recv_sem=fwd_recv.at[sem],
                    device_id=sibling, device_id_type=MESH))

    def send(self):
        for cp in self.mine + self.first:
            cp.start()

    def pass_on(self):
        for got, onward in zip(self.arrive, self.passed):
            got.wait_recv()
            onward.start()

    def finish(self):
        for got in self.arrive_fwd:
            got.wait_recv()
        for cp in self.first + self.passed:
            cp.wait_send()
        for cp in self.mine:
            cp.wait()


def _gather_sems(n):
    return [pltpu.SemaphoreType.DMA((3 * n,))] * 4 + [pltpu.SemaphoreType.DMA((n,))]


def _gather_weights(shards):
    n = len(shards)

    def body(*refs):
        plan = _GatherPlan(refs[:n], refs[n:2 * n], *refs[2 * n:])
        plan.send()
        plan.pass_on()
        plan.finish()

    return pl.pallas_call(
        body, name="gather_weights",
        in_specs=[ANY] * n, out_specs=[ANY] * n,
        out_shape=[jax.ShapeDtypeStruct((N_CHIPS,) + s.shape, s.dtype) for s in shards],
        scratch_shapes=_gather_sems(n),
    )(*shards)


class _SwapPlan:
    def __init__(self, src, dst, send, recv, send_half):
        x, y, c, _ = _place()
        sibling = (x, y, 1 - c)
        self.out_copies, self.in_copies = [], []
        for a in range(len(src)):
            rows = src[a].shape[-2] // 2
            lead = (slice(None),) * (len(src[a].shape) - 2)
            going = lead + (pl.ds(send_half(c) * rows, rows), slice(None))
            coming = lead + (pl.ds((1 - send_half(c)) * rows, rows), slice(None))
            self.out_copies.append(pltpu.make_async_remote_copy(
                src_ref=src[a].at[going], dst_ref=dst[a].at[going], send_sem=send.at[a], recv_sem=recv.at[a],
                device_id=sibling, device_id_type=MESH))
            self.in_copies.append(pltpu.make_async_remote_copy(
                src_ref=src[a].at[coming], dst_ref=dst[a].at[coming], send_sem=send.at[a], recv_sem=recv.at[a],
                device_id=sibling, device_id_type=MESH))

    def send(self):
        for cp in self.out_copies:
            cp.start()

    def finish(self):
        for cp in self.in_copies:
            cp.wait_recv()
        for cp in self.out_copies:
            cp.wait_send()


def _keep_own_half(c):
    return 1 - c


def _swap_halves(arrays, name, send_half, in_place):
    n = len(arrays)

    def body(*refs):
        src, dst = refs[:n], refs[n:2 * n]
        plan = _SwapPlan(dst if in_place else src, dst, *refs[2 * n:], send_half=send_half)
        plan.send()
        plan.finish()

    return pl.pallas_call(
        body, name=name,
        in_specs=[ANY] * n, out_specs=[ANY] * n,
        out_shape=[jax.ShapeDtypeStruct(a.shape, a.dtype) for a in arrays],
        scratch_shapes=[pltpu.SemaphoreType.DMA((n,)), pltpu.SemaphoreType.DMA((n,))],
        input_output_aliases={a: a for a in range(n)} if in_place else {},
    )(*arrays)


class _ScatterPlan:
    def __init__(self, src, dst, send, recv):
        x, y, c, others = _place()
        self.out_copies, self.in_copies = [], []
        for a in range(len(src)):
            rows = src[a].shape[1] // 2
            mine = pl.ds(c * rows, rows)
            for k, (ox, oy) in enumerate(others):
                sem = a * 3 + k
                self.out_copies.append(pltpu.make_async_remote_copy(
                    src_ref=src[a].at[2 * ox + oy, mine, :], dst_ref=dst[a].at[k, mine, :],
                    send_sem=send.at[sem], recv_sem=recv.at[sem], device_id=(ox, oy, c), device_id_type=MESH))
                self.in_copies.append(pltpu.make_async_remote_copy(
                    src_ref=dst[a].at[k, mine, :], dst_ref=dst[a].at[k, mine, :],
                    send_sem=send.at[sem], recv_sem=recv.at[sem], device_id=(ox, oy, c), device_id_type=MESH))

    def send(self):
        for cp in self.out_copies:
            cp.start()

    def finish(self):
        for cp in self.in_copies:
            cp.wait_recv()
        for cp in self.out_copies:
            cp.wait_send()


def _scatter_to_chips(arrays):
    n = len(arrays)

    def body(*refs):
        plan = _ScatterPlan(refs[:n], refs[n:2 * n], *refs[2 * n:])
        plan.send()
        plan.finish()

    return pl.pallas_call(
        body, name="scatter_to_chips",
        in_specs=[ANY] * n, out_specs=[ANY] * n,
        out_shape=[jax.ShapeDtypeStruct((3,) + a.shape[1:], a.dtype) for a in arrays],
        scratch_shapes=[pltpu.SemaphoreType.DMA((3 * n,)), pltpu.SemaphoreType.DMA((3 * n,))],
    )(*arrays)


def _allreduce_small(part):
    rows, width = part.shape

    def body(p_ref, out_ref, all_ref, send, recv):
        x, y, c, _ = _place()
        me = 4 * x + 2 * y + c
        all_ref[me] = p_ref[...]
        copies = []
        for k in range(1, N_DEV):
            px, py, pc = x ^ (k >> 2), y ^ ((k >> 1) & 1), c ^ (k & 1)
            copies.append(pltpu.make_async_remote_copy(
                src_ref=p_ref, dst_ref=all_ref.at[me], send_sem=send.at[k - 1], recv_sem=recv.at[k - 1],
                device_id=(px, py, pc), device_id_type=MESH))
        for cp in copies:
            cp.start()
        for cp in copies:
            cp.wait()
        total = all_ref[0]
        for k in range(1, N_DEV):
            total = total + all_ref[k]
        out_ref[...] = total

    return pl.pallas_call(
        body, name="allreduce_small",
        in_specs=[pl.BlockSpec(memory_space=pltpu.VMEM)],
        out_specs=pl.BlockSpec(memory_space=pltpu.VMEM),
        out_shape=jax.ShapeDtypeStruct((rows, width), F32),
        scratch_shapes=[pltpu.VMEM((N_DEV, rows, width), F32),
                        pltpu.SemaphoreType.DMA((N_DEV - 1,)), pltpu.SemaphoreType.DMA((N_DEV - 1,))],
    )(part)


def _row_tile(rows):
    t = min(rows, 256)
    while rows % t:
        t //= 2
    return t


def _core_and_chip():
    return jnp.stack([lax.axis_index("c"), 2 * lax.axis_index("x") + lax.axis_index("y")]).astype(jnp.int32)


def _sum_siblings(own, got, name):
    _, rows, cols = own.shape
    half = rows // 2
    t = _row_tile(half)
    nb = half // t

    def body(place_ref, own_ref, got_ref, mine_ref, out_ref):
        total = own_ref[...] + got_ref[...]
        out_ref[...] = total.astype(BF16)

        @pl.when(pl.program_id(1) == place_ref[1])
        def _():
            mine_ref[...] = total

    slab_blk = pl.BlockSpec((None, t, cols), lambda r, j, place: (j, place[0] * nb + r, 0))
    return pl.pallas_call(
        body, name=name,
        grid_spec=pltpu.PrefetchScalarGridSpec(
            num_scalar_prefetch=1, grid=(nb, N_CHIPS),
            in_specs=[slab_blk, slab_blk],
            out_specs=[pl.BlockSpec((t, cols), lambda r, j, place: (r, 0)), slab_blk]),
        out_shape=[jax.ShapeDtypeStruct((half, cols), F32), jax.ShapeDtypeStruct(own.shape, BF16)],
        compiler_params=_params(("arbitrary", "arbitrary")),
    )(_core_and_chip(), own, got)


def _sum_chips(mine, got, name):
    half, cols = mine.shape
    t = _row_tile(half)
    nb = half // t

    def body(place_ref, mine_ref, got_ref, out_ref):
        total = mine_ref[...]
        for k in range(3):
            total = total + got_ref[k].astype(F32)
        out_ref[...] = total

    return pl.pallas_call(
        body, name=name,
        grid_spec=pltpu.PrefetchScalarGridSpec(
            num_scalar_prefetch=1, grid=(nb,),
            in_specs=[pl.BlockSpec((t, cols), lambda r, place: (r, 0)),
                      pl.BlockSpec((3, t, cols), lambda r, place: (0, place[0] * nb + r, 0))],
            out_specs=pl.BlockSpec((t, cols), lambda r, place: (place[0] * nb + r, 0))),
        out_shape=jax.ShapeDtypeStruct((2 * half, cols), F32),
        compiler_params=_params(("arbitrary",)),
    )(_core_and_chip(), mine, got)


def _adamw(w, g, m, v, name, g_row0=0):
    rows, cols = w.shape
    t = _row_tile(rows)
    assert g_row0 % t == 0
    off = g_row0 // t

    def body(w_ref, g_ref, m_ref, v_ref, go_ref, d_ref, mo_ref, vo_ref):
        g_t = g_ref[...]
        m_new = ADAM_B1 * m_ref[...] + (1.0 - ADAM_B1) * g_t
        v_new = ADAM_B2 * v_ref[...] + (1.0 - ADAM_B2) * (g_t * g_t)
        m_hat = m_new / (1.0 - ADAM_B1 ** ADAM_STEP)
        v_hat = v_new / (1.0 - ADAM_B2 ** ADAM_STEP)
        go_ref[...] = g_t
        d_ref[...] = -ADAM_LR * (m_hat / (jnp.sqrt(v_hat) + ADAM_EPS) + ADAM_WD * w_ref[...])
        mo_ref[...] = m_new
        vo_ref[...] = v_new

    blk = pl.BlockSpec((t, cols), lambda r: (r, 0))
    return pl.pallas_call(
        body, name=name, grid=(rows // t,),
        in_specs=[blk, pl.BlockSpec((t, cols), lambda r: (r + off, 0)), blk, blk],
        out_specs=[blk] * 4,
        out_shape=[jax.ShapeDtypeStruct((rows, cols), F32)] * 4,
        compiler_params=_params(("arbitrary",)),
    )(w, g, m, v)


def _pack_small(d, vectors):
    rows = []
    for vec in vectors:
        flat = vec.reshape(-1)
        n_rows = -(-flat.shape[0] // d)
        rows.append(jnp.pad(flat, (0, n_rows * d - flat.shape[0])).reshape(n_rows, d))
    return jnp.concatenate(rows, axis=0)


def _unpack_small(packed, d, shapes):
    out, row = [], 0
    for shape in shapes:
        size = 1
        for s in shape:
            size *= s
        n_rows = -(-size // d)
        out.append(packed[row:row + n_rows].reshape(-1)[:size].reshape(shape))
        row += n_rows
    return out


def kernel(x, norm_pre, norm_post, pool_w_in, pool_w_group, pool_scale, pool_w_out, att_w_in, att_rel_bias, att_w_out, loss_target, m_norm_pre, m_norm_post, m_pool_w_in, m_pool_w_group, m_pool_scale, m_pool_w_out, m_att_w_in, m_att_rel_bias, m_att_w_out, v_norm_pre, v_norm_post, v_pool_w_in, v_pool_w_group, v_pool_scale, v_pool_w_out, v_att_w_in, v_att_rel_bias, v_att_w_out):
    _, s_len, d = x.shape
    gw = d // 2
    q = gw // N_CHIPS
    x2d = x.reshape(s_len, d)
    target = loss_target.reshape(s_len, d)

    def pack_g(p_group):
        return p_group.reshape(N_CHIPS * q, gw)

    wp_shard = jnp.concatenate([pool_w_in[0], pool_w_out[0]], axis=0).astype(BF16)
    wt_shard = jnp.concatenate([att_w_in[0], att_w_out[0]], axis=0).astype(BF16)
    wp, wg = _gather_weights([wp_shard, pack_g(pool_w_group).astype(BF16)])
    wg = wg.reshape(N_CHIPS, N_CHIPS, q, gw)

    x1, h0, mixed, z0, y0, wt = _l0_fwd(x2d, norm_pre[0:1], norm_post[0:1], pool_scale, wp, wg, wt_shard)
    h1, q_, k_, v_, z1 = _l1_inproj(x1, norm_pre[1:2], wt)
    bias = _bias_tiles(att_rel_bias[0])
    o, lse = _attn_fwd(q_, k_, v_, bias)

    g2, gated1, dy1, d_o, dz1, dgpost1, sq = _l1_out(o, z1, x1, target, norm_post[1:2], wt)
    dq, dk, dv, dbias = _attn_bwd(q_, k_, v_, bias, d_o, o, lse)
    dx1, du1, dgpre1 = _l1_in_bwd(dq, dk, dv, dz1, x1, g2, norm_pre[1:2], wt)
    d_rel = _rel_bias_grad(dbias)

    blk = lambda n: n
    zero = lambda n: 0
    slab_t, slab_p = d + d // N_CHIPS, d + gw
    whole_in = pl.BlockSpec((None, d, d), lambda n, kk: (n, 0, 0))
    gt = _wgrad(h1, du1, "wgrad_att_in", N_CHIPS, d, d, zero, blk, (N_CHIPS, slab_t, d), whole_in)
    gt = _wgrad(gated1, dy1, "wgrad_att_out", 1, d, d, zero, zero, (N_CHIPS, slab_t, d),
                pl.BlockSpec((N_CHIPS, d // N_CHIPS, d), lambda n, kk: (0, N_CHIPS, 0)), into=gt)

    grad_x, du0, gated0, dy0, dmm, dgpre0, dgpost0, dscale = _l0_bwd(
        dx1, x2d, y0, mixed, z0, norm_pre[0:1], norm_post[0:1], pool_scale, wp, wg)

    gp, got_t = _wgrad(h0, du0, "wgrad_pool_in", N_CHIPS, d, d, zero, blk, (N_CHIPS, slab_p, d), whole_in,
                       rider=(functools.partial(_SwapPlan, send_half=_keep_own_half), gt, gt.shape, (1, 1)))
    mine_t, send_t = _sum_siblings(gt, got_t, "sum_siblings_att")
    gp, recv_t = _wgrad(gated0, dy0, "wgrad_pool_out", N_CHIPS, gw, d, blk, zero, (N_CHIPS, slab_p, d),
                        pl.BlockSpec((None, gw, d), lambda n, kk: (n, d // gw, 0)), into=gp,
                        rider=(_ScatterPlan, send_t, (3,) + send_t.shape[1:], (3, 3)))
    gg = _wgrad(mixed, dmm, "wgrad_pool_group", N_CHIPS, gw, gw, blk, blk, (N_CHIPS, N_CHIPS, q, gw),
                pl.BlockSpec((N_CHIPS, None, q, gw), lambda n, kk: (0, n, 0, 0))).reshape(N_CHIPS, N_CHIPS * q, gw)
    got_p, got_g = _swap_halves([gp, gg], "swap_core_partials_pool", _keep_own_half, in_place=False)
    mine_p, send_p = _sum_siblings(gp, got_p, "sum_siblings_pool")
    mine_g, send_g = _sum_siblings(gg, got_g, "sum_siblings_group")
    recv_p, recv_g = _scatter_to_chips([send_p, send_g])
    red_t = _sum_chips(mine_t, recv_t, "sum_chips_att")
    red_p = _sum_chips(mine_p, recv_p, "sum_chips_pool")
    red_g = _sum_chips(mine_g, recv_g, "sum_chips_group")
    grad_t, grad_p, grad_g = _swap_halves([red_t, red_p, red_g], "swap_reduced_halves", lambda c: c, in_place=True)

    small_shapes = [norm_pre.shape, norm_post.shape, pool_scale.shape, att_rel_bias.shape]
    part = _pack_small(d, [jnp.concatenate([dgpre0, dgpre1], axis=0), jnp.concatenate([dgpost0, dgpost1], axis=0),
                           dscale, d_rel, sq])
    total = _allreduce_small(part)
    loss = (0.5 / d) * jnp.sum(total[-1])
    w_small = _pack_small(d, [norm_pre, norm_post, pool_scale, att_rel_bias, jnp.zeros((d,), F32)])
    m_small = _pack_small(d, [m_norm_pre, m_norm_post, m_pool_scale, m_att_rel_bias, jnp.zeros((d,), F32)])
    v_small = _pack_small(d, [v_norm_pre, v_norm_post, v_pool_scale, v_att_rel_bias, jnp.ones((d,), F32)])
    small_out = [_unpack_small(a, d, small_shapes) for a in _adamw(w_small, total, m_small, v_small, "adamw_small")]

    big = {}
    for name, w, m, v, grad, row0 in [("pool_w_in", pool_w_in, m_pool_w_in, v_pool_w_in, grad_p, 0),
                                      ("pool_w_out", pool_w_out, m_pool_w_out, v_pool_w_out, grad_p, d),
                                      ("att_w_in", att_w_in, m_att_w_in, v_att_w_in, grad_t, 0),
                                      ("att_w_out", att_w_out, m_att_w_out, v_att_w_out, grad_t, d)]:
        outs = _adamw(w[0], grad, m[0], v[0], "adamw_" + name, g_row0=row0)
        big[name] = [a.reshape(w.shape) for a in outs]
    outs = _adamw(pack_g(pool_w_group), grad_g, pack_g(m_pool_w_group), pack_g(v_pool_w_group), "adamw_pool_w_group")
    big["pool_w_group"] = [a.reshape(pool_w_group.shape) for a in outs]

    def leaf(kind):
        return (small_out[kind][0], small_out[kind][1], big["pool_w_in"][kind], big["pool_w_group"][kind],
                small_out[kind][2], big["pool_w_out"][kind], big["att_w_in"][kind], small_out[kind][3],
                big["att_w_out"][kind])

    return (loss, grad_x.reshape(x.shape), *leaf(0), *leaf(1), *leaf(2), *leaf(3))
```

```python
import functools

import jax
import jax.numpy as jnp
from jax import lax
from jax.experimental import pallas as pl
from jax.experimental.pallas import tpu as pltpu

F32 = jnp.float32
BF16 = jnp.bfloat16

RMS_EPS = 1e-6
CHUNK = 64
HEAD_DIM = 64
LEFT_CHUNKS = 8
PAD = LEFT_CHUNKS * CHUNK
MAX_REL = 256
POOL_WINDOWS = (2, 4, 8, 16)
HALO = 16
TS = 256
QB = 256
KB = QB + PAD
HEAD_PAIR = 2 * HEAD_DIM
NEG = -1e30
N_CHIPS = 4
N_DEV = 8

ADAM_LR = 0.001
ADAM_B1 = 0.9
ADAM_B2 = 0.999
ADAM_EPS = 1e-08
ADAM_WD = 0.01
ADAM_STEP = 10

VMEM_LIMIT = 56 * 1024 * 1024
MESH = pl.DeviceIdType.MESH
ANY = pl.BlockSpec(memory_space=pl.ANY)


def _dot(a, b):
    return jnp.dot(a, b, preferred_element_type=F32)


def _dot_nt(a, b):
    return lax.dot_general(a, b, (((1,), (1,)), ((), ())), preferred_element_type=F32)


def _dot_tn(a, b):
    return lax.dot_general(a, b, (((0,), (0,)), ((), ())), preferred_element_type=F32)


def _params(sem, limit=VMEM_LIMIT):
    return pltpu.CompilerParams(dimension_semantics=sem, vmem_limit_bytes=limit)


def _rms(x):
    return lax.rsqrt(jnp.mean(x * x, axis=-1, keepdims=True) + RMS_EPS)


def _rms_bwd(dyh, xh, r):
    return r * (dyh - xh * jnp.mean(dyh * xh, axis=-1, keepdims=True))


def _silu_parts(z):
    sig = jax.nn.sigmoid(z)
    return z * sig, sig * (1.0 + z * (1.0 - sig))


def _inv_count(tile, rows, w):
    t = tile * rows + lax.broadcasted_iota(jnp.int32, (rows, 1), 0)
    return 1.0 / jnp.minimum(t + 1, w).astype(F32)


def _start_weight_copies(copies):
    for c in copies:
        c.start()
    for c in copies:
        c.wait()


def _l0_fwd(x, g_pre, g_post, scale, wa, wg, next_shard):
    s_len, d = x.shape
    pw, gw = 2 * d, d // 2
    q = gw // N_CHIPS
    nt = s_len // TS
    assert nt >= 3

    def body(x_ref, gpre_ref, gpost_ref, sc_ref, wa_ref, wg_ref, shard_ref,
             x1_ref, h_ref, mixed_ref, z_ref, y_ref, next_ref,
             win_s, wout_s, wg_s, halo_s, sems, *gather_sems):
        i = pl.program_id(0)

        @pl.when(i == 0)
        def _():
            _GatherPlan([shard_ref], [next_ref], *gather_sems).send()
            copies = [pltpu.make_async_copy(wa_ref.at[:, pl.ds(0, d), :], win_s, sems.at[0]),
                      pltpu.make_async_copy(wa_ref.at[:, pl.ds(d, gw), :], wout_s, sems.at[1])]
            copies += [pltpu.make_async_copy(wg_ref.at[j], wg_s.at[:, pl.ds(j * q, q), :], sems.at[2 + j])
                       for j in range(N_CHIPS)]
            _start_weight_copies(copies)
            halo_s[...] = jnp.zeros_like(halo_s)

        @pl.when(i == nt // 2)
        def _():
            _GatherPlan([shard_ref], [next_ref], *gather_sems).pass_on()

        @pl.when(i == nt - 1)
        def _():
            _GatherPlan([shard_ref], [next_ref], *gather_sems).finish()

        xt = x_ref[...]
        h = ((xt * _rms(xt)) * gpre_ref[...]).astype(BF16)
        h_ref[...] = h
        a_blocks = [_dot(h, win_s[0]), _dot(h, win_s[1])]
        y = jnp.zeros((TS, d), F32)
        for g, w in enumerate(POOL_WINDOWS):
            a_g = a_blocks[g // 2][:, (g % 2) * gw:(g % 2 + 1) * gw]
            ext = jnp.concatenate([halo_s[g], a_g], axis=0)
            shift = 1
            while shift < w:
                ext = ext + pltpu.roll(ext, shift, 0)
                shift *= 2
            mixed = (ext[HALO:] * _inv_count(i, TS, w) - a_g).astype(BF16)
            halo_s[g] = a_g[TS - HALO:]
            mixed_ref[:, g * gw:(g + 1) * gw] = mixed
            z_g = _dot(h, win_s[2 + g // 2, :, (g % 2) * gw:(g % 2 + 1) * gw])
            z_ref[:, g * gw:(g + 1) * gw] = z_g
            ms = _dot(mixed, wg_s[g]) * sc_ref[:, g * gw:(g + 1) * gw]
            gated = (ms * _silu_parts(z_g)[0]).astype(BF16)
            y = y + _dot(gated, wout_s[g])
        y_ref[...] = y
        x1_ref[...] = xt + (y * _rms(y)) * gpost_ref[...]

    tile = lambda wdt: pl.BlockSpec((TS, wdt), lambda i: (i, 0))
    row = lambda wdt: pl.BlockSpec((1, wdt), lambda i: (0, 0))
    return pl.pallas_call(
        body, name="l0_fwd", grid=(nt,),
        in_specs=[tile(d), row(d), row(d), row(pw), ANY, ANY, ANY],
        out_specs=[tile(d), tile(d), tile(pw), tile(pw), tile(d), ANY],
        out_shape=[jax.ShapeDtypeStruct((s_len, d), F32), jax.ShapeDtypeStruct((s_len, d), BF16),
                   jax.ShapeDtypeStruct((s_len, pw), BF16), jax.ShapeDtypeStruct((s_len, pw), F32),
                   jax.ShapeDtypeStruct((s_len, d), F32),
                   jax.ShapeDtypeStruct((N_CHIPS,) + next_shard.shape, next_shard.dtype)],
        scratch_shapes=[pltpu.VMEM((N_CHIPS, d, d), BF16), pltpu.VMEM((N_CHIPS, gw, d), BF16),
                        pltpu.VMEM((N_CHIPS, gw, gw), BF16), pltpu.VMEM((N_CHIPS, HALO, gw), F32),
                        pltpu.SemaphoreType.DMA((2 + N_CHIPS,))] + _gather_sems(1),
        compiler_params=_params(("arbitrary",)),
    )(x, g_pre, g_post, scale, wa, wg, next_shard)


def _l0_bwd(dx1, x, y, mixed, z, g_pre, g_post, scale, wa, wg):
    s_len, d = x.shape
    pw, gw = 2 * d, d // 2
    q = gw // N_CHIPS
    nt = s_len // TS

    def body(dx1_ref, x_ref, y_ref, mixed_ref, z_ref, gpre_ref, gpost_ref, sc_ref, wa_ref, wg_ref,
             gx_ref, du_ref, gated_ref, dy_ref, dmm_ref, dgpre_ref, dgpost_ref, dsc_ref,
             win_s, wout_s, wg_s, halo_s, sems):
        i = pl.program_id(0)
        tile = nt - 1 - i

        @pl.when(i == 0)
        def _():
            copies = [pltpu.make_async_copy(wa_ref.at[:, pl.ds(0, d), :], win_s, sems.at[0]),
                      pltpu.make_async_copy(wa_ref.at[:, pl.ds(d, gw), :], wout_s, sems.at[1])]
            copies += [pltpu.make_async_copy(wg_ref.at[j], wg_s.at[:, pl.ds(j * q, q), :], sems.at[2 + j])
                       for j in range(N_CHIPS)]
            _start_weight_copies(copies)
            halo_s[...] = jnp.zeros_like(halo_s)
            dgpre_ref[...] = jnp.zeros_like(dgpre_ref)
            dgpost_ref[...] = jnp.zeros_like(dgpost_ref)
            dsc_ref[...] = jnp.zeros_like(dsc_ref)

        g_in = dx1_ref[...]
        yt = y_ref[...]
        r_y = _rms(yt)
        yh = yt * r_y
        dgpost_ref[...] += jnp.sum(g_in * yh, axis=0, keepdims=True)
        dy = _rms_bwd(g_in * gpost_ref[...], yh, r_y).astype(BF16)
        dy_ref[...] = dy
        dh = jnp.zeros((TS, d), F32)
        for g, w in enumerate(POOL_WINDOWS):
            cols = slice(g * gw, (g + 1) * gw)
            dgated = _dot_nt(dy, wout_s[g])
            mm = _dot(mixed_ref[:, cols], wg_s[g])
            sc = sc_ref[:, cols]
            ms = mm * sc
            z_g = z_ref[:, cols]
            sz, dsz = _silu_parts(z_g)
            gated_ref[:, cols] = (ms * sz).astype(BF16)
            dms = dgated * sz
            dz = (dgated * ms * dsz).astype(BF16)
            dsc_ref[:, cols] += jnp.sum(dms * mm, axis=0, keepdims=True)
            dmm = (dms * sc).astype(BF16)
            dmm_ref[:, cols] = dmm
            dmixed = _dot_nt(dmm, wg_s[g])
            e = dmixed * _inv_count(tile, TS, w)
            ext = jnp.concatenate([e, halo_s[g]], axis=0)
            shift = 1
            while shift < w:
                ext = ext + pltpu.roll(ext, TS + HALO - shift, 0)
                shift *= 2
            da = (ext[:TS] - dmixed).astype(BF16)
            halo_s[g] = e[:HALO]
            du_ref[:, cols] = da
            du_ref[:, pw + g * gw:pw + (g + 1) * gw] = dz
            wa_blk = win_s[g // 2, :, (g % 2) * gw:(g % 2 + 1) * gw]
            wz_blk = win_s[2 + g // 2, :, (g % 2) * gw:(g % 2 + 1) * gw]
            dh = dh + _dot_nt(da, wa_blk) + _dot_nt(dz, wz_blk)
        xt = x_ref[...]
        r_x = _rms(xt)
        xh = xt * r_x
        dgpre_ref[...] += jnp.sum(dh * xh, axis=0, keepdims=True)
        gx_ref[...] = g_in + _rms_bwd(dh * gpre_ref[...], xh, r_x)

    tile_spec = lambda wdt: pl.BlockSpec((TS, wdt), lambda i: (nt - 1 - i, 0))
    row = lambda wdt: pl.BlockSpec((1, wdt), lambda i: (0, 0))
    return pl.pallas_call(
        body, name="l0_bwd", grid=(nt,),
        in_specs=[tile_spec(d), tile_spec(d), tile_spec(d), tile_spec(pw), tile_spec(pw),
                  row(d), row(d), row(pw), ANY, ANY],
        out_specs=[tile_spec(d), tile_spec(2 * pw), tile_spec(pw), tile_spec(d), tile_spec(pw),
                   row(d), row(d), row(pw)],
        out_shape=[jax.ShapeDtypeStruct((s_len, d), F32), jax.ShapeDtypeStruct((s_len, 2 * pw), BF16),
                   jax.ShapeDtypeStruct((s_len, pw), BF16), jax.ShapeDtypeStruct((s_len, d), BF16),
                   jax.ShapeDtypeStruct((s_len, pw), BF16),
                   jax.ShapeDtypeStruct((1, d), F32), jax.ShapeDtypeStruct((1, d), F32),
                   jax.ShapeDtypeStruct((1, pw), F32)],
        scratch_shapes=[pltpu.VMEM((N_CHIPS, d, d), BF16), pltpu.VMEM((N_CHIPS, gw, d), BF16),
                        pltpu.VMEM((N_CHIPS, gw, gw), BF16), pltpu.VMEM((N_CHIPS, HALO, gw), F32),
                        pltpu.SemaphoreType.DMA((2 + N_CHIPS,))],
        compiler_params=_params(("arbitrary",)),
    )(dx1, x, y, mixed, z, g_pre, g_post, scale, wa, wg)


def _l1_inproj(x1, g_pre, wa):
    s_len, d = x1.shape
    nt = s_len // TS
    npad = PAD // TS

    def body(x_ref, gpre_ref, wa_ref, h_ref, q_ref, k_ref, v_ref, z_ref, win_s, sem):
        i = pl.program_id(0)

        @pl.when(i == 0)
        def _():
            _start_weight_copies([pltpu.make_async_copy(wa_ref.at[:, pl.ds(0, d), :], win_s, sem.at[0])])

        @pl.when(i < npad)
        def _():
            k_ref[...] = jnp.zeros_like(k_ref)
            v_ref[...] = jnp.zeros_like(v_ref)

        @pl.when(i >= npad)
        def _():
            xt = x_ref[...]
            h = ((xt * _rms(xt)) * gpre_ref[...]).astype(BF16)
            h_ref[...] = h
            q_ref[...] = _dot(h, win_s[0]).astype(BF16)
            k_ref[...] = _dot(h, win_s[1]).astype(BF16)
            v_ref[...] = _dot(h, win_s[2]).astype(BF16)
            z_ref[...] = _dot(h, win_s[3])

    tile = pl.BlockSpec((TS, d), lambda i: (jnp.maximum(i - npad, 0), 0))
    padded = pl.BlockSpec((TS, d), lambda i: (i, 0))
    return pl.pallas_call(
        body, name="l1_inproj", grid=(nt + npad,),
        in_specs=[tile, pl.BlockSpec((1, d), lambda i: (0, 0)), ANY],
        out_specs=[tile, tile, padded, padded, tile],
        out_shape=[jax.ShapeDtypeStruct((s_len, d), BF16), jax.ShapeDtypeStruct((s_len, d), BF16),
                   jax.ShapeDtypeStruct((PAD + s_len, d), BF16), jax.ShapeDtypeStruct((PAD + s_len, d), BF16),
                   jax.ShapeDtypeStruct((s_len, d), F32)],
        scratch_shapes=[pltpu.VMEM((N_CHIPS, d, d), BF16), pltpu.SemaphoreType.DMA((1,))],
        compiler_params=_params(("arbitrary",)),
    )(x1, g_pre, wa)


SKEW = QB + KB


def _bias_tiles(rel_bias):
    nh = rel_bias.shape[0]
    assert QB == MAX_REL
    by_column = jnp.concatenate([jnp.broadcast_to(rel_bias[:, 2 * MAX_REL:], (nh, PAD + 1)),
                                 jnp.flip(rel_bias[:, 1:2 * MAX_REL], axis=1)], axis=1).astype(F32)

    def body(col_ref, out_ref):
        rows = jnp.broadcast_to(col_ref[pl.ds(pl.program_id(0), 1), :], (QB, SKEW))
        tile = pltpu.roll(rows, SKEW - QB, 1, stride=1, stride_axis=0)[:, :KB]
        i = lax.broadcasted_iota(jnp.int32, (QB, KB), 0)
        j = lax.broadcasted_iota(jnp.int32, (QB, KB), 1)
        first = (i // CHUNK) * CHUNK
        out_ref[...] = jnp.where((j >= first) & (j < first + PAD + CHUNK), tile, NEG)

    return pl.pallas_call(
        body, name="bias_tiles", grid=(nh,),
        in_specs=[pl.BlockSpec((nh, SKEW), lambda h: (0, 0))],
        out_specs=pl.BlockSpec((None, QB, KB), lambda h: (h, 0, 0)),
        out_shape=jax.ShapeDtypeStruct((nh, QB, KB), F32),
        compiler_params=_params(("arbitrary",)),
    )(by_column)


ROWS = 16


BLOCKS_PER_STEP = 4


def _row_blocks():
    return [pl.ds(r * ROWS, ROWS) for r in range(QB // ROWS)]


def _attn_fwd(q, k, v, bias):
    s_len, d = q.shape
    nhp = d // HEAD_PAIR
    per_step = min(BLOCKS_PER_STEP, s_len // QB)
    nq = s_len // (QB * per_step)

    def body(q_ref, k_ref, v_ref, b_ref, o_ref, lse_ref, s_s, p_s, l_s):
        head_of_lane = lax.broadcasted_iota(jnp.int32, (1, HEAD_PAIR), 1) // HEAD_DIM

        def one_block(u, carry):
            here = pl.ds(pl.multiple_of(u * QB, QB), QB)
            start = pl.multiple_of((pl.program_id(1) * per_step + u) * QB, QB)
            qt = q_ref[here, :] * (HEAD_DIM ** -0.5)
            kb = k_ref[pl.ds(start, KB), :]
            vb = v_ref[pl.ds(start, KB), :]
            colvalid = (start + lax.broadcasted_iota(jnp.int32, (1, KB), 1)) >= PAD
            outs, lses = [], []
            for hh in range(2):
                s_s[...] = _dot_nt(jnp.where(head_of_lane == hh, qt, jnp.zeros_like(qt)), kb)
                for rows in _row_blocks():
                    s = jnp.where(colvalid, s_s[rows, :] + b_ref[hh, rows, :], NEG)
                    m = jnp.max(s, axis=-1, keepdims=True)
                    e = jnp.exp(s - m)
                    l = jnp.sum(e, axis=-1, keepdims=True)
                    p_s[rows, :] = (e * (1.0 / l)).astype(BF16)
                    l_s[rows, :] = m + jnp.log(l)
                outs.append(_dot(p_s[...], vb))
                lses.append(l_s[...])
            o_ref[here, :] = jnp.where(head_of_lane == 0, outs[0], outs[1])
            lse_ref[here, :] = jnp.where(head_of_lane == 0, lses[0], lses[1])
            return carry

        lax.fori_loop(0, per_step, one_block, 0)

    blk = pl.BlockSpec((per_step * QB, HEAD_PAIR), lambda hp, b: (b, hp))
    whole = pl.BlockSpec((PAD + s_len, HEAD_PAIR), lambda hp, b: (0, hp))
    return pl.pallas_call(
        body, name="attn_fwd", grid=(nhp, nq),
        in_specs=[blk, whole, whole, pl.BlockSpec((2, QB, KB), lambda hp, b: (hp, 0, 0))],
        out_specs=[blk, blk],
        out_shape=[jax.ShapeDtypeStruct((s_len, d), F32), jax.ShapeDtypeStruct((s_len, d), F32)],
        scratch_shapes=[pltpu.VMEM((QB, KB), F32), pltpu.VMEM((QB, KB), BF16), pltpu.VMEM((QB, 1), F32)],
        compiler_params=_params(("arbitrary", "arbitrary")),
    )(q, k, v, bias)


def _attn_bwd(q, k, v, bias, d_o, o, lse):
    s_len, d = q.shape
    nhp = d // HEAD_PAIR
    per_step = min(BLOCKS_PER_STEP, s_len // QB)
    nq = s_len // (QB * per_step)
    qk_scale = HEAD_DIM ** -0.5

    def body(q_ref, k_ref, v_ref, b_ref, do_ref, o_ref, lse_ref, dq_ref, dk_ref, dv_ref, db_ref,
             s_s, dp_s, p_s, ds_s, lse_s, delta_s):
        @pl.when(pl.program_id(1) == 0)
        def _():
            dk_ref[...] = jnp.zeros_like(dk_ref)
            dv_ref[...] = jnp.zeros_like(dv_ref)
            db_ref[...] = jnp.zeros_like(db_ref)

        head_of_lane = lax.broadcasted_iota(jnp.int32, (1, HEAD_PAIR), 1) // HEAD_DIM

        def one_block(u, carry):
            here = pl.ds(pl.multiple_of(u * QB, QB), QB)
            start = pl.multiple_of((pl.program_id(1) * per_step + u) * QB, QB)
            qt = q_ref[here, :] * qk_scale
            dot_ = do_ref[here, :]
            kb = k_ref[pl.ds(start, KB), :]
            vb = v_ref[pl.ds(start, KB), :]
            colvalid = (start + lax.broadcasted_iota(jnp.int32, (1, KB), 1)) >= PAD
            do_o = dot_.astype(F32) * o_ref[here, :]
            dq_heads = []
            dk_band = jnp.zeros((KB, HEAD_PAIR), F32)
            dv_band = jnp.zeros((KB, HEAD_PAIR), F32)
            for hh in range(2):
                mine = head_of_lane == hh
                q_m = jnp.where(mine, qt, jnp.zeros_like(qt))
                do_m = jnp.where(mine, dot_, jnp.zeros_like(dot_))
                delta_s[...] = jnp.sum(jnp.where(mine, do_o, 0.0), axis=-1, keepdims=True)
                lse_s[...] = lse_ref[here, hh * HEAD_DIM:hh * HEAD_DIM + 1]
                s_s[...] = _dot_nt(q_m, kb)
                dp_s[...] = _dot_nt(do_m, vb)
                for rows in _row_blocks():
                    t = jnp.where(colvalid, s_s[rows, :] + b_ref[hh, rows, :] - lse_s[rows, :], NEG)
                    p = jnp.exp(t)
                    ds = p * (dp_s[rows, :] - delta_s[rows, :])
                    db_ref[hh, rows, :] += ds
                    p_s[rows, :] = p.astype(BF16)
                    ds_s[rows, :] = ds.astype(BF16)
                dv_band = dv_band + _dot_tn(p_s[...], do_m)
                dq_heads.append(_dot(ds_s[...], kb) * qk_scale)
                dk_band = dk_band + _dot_tn(ds_s[...], q_m)
            dq_ref[here, :] = jnp.where(head_of_lane == 0, dq_heads[0], dq_heads[1]).astype(BF16)
            dk_ref[pl.ds(start, KB), :] += dk_band
            dv_ref[pl.ds(start, KB), :] += dv_band
            return carry

        lax.fori_loop(0, per_step, one_block, 0)

    blk = pl.BlockSpec((per_step * QB, HEAD_PAIR), lambda hp, b: (b, hp))
    whole = pl.BlockSpec((PAD + s_len, HEAD_PAIR), lambda hp, b: (0, hp))
    btile = pl.BlockSpec((2, QB, KB), lambda hp, b: (hp, 0, 0))
    return pl.pallas_call(
        body, name="attn_bwd", grid=(nhp, nq),
        in_specs=[blk, whole, whole, btile, blk, blk, blk],
        out_specs=[blk, whole, whole, btile],
        out_shape=[jax.ShapeDtypeStruct((s_len, d), BF16),
                   jax.ShapeDtypeStruct((PAD + s_len, d), F32), jax.ShapeDtypeStruct((PAD + s_len, d), F32),
                   jax.ShapeDtypeStruct(bias.shape, F32)],
        scratch_shapes=[pltpu.VMEM((QB, KB), F32), pltpu.VMEM((QB, KB), F32),
                        pltpu.VMEM((QB, KB), BF16), pltpu.VMEM((QB, KB), BF16),
                        pltpu.VMEM((QB, 1), F32), pltpu.VMEM((QB, 1), F32)],
        compiler_params=_params(("arbitrary", "arbitrary")),
    )(q, k, v, bias, d_o, o, lse)


def _rel_bias_grad(db):
    nh = db.shape[0]
    assert QB == MAX_REL

    def body(db_ref, out_ref):
        i0 = lax.broadcasted_iota(jnp.int32, (QB, QB), 0)
        i1 = lax.broadcasted_iota(jnp.int32, (QB, QB), 1)
        exchange = jnp.where(i0 + i1 == QB - 1, 1.0, 0.0).astype(BF16)
        rest = db_ref[...]
        flipped = jnp.zeros((QB, KB), F32)
        for _ in range(3):
            piece = rest.astype(BF16)
            flipped = flipped + _dot(exchange, piece)
            rest = rest - piece.astype(F32)
        m = jnp.concatenate([flipped, jnp.zeros((QB, SKEW - KB), F32)], axis=1)
        diag = jnp.sum(pltpu.roll(m, 1, 1, stride=1, stride_axis=0), axis=0, keepdims=True)
        c = lax.broadcasted_iota(jnp.int32, (1, SKEW), 1)
        clipped = jnp.sum(jnp.where(c <= PAD, diag, 0.0), axis=1, keepdims=True)
        out_ref[...] = jnp.where(c == 0, clipped, diag)

    diag = pl.pallas_call(
        body, name="bias_diagonals", grid=(nh,),
        in_specs=[pl.BlockSpec((None, QB, KB), lambda h: (h, 0, 0))],
        out_specs=pl.BlockSpec((None, 1, SKEW), lambda h: (h, 0, 0)),
        out_shape=jax.ShapeDtypeStruct((nh, 1, SKEW), F32),
        compiler_params=_params(("arbitrary",)),
    )(db)[:, 0]
    return jnp.concatenate([jnp.zeros((nh, 1), F32), jnp.flip(diag[:, PAD + 1:], axis=1), diag[:, :1]], axis=1)


def _l1_out(o, z, x1, target, g_post, wa):
    s_len, d = o.shape
    nt = s_len // TS
    slab = d // N_CHIPS

    def body(o_ref, z_ref, x1_ref, t_ref, gpost_ref, wa_ref,
             g2_ref, gated_ref, dy_ref, do_ref, dz_ref, dgpost_ref, sq_ref, wout_s, sems):
        i = pl.program_id(0)

        @pl.when(i == 0)
        def _():
            _start_weight_copies([
                pltpu.make_async_copy(wa_ref.at[j, pl.ds(d, slab), :],
                                      wout_s.at[pl.ds(j * slab, slab), :], sems.at[j])
                for j in range(N_CHIPS)])
            dgpost_ref[...] = jnp.zeros_like(dgpost_ref)
            sq_ref[...] = jnp.zeros_like(sq_ref)

        ot = o_ref[...]
        sz, dsz = _silu_parts(z_ref[...])
        gated = (ot * sz).astype(BF16)
        gated_ref[...] = gated
        y = _dot(gated, wout_s[...])
        r_y = _rms(y)
        yh = y * r_y
        err = x1_ref[...] + yh * gpost_ref[...] - t_ref[...]
        sq_ref[...] += jnp.sum(err * err, axis=0, keepdims=True)
        g2 = err * (1.0 / d)
        g2_ref[...] = g2
        dgpost_ref[...] += jnp.sum(g2 * yh, axis=0, keepdims=True)
        dy = _rms_bwd(g2 * gpost_ref[...], yh, r_y).astype(BF16)
        dy_ref[...] = dy
        dgated = _dot_nt(dy, wout_s[...])
        do_ref[...] = (dgated * sz).astype(BF16)
        dz_ref[...] = (dgated * ot * dsz).astype(BF16)

    tile = pl.BlockSpec((TS, d), lambda i: (i, 0))
    row = pl.BlockSpec((1, d), lambda i: (0, 0))
    return pl.pallas_call(
        body, name="l1_out", grid=(nt,),
        in_specs=[tile, tile, tile, tile, row, ANY],
        out_specs=[tile, tile, tile, tile, tile, row, row],
        out_shape=[jax.ShapeDtypeStruct((s_len, d), F32)] + [jax.ShapeDtypeStruct((s_len, d), BF16)] * 4
                  + [jax.ShapeDtypeStruct((1, d), F32)] * 2,
        scratch_shapes=[pltpu.VMEM((d, d), BF16), pltpu.SemaphoreType.DMA((N_CHIPS,))],
        compiler_params=_params(("arbitrary",)),
    )(o, z, x1, target, g_post, wa)


def _l1_in_bwd(dq, dk, dv, dz, x1, g2, g_pre, wa):
    s_len, d = x1.shape
    nt = s_len // TS
    npad = PAD // TS

    def body(dq_ref, dk_ref, dv_ref, dz_ref, x_ref, g2_ref, gpre_ref, wa_ref,
             dx_ref, du_ref, dgpre_ref, win_s, sem):
        i = pl.program_id(0)

        @pl.when(i == 0)
        def _():
            _start_weight_copies([pltpu.make_async_copy(wa_ref.at[:, pl.ds(0, d), :], win_s, sem.at[0])])
            dgpre_ref[...] = jnp.zeros_like(dgpre_ref)

        parts = [dq_ref[...], dk_ref[...].astype(BF16), dv_ref[...].astype(BF16), dz_ref[...]]
        dh = jnp.zeros((TS, d), F32)
        for j, part in enumerate(parts):
            du_ref[:, j * d:(j + 1) * d] = part
            dh = dh + _dot_nt(part, win_s[j])
        xt = x_ref[...]
        r_x = _rms(xt)
        xh = xt * r_x
        dgpre_ref[...] += jnp.sum(dh * xh, axis=0, keepdims=True)
        dx_ref[...] = g2_ref[...] + _rms_bwd(dh * gpre_ref[...], xh, r_x)

    tile = pl.BlockSpec((TS, d), lambda i: (i, 0))
    padded = pl.BlockSpec((TS, d), lambda i: (i + npad, 0))
    row = pl.BlockSpec((1, d), lambda i: (0, 0))
    return pl.pallas_call(
        body, name="l1_in_bwd", grid=(nt,),
        in_specs=[tile, padded, padded, tile, tile, tile, row, ANY],
        out_specs=[tile, pl.BlockSpec((TS, 4 * d), lambda i: (i, 0)), row],
        out_shape=[jax.ShapeDtypeStruct((s_len, d), F32), jax.ShapeDtypeStruct((s_len, 4 * d), BF16),
                   jax.ShapeDtypeStruct((1, d), F32)],
        scratch_shapes=[pltpu.VMEM((N_CHIPS, d, d), BF16), pltpu.SemaphoreType.DMA((1,))],
        compiler_params=_params(("arbitrary",)),
    )(dq, dk, dv, dz, x1, g2, g_pre, wa)


def _wgrad(a, b, name, n_out, a_width, b_width, a_block, b_block, out_shape, out_spec, into=None, rider=None):
    s_len = a.shape[0]
    tk = min(1024, s_len)
    nk = s_len // tk
    n_in = 2 + (into is not None) + (rider is not None)

    def body(*refs):
        a_ref, b_ref = refs[:2]
        out_ref = refs[n_in]
        if rider is not None:
            plan = lambda: rider[0]([refs[n_in - 1]], [refs[n_in + 1]], *refs[n_in + 2:])
            first = (pl.program_id(0) == 0) & (pl.program_id(1) == 0)
            last = (pl.program_id(0) == n_out - 1) & (pl.program_id(1) == nk - 1)

            @pl.when(first)
            def _():
                plan().send()

        @pl.when(pl.program_id(1) == 0)
        def _():
            out_ref[...] = jnp.zeros_like(out_ref)

        out_ref[...] += _dot_tn(a_ref[...], b_ref[...]).reshape(out_ref.shape)

        if rider is not None:
            @pl.when(last)
            def _():
                plan().finish()

    operands = [a, b] + ([into] if into is not None else []) + ([rider[1]] if rider is not None else [])
    results = pl.pallas_call(
        body, name=name, grid=(n_out, nk),
        in_specs=[pl.BlockSpec((tk, a_width), lambda n, kk: (kk, a_block(n))),
                  pl.BlockSpec((tk, b_width), lambda n, kk: (kk, b_block(n)))] + [ANY] * (n_in - 2),
        out_specs=[out_spec] + ([ANY] if rider is not None else []),
        out_shape=[jax.ShapeDtypeStruct(out_shape, F32)]
                  + ([jax.ShapeDtypeStruct(rider[2], rider[1].dtype)] if rider is not None else []),
        scratch_shapes=[pltpu.SemaphoreType.DMA((n,)) for n in rider[3]] if rider is not None else [],
        input_output_aliases={2: 0} if into is not None else {},
        compiler_params=_params(("arbitrary", "arbitrary")),
    )(*operands)
    return results if rider is not None else results[0]


def _place():
    x, y, c = lax.axis_index("x"), lax.axis_index("y"), lax.axis_index("c")
    others = [(1 - x, y), (x, 1 - y), (1 - x, 1 - y)]
    return x, y, c, others


class _GatherPlan:
    def __init__(self, src, dst, send, recv, fwd_send, fwd_recv, local):
        x, y, c, others = _place()
        me = 2 * x + y
        sibling = (x, y, 1 - c)

        def half(ref, cc):
            rows = ref.shape[0] // 2
            return ref.at[pl.ds(cc * rows, rows), :]

        self.mine = [pltpu.make_async_copy(src[a], dst[a].at[me], local.at[a]) for a in range(len(src))]
        self.first, self.passed, self.arrive, self.arrive_fwd = [], [], [], []
        for a in range(len(src)):
            for k, (ox, oy) in enumerate(others):
                sem = a * 3 + k
                self.first.append(pltpu.make_async_remote_copy(
                    src_ref=half(src[a], c), dst_ref=half(dst[a].at[me], c),
                    send_sem=send.at[sem], recv_sem=recv.at[sem], device_id=(ox, oy, c), device_id_type=MESH))
                theirs = half(dst[a].at[2 * ox + oy], c)
                self.arrive.append(pltpu.make_async_remote_copy(
                    src_ref=theirs, dst_ref=theirs, send_sem=send.at[sem], recv_sem=recv.at[sem],
                    device_id=(ox, oy, c), device_id_type=MESH))
                self.passed.append(pltpu.make_async_remote_copy(
                    src_ref=theirs, dst_ref=theirs, send_sem=fwd_send.at[sem], recv_sem=fwd_recv.at[sem],
                    device_id=sibling, device_id_type=MESH))
                other_half = half(dst[a].at[2 * ox + oy], 1 - c)
                self.arrive_fwd.append(pltpu.make_async_remote_copy(
                    src_ref=other_half, dst_ref=other_half, send_sem=fwd_send.at[sem], recv_sem=fwd_recv.at[sem],
                    device_id=sibling, device_id_type=MESH))

    def send(self):
        for cp in self.mine + self.first:
            cp.start()

    def pass_on(self):
        for got, onward in zip(self.arrive, self.passed):
            got.wait_recv()
            onward.start()

    def finish(self):
        for got in self.arrive_fwd:
            got.wait_recv()
        for cp in self.first + self.passed:
            cp.wait_send()
        for cp in self.mine:
            cp.wait()


def _gather_sems(n):
    return [pltpu.SemaphoreType.DMA((3 * n,))] * 4 + [pltpu.SemaphoreType.DMA((n,))]


def _gather_weights(shards):
    n = len(shards)

    def body(*refs):
        plan = _GatherPlan(refs[:n], refs[n:2 * n], *refs[2 * n:])
        plan.send()
        plan.pass_on()
        plan.finish()

    return pl.pallas_call(
        body, name="gather_weights",
        in_specs=[ANY] * n, out_specs=[ANY] * n,
        out_shape=[jax.ShapeDtypeStruct((N_CHIPS,) + s.shape, s.dtype) for s in shards],
        scratch_shapes=_gather_sems(n),
    )(*shards)


class _SwapPlan:
    def __init__(self, src, dst, send, recv, send_half):
        x, y, c, _ = _place()
        sibling = (x, y, 1 - c)
        self.out_copies, self.in_copies = [], []
        for a in range(len(src)):
            rows = src[a].shape[-2] // 2
            lead = (slice(None),) * (len(src[a].shape) - 2)
            going = lead + (pl.ds(send_half(c) * rows, rows), slice(None))
            coming = lead + (pl.ds((1 - send_half(c)) * rows, rows), slice(None))
            self.out_copies.append(pltpu.make_async_remote_copy(
                src_ref=src[a].at[going], dst_ref=dst[a].at[going], send_sem=send.at[a], recv_sem=recv.at[a],
                device_id=sibling, device_id_type=MESH))
            self.in_copies.append(pltpu.make_async_remote_copy(
                src_ref=src[a].at[coming], dst_ref=dst[a].at[coming], send_sem=send.at[a], recv_sem=recv.at[a],
                device_id=sibling, device_id_type=MESH))

    def send(self):
        for cp in self.out_copies:
            cp.start()

    def finish(self):
        for cp in self.in_copies:
            cp.wait_recv()
        for cp in self.out_copies:
            cp.wait_send()


def _keep_own_half(c):
    return 1 - c


def _swap_halves(arrays, name, send_half, in_place):
    n = len(arrays)

    def body(*refs):
        src, dst = refs[:n], refs[n:2 * n]
        plan = _SwapPlan(dst if in_place else src, dst, *refs[2 * n:], send_half=send_half)
        plan.send()
        plan.finish()

    return pl.pallas_call(
        body, name=name,
        in_specs=[ANY] * n, out_specs=[ANY] * n,
        out_shape=[jax.ShapeDtypeStruct(a.shape, a.dtype) for a in arrays],
        scratch_shapes=[pltpu.SemaphoreType.DMA((n,)), pltpu.SemaphoreType.DMA((n,))],
        input_output_aliases={a: a for a in range(n)} if in_place else {},
    )(*arrays)


class _ScatterPlan:
    def __init__(self, src, dst, send, recv):
        x, y, c, others = _place()
        self.out_copies, self.in_copies = [], []
        for a in range(len(src)):
            rows = src[a].shape[1] // 2
            mine = pl.ds(c * rows, rows)
            for k, (ox, oy) in enumerate(others):
                sem = a * 3 + k
                self.out_copies.append(pltpu.make_async_remote_copy(
                    src_ref=src[a].at[2 * ox + oy, mine, :], dst_ref=dst[a].at[k, mine, :],
                    send_sem=send.at[sem], recv_sem=recv.at[sem], device_id=(ox, oy, c), device_id_type=MESH))
                self.in_copies.append(pltpu.make_async_remote_copy(
                    src_ref=dst[a].at[k, mine, :], dst_ref=dst[a].at[k, mine, :],
                    send_sem=send.at[sem], recv_sem=recv.at[sem], device_id=(ox, oy, c), device_id_type=MESH))

    def send(self):
        for cp in self.out_copies:
            cp.start()

    def finish(self):
        for cp in self.in_copies:
            cp.wait_recv()
        for cp in self.out_copies:
            cp.wait_send()


def _scatter_to_chips(arrays):
    n = len(arrays)

    def body(*refs):
        plan = _ScatterPlan(refs[:n], refs[n:2 * n], *refs[2 * n:])
        plan.send()
        plan.finish()

    return pl.pallas_call(
        body, name="scatter_to_chips",
        in_specs=[ANY] * n, out_specs=[ANY] * n,
        out_shape=[jax.ShapeDtypeStruct((3,) + a.shape[1:], a.dtype) for a in arrays],
        scratch_shapes=[pltpu.SemaphoreType.DMA((3 * n,)), pltpu.SemaphoreType.DMA((3 * n,))],
    )(*arrays)


def _allreduce_small(part):
    rows, width = part.shape

    def body(p_ref, out_ref, all_ref, send, recv):
        x, y, c, _ = _place()
        me = 4 * x + 2 * y + c
        all_ref[me] = p_ref[...]
        copies = []
        for k in range(1, N_DEV):
            px, py, pc = x ^ (k >> 2), y ^ ((k >> 1) & 1), c ^ (k & 1)
            copies.append(pltpu.make_async_remote_copy(
                src_ref=p_ref, dst_ref=all_ref.at[me], send_sem=send.at[k - 1], recv_sem=recv.at[k - 1],
                device_id=(px, py, pc), device_id_type=MESH))
        for cp in copies:
            cp.start()
        for cp in copies:
            cp.wait()
        total = all_ref[0]
        for k in range(1, N_DEV):
            total = total + all_ref[k]
        out_ref[...] = total

    return pl.pallas_call(
        body, name="allreduce_small",
        in_specs=[pl.BlockSpec(memory_space=pltpu.VMEM)],
        out_specs=pl.BlockSpec(memory_space=pltpu.VMEM),
        out_shape=jax.ShapeDtypeStruct((rows, width), F32),
        scratch_shapes=[pltpu.VMEM((N_DEV, rows, width), F32),
                        pltpu.SemaphoreType.DMA((N_DEV - 1,)), pltpu.SemaphoreType.DMA((N_DEV - 1,))],
    )(part)


def _row_tile(rows):
    t = min(rows, 256)
    while rows % t:
        t //= 2
    return t


def _core_and_chip():
    return jnp.stack([lax.axis_index("c"), 2 * lax.axis_index("x") + lax.axis_index("y")]).astype(jnp.int32)


def _sum_siblings(own, got, name):
    _, rows, cols = own.shape
    half = rows // 2
    t = _row_tile(half)
    nb = half // t

    def body(place_ref, own_ref, got_ref, mine_ref, out_ref):
        total = own_ref[...] + got_ref[...]
        out_ref[...] = total.astype(BF16)

        @pl.when(pl.program_id(1) == place_ref[1])
        def _():
            mine_ref[...] = total

    slab_blk = pl.BlockSpec((None, t, cols), lambda r, j, place: (j, place[0] * nb + r, 0))
    return pl.pallas_call(
        body, name=name,
        grid_spec=pltpu.PrefetchScalarGridSpec(
            num_scalar_prefetch=1, grid=(nb, N_CHIPS),
            in_specs=[slab_blk, slab_blk],
            out_specs=[pl.BlockSpec((t, cols), lambda r, j, place: (r, 0)), slab_blk]),
        out_shape=[jax.ShapeDtypeStruct((half, cols), F32), jax.ShapeDtypeStruct(own.shape, BF16)],
        compiler_params=_params(("arbitrary", "arbitrary")),
    )(_core_and_chip(), own, got)


def _sum_chips(mine, got, name):
    half, cols = mine.shape
    t = _row_tile(half)
    nb = half // t

    def body(place_ref, mine_ref, got_ref, out_ref):
        total = mine_ref[...]
        for k in range(3):
            total = total + got_ref[k].astype(F32)
        out_ref[...] = total

    return pl.pallas_call(
        body, name=name,
        grid_spec=pltpu.PrefetchScalarGridSpec(
            num_scalar_prefetch=1, grid=(nb,),
            in_specs=[pl.BlockSpec((t, cols), lambda r, place: (r, 0)),
                      pl.BlockSpec((3, t, cols), lambda r, place: (0, place[0] * nb + r, 0))],
            out_specs=pl.BlockSpec((t, cols), lambda r, place: (place[0] * nb + r, 0))),
        out_shape=jax.ShapeDtypeStruct((2 * half, cols), F32),
        compiler_params=_params(("arbitrary",)),
    )(_core_and_chip(), mine, got)


def _adamw(w, g, m, v, name, g_row0=0):
    rows, cols = w.shape
    t = _row_tile(rows)
    assert g_row0 % t == 0
    off = g_row0 // t

    def body(w_ref, g_ref, m_ref, v_ref, go_ref, d_ref, mo_ref, vo_ref):
        g_t = g_ref[...]
        m_new = ADAM_B1 * m_ref[...] + (1.0 - ADAM_B1) * g_t
        v_new = ADAM_B2 * v_ref[...] + (1.0 - ADAM_B2) * (g_t * g_t)
        m_hat = m_new / (1.0 - ADAM_B1 ** ADAM_STEP)
        v_hat = v_new / (1.0 - ADAM_B2 ** ADAM_STEP)
        go_ref[...] = g_t
        d_ref[...] = -ADAM_LR * (m_hat / (jnp.sqrt(v_hat) + ADAM_EPS) + ADAM_WD * w_ref[...])
        mo_ref[...] = m_new
        vo_ref[...] = v_new

    blk = pl.BlockSpec((t, cols), lambda r: (r, 0))
    return pl.pallas_call(
        body, name=name, grid=(rows // t,),
        in_specs=[blk, pl.BlockSpec((t, cols), lambda r: (r + off, 0)), blk, blk],
        out_specs=[blk] * 4,
        out_shape=[jax.ShapeDtypeStruct((rows, cols), F32)] * 4,
        compiler_params=_params(("arbitrary",)),
    )(w, g, m, v)


def _pack_small(d, vectors):
    rows = []
    for vec in vectors:
        flat = vec.reshape(-1)
        n_rows = -(-flat.shape[0] // d)
        rows.append(jnp.pad(flat, (0, n_rows * d - flat.shape[0])).reshape(n_rows, d))
    return jnp.concatenate(rows, axis=0)


def _unpack_small(packed, d, shapes):
    out, row = [], 0
    for shape in shapes:
        size = 1
        for s in shape:
            size *= s
        n_rows = -(-size // d)
        out.append(packed[row:row + n_rows].reshape(-1)[:size].reshape(shape))
        row += n_rows
    return out


def kernel(x, norm_pre, norm_post, pool_w_in, pool_w_group, pool_scale, pool_w_out, att_w_in, att_rel_bias, att_w_out, loss_target, m_norm_pre, m_norm_post, m_pool_w_in, m_pool_w_group, m_pool_scale, m_pool_w_out, m_att_w_in, m_att_rel_bias, m_att_w_out, v_norm_pre, v_norm_post, v_pool_w_in, v_pool_w_group, v_pool_scale, v_pool_w_out, v_att_w_in, v_att_rel_bias, v_att_w_out):
    _, s_len, d = x.shape
    gw = d // 2
    q = gw // N_CHIPS
    x2d = x.reshape(s_len, d)
    target = loss_target.reshape(s_len, d)

    def pack_g(p_group):
        return p_group.reshape(N_CHIPS * q, gw)

    wp_shard = jnp.concatenate([pool_w_in[0], pool_w_out[0]], axis=0).astype(BF16)
    wt_shard = jnp.concatenate([att_w_in[0], att_w_out[0]], axis=0).astype(BF16)
    wp, wg = _gather_weights([wp_shard, pack_g(pool_w_group).astype(BF16)])
    wg = wg.reshape(N_CHIPS, N_CHIPS, q, gw)

    x1, h0, mixed, z0, y0, wt = _l0_fwd(x2d, norm_pre[0:1], norm_post[0:1], pool_scale, wp, wg, wt_shard)
    h1, q_, k_, v_, z1 = _l1_inproj(x1, norm_pre[1:2], wt)
    bias = _bias_tiles(att_rel_bias[0])
    o, lse = _attn_fwd(q_, k_, v_, bias)

    g2, gated1, dy1, d_o, dz1, dgpost1, sq = _l1_out(o, z1, x1, target, norm_post[1:2], wt)
    dq, dk, dv, dbias = _attn_bwd(q_, k_, v_, bias, d_o, o, lse)
    dx1, du1, dgpre1 = _l1_in_bwd(dq, dk, dv, dz1, x1, g2, norm_pre[1:2], wt)
    d_rel = _rel_bias_grad(dbias)

    blk = lambda n: n
    zero = lambda n: 0
    slab_t = d + d // N_CHIPS
    whole_in = pl.BlockSpec((None, d, d), lambda n, kk: (n, 0, 0))
    gt = _wgrad(h1, du1, "wgrad_att_in", N_CHIPS, d, d, zero, blk, (N_CHIPS, slab_t, d), whole_in)
    gt = _wgrad(gated1, dy1, "wgrad_att_out", 1, d, d, zero, zero, (N_CHIPS, slab_t, d),
                pl.BlockSpec((N_CHIPS, d // N_CHIPS, d), lambda n, kk: (0, N_CHIPS, 0)), into=gt)

    grad_x, du0, gated0, dy0, dmm, dgpre0, dgpost0, dscale = _l0_bwd(
        dx1, x2d, y0, mixed, z0, norm_pre[0:1], norm_post[0:1], pool_scale, wp, wg)

    swap_rider = functools.partial(_SwapPlan, send_half=_keep_own_half)
    scatter_rider = lambda send: (_ScatterPlan, send, (3,) + send.shape[1:], (3, 3))
    gi, got_t = _wgrad(h0, du0, "wgrad_pool_in", N_CHIPS, d, d, zero, blk, (N_CHIPS, d, d), whole_in,
                       rider=(swap_rider, gt, gt.shape, (1, 1)))
    mine_t, send_t = _sum_siblings(gt, got_t, "sum_siblings_att")
    go, recv_t = _wgrad(gated0, dy0, "wgrad_pool_out", N_CHIPS, gw, d, blk, zero, (N_CHIPS, gw, d),
                        pl.BlockSpec((None, gw, d), lambda n, kk: (n, 0, 0)), rider=scatter_rider(send_t))
    (got_i,) = _swap_halves([gi], "swap_core_partials_pool_in", _keep_own_half, in_place=False)
    mine_i, send_i = _sum_siblings(gi, got_i, "sum_siblings_pool_in")
    gg, recv_i = _wgrad(mixed, dmm, "wgrad_pool_group", N_CHIPS, gw, gw, blk, blk, (N_CHIPS, N_CHIPS, q, gw),
                        pl.BlockSpec((N_CHIPS, None, q, gw), lambda n, kk: (0, n, 0, 0)), rider=scatter_rider(send_i))
    gg = gg.reshape(N_CHIPS, N_CHIPS * q, gw)
    got_o, got_g = _swap_halves([go, gg], "swap_core_partials_pool", _keep_own_half, in_place=False)
    mine_o, send_o = _sum_siblings(go, got_o, "sum_siblings_pool_out")
    mine_g, send_g = _sum_siblings(gg, got_g, "sum_siblings_group")
    recv_o, recv_g = _scatter_to_chips([send_o, send_g])
    red_t = _sum_chips(mine_t, recv_t, "sum_chips_att")
    red_i = _sum_chips(mine_i, recv_i, "sum_chips_pool_in")
    red_o = _sum_chips(mine_o, recv_o, "sum_chips_pool_out")
    red_g = _sum_chips(mine_g, recv_g, "sum_chips_group")
    grad_t, grad_i, grad_o, grad_g = _swap_halves([red_t, red_i, red_o, red_g], "swap_reduced_halves",
                                                  lambda c: c, in_place=True)

    small_shapes = [norm_pre.shape, norm_post.shape, pool_scale.shape, att_rel_bias.shape]
    part = _pack_small(d, [jnp.concatenate([dgpre0, dgpre1], axis=0), jnp.concatenate([dgpost0, dgpost1], axis=0),
                           dscale, d_rel, sq])
    total = _allreduce_small(part)
    loss = (0.5 / d) * jnp.sum(total[-1])
    w_small = _pack_small(d, [norm_pre, norm_post, pool_scale, att_rel_bias, jnp.zeros((d,), F32)])
    m_small = _pack_small(d, [m_norm_pre, m_norm_post, m_pool_scale, m_att_rel_bias, jnp.zeros((d,), F32)])
    v_small = _pack_small(d, [v_norm_pre, v_norm_post, v_pool_scale, v_att_rel_bias, jnp.ones((d,), F32)])
    small_out = [_unpack_small(a, d, small_shapes) for a in _adamw(w_small, total, m_small, v_small, "adamw_small")]

    big = {}
    for name, w, m, v, grad, row0 in [("pool_w_in", pool_w_in, m_pool_w_in, v_pool_w_in, grad_i, 0),
                                      ("pool_w_out", pool_w_out, m_pool_w_out, v_pool_w_out, grad_o, 0),
                                      ("att_w_in", att_w_in, m_att_w_in, v_att_w_in, grad_t, 0),
                                      ("att_w_out", att_w_out, m_att_w_out, v_att_w_out, grad_t, d)]:
        outs = _adamw(w[0], grad, m[0], v[0], "adamw_" + name, g_row0=row0)
        big[name] = [a.reshape(w.shape) for a in outs]
    outs = _adamw(pack_g(pool_w_group), grad_g, pack_g(m_pool_w_group), pack_g(v_pool_w_group), "adamw_pool_w_group")
    big["pool_w_group"] = [a.reshape(pool_w_group.shape) for a in outs]

    def leaf(kind):
        return (small_out[kind][0], small_out[kind][1], big["pool_w_in"][kind], big["pool_w_group"][kind],
                small_out[kind][2], big["pool_w_out"][kind], big["att_w_in"][kind], small_out[kind][3],
                big["att_w_out"][kind])

    return (loss, grad_x.reshape(x.shape), *leaf(0), *leaf(1), *leaf(2), *leaf(3))
```

```python
import functools

import jax
import jax.numpy as jnp
from jax import lax
from jax.experimental import pallas as pl
from jax.experimental.pallas import tpu as pltpu

F32 = jnp.float32
BF16 = jnp.bfloat16

RMS_EPS = 1e-6
CHUNK = 64
HEAD_DIM = 64
LEFT_CHUNKS = 8
PAD = LEFT_CHUNKS * CHUNK
MAX_REL = 256
POOL_WINDOWS = (2, 4, 8, 16)
HALO = 16
TS = 256
QB = 256
KB = QB + PAD
HEAD_PAIR = 2 * HEAD_DIM
NEG = -1e30
N_CHIPS = 4
N_DEV = 8

ADAM_LR = 0.001
ADAM_B1 = 0.9
ADAM_B2 = 0.999
ADAM_EPS = 1e-08
ADAM_WD = 0.01
ADAM_STEP = 10

VMEM_LIMIT = 56 * 1024 * 1024
MESH = pl.DeviceIdType.MESH
ANY = pl.BlockSpec(memory_space=pl.ANY)


def _dot(a, b):
    return jnp.dot(a, b, preferred_element_type=F32)


def _dot_nt(a, b):
    return lax.dot_general(a, b, (((1,), (1,)), ((), ())), preferred_element_type=F32)


def _dot_tn(a, b):
    return lax.dot_general(a, b, (((0,), (0,)), ((), ())), preferred_element_type=F32)


def _params(sem, limit=VMEM_LIMIT):
    return pltpu.CompilerParams(dimension_semantics=sem, vmem_limit_bytes=limit)


def _rms(x):
    return lax.rsqrt(jnp.mean(x * x, axis=-1, keepdims=True) + RMS_EPS)


def _rms_bwd(dyh, xh, r):
    return r * (dyh - xh * jnp.mean(dyh * xh, axis=-1, keepdims=True))


def _silu_parts(z):
    sig = jax.nn.sigmoid(z)
    return z * sig, sig * (1.0 + z * (1.0 - sig))


def _inv_count(tile, rows, w):
    t = tile * rows + lax.broadcasted_iota(jnp.int32, (rows, 1), 0)
    return 1.0 / jnp.minimum(t + 1, w).astype(F32)


def _start_weight_copies(copies):
    for c in copies:
        c.start()
    for c in copies:
        c.wait()


def _l0_fwd(x, g_pre, g_post, scale, wa, wg, next_shard):
    s_len, d = x.shape
    pw, gw = 2 * d, d // 2
    q = gw // N_CHIPS
    nt = s_len // TS
    assert nt >= 3

    def body(x_ref, gpre_ref, gpost_ref, sc_ref, wa_ref, wg_ref, shard_ref,
             x1_ref, h_ref, mixed_ref, z_ref, y_ref, next_ref,
             win_s, wout_s, wg_s, halo_s, sems, *gather_sems):
        i = pl.program_id(0)

        @pl.when(i == 0)
        def _():
            _GatherPlan([shard_ref], [next_ref], *gather_sems).send()
            copies = [pltpu.make_async_copy(wa_ref.at[:, pl.ds(0, d), :], win_s, sems.at[0]),
                      pltpu.make_async_copy(wa_ref.at[:, pl.ds(d, gw), :], wout_s, sems.at[1])]
            copies += [pltpu.make_async_copy(wg_ref.at[j], wg_s.at[:, pl.ds(j * q, q), :], sems.at[2 + j])
                       for j in range(N_CHIPS)]
            _start_weight_copies(copies)
            halo_s[...] = jnp.zeros_like(halo_s)

        @pl.when(i == nt // 2)
        def _():
            _GatherPlan([shard_ref], [next_ref], *gather_sems).pass_on()

        @pl.when(i == nt - 1)
        def _():
            _GatherPlan([shard_ref], [next_ref], *gather_sems).finish()

        xt = x_ref[...]
        h = ((xt * _rms(xt)) * gpre_ref[...]).astype(BF16)
        h_ref[...] = h
        a_blocks = [_dot(h, win_s[0]), _dot(h, win_s[1])]
        y = jnp.zeros((TS, d), F32)
        for g, w in enumerate(POOL_WINDOWS):
            a_g = a_blocks[g // 2][:, (g % 2) * gw:(g % 2 + 1) * gw]
            ext = jnp.concatenate([halo_s[g], a_g], axis=0)
            shift = 1
            while shift < w:
                ext = ext + pltpu.roll(ext, shift, 0)
                shift *= 2
            mixed = (ext[HALO:] * _inv_count(i, TS, w) - a_g).astype(BF16)
            halo_s[g] = a_g[TS - HALO:]
            mixed_ref[:, g * gw:(g + 1) * gw] = mixed
            z_g = _dot(h, win_s[2 + g // 2, :, (g % 2) * gw:(g % 2 + 1) * gw])
            z_ref[:, g * gw:(g + 1) * gw] = z_g
            ms = _dot(mixed, wg_s[g]) * sc_ref[:, g * gw:(g + 1) * gw]
            gated = (ms * _silu_parts(z_g)[0]).astype(BF16)
            y = y + _dot(gated, wout_s[g])
        y_ref[...] = y
        x1_ref[...] = xt + (y * _rms(y)) * gpost_ref[...]

    tile = lambda wdt: pl.BlockSpec((TS, wdt), lambda i: (i, 0))
    row = lambda wdt: pl.BlockSpec((1, wdt), lambda i: (0, 0))
    return pl.pallas_call(
        body, name="l0_fwd", grid=(nt,),
        in_specs=[tile(d), row(d), row(d), row(pw), ANY, ANY, ANY],
        out_specs=[tile(d), tile(d), tile(pw), tile(pw), tile(d), ANY],
        out_shape=[jax.ShapeDtypeStruct((s_len, d), F32), jax.ShapeDtypeStruct((s_len, d), BF16),
                   jax.ShapeDtypeStruct((s_len, pw), BF16), jax.ShapeDtypeStruct((s_len, pw), F32),
                   jax.ShapeDtypeStruct((s_len, d), F32),
                   jax.ShapeDtypeStruct((N_CHIPS,) + next_shard.shape, next_shard.dtype)],
        scratch_shapes=[pltpu.VMEM((N_CHIPS, d, d), BF16), pltpu.VMEM((N_CHIPS, gw, d), BF16),
                        pltpu.VMEM((N_CHIPS, gw, gw), BF16), pltpu.VMEM((N_CHIPS, HALO, gw), F32),
                        pltpu.SemaphoreType.DMA((2 + N_CHIPS,))] + _gather_sems(1),
        compiler_params=_params(("arbitrary",)),
    )(x, g_pre, g_post, scale, wa, wg, next_shard)


def _l0_bwd(dx1, x, y, mixed, z, g_pre, g_post, scale, wa, wg):
    s_len, d = x.shape
    pw, gw = 2 * d, d // 2
    q = gw // N_CHIPS
    nt = s_len // TS

    def body(dx1_ref, x_ref, y_ref, mixed_ref, z_ref, gpre_ref, gpost_ref, sc_ref, wa_ref, wg_ref,
             gx_ref, du_ref, gated_ref, dy_ref, dmm_ref, dgpre_ref, dgpost_ref, dsc_ref,
             win_s, wout_s, wg_s, halo_s, sems):
        i = pl.program_id(0)
        tile = nt - 1 - i

        @pl.when(i == 0)
        def _():
            copies = [pltpu.make_async_copy(wa_ref.at[:, pl.ds(0, d), :], win_s, sems.at[0]),
                      pltpu.make_async_copy(wa_ref.at[:, pl.ds(d, gw), :], wout_s, sems.at[1])]
            copies += [pltpu.make_async_copy(wg_ref.at[j], wg_s.at[:, pl.ds(j * q, q), :], sems.at[2 + j])
                       for j in range(N_CHIPS)]
            _start_weight_copies(copies)
            halo_s[...] = jnp.zeros_like(halo_s)
            dgpre_ref[...] = jnp.zeros_like(dgpre_ref)
            dgpost_ref[...] = jnp.zeros_like(dgpost_ref)
            dsc_ref[...] = jnp.zeros_like(dsc_ref)

        g_in = dx1_ref[...]
        yt = y_ref[...]
        r_y = _rms(yt)
        yh = yt * r_y
        dgpost_ref[...] += jnp.sum(g_in * yh, axis=0, keepdims=True)
        dy = _rms_bwd(g_in * gpost_ref[...], yh, r_y).astype(BF16)
        dy_ref[...] = dy
        dh = jnp.zeros((TS, d), F32)
        for g, w in enumerate(POOL_WINDOWS):
            cols = slice(g * gw, (g + 1) * gw)
            dgated = _dot_nt(dy, wout_s[g])
            mm = _dot(mixed_ref[:, cols], wg_s[g])
            sc = sc_ref[:, cols]
            ms = mm * sc
            z_g = z_ref[:, cols]
            sz, dsz = _silu_parts(z_g)
            gated_ref[:, cols] = (ms * sz).astype(BF16)
            dms = dgated * sz
            dz = (dgated * ms * dsz).astype(BF16)
            dsc_ref[:, cols] += jnp.sum(dms * mm, axis=0, keepdims=True)
            dmm = (dms * sc).astype(BF16)
            dmm_ref[:, cols] = dmm
            dmixed = _dot_nt(dmm, wg_s[g])
            e = dmixed * _inv_count(tile, TS, w)
            ext = jnp.concatenate([e, halo_s[g]], axis=0)
            shift = 1
            while shift < w:
                ext = ext + pltpu.roll(ext, TS + HALO - shift, 0)
                shift *= 2
            da = (ext[:TS] - dmixed).astype(BF16)
            halo_s[g] = e[:HALO]
            du_ref[:, cols] = da
            du_ref[:, pw + g * gw:pw + (g + 1) * gw] = dz
            wa_blk = win_s[g // 2, :, (g % 2) * gw:(g % 2 + 1) * gw]
            wz_blk = win_s[2 + g // 2, :, (g % 2) * gw:(g % 2 + 1) * gw]
            dh = dh + _dot_nt(da, wa_blk) + _dot_nt(dz, wz_blk)
        xt = x_ref[...]
        r_x = _rms(xt)
        xh = xt * r_x
        dgpre_ref[...] += jnp.sum(dh * xh, axis=0, keepdims=True)
        gx_ref[...] = g_in + _rms_bwd(dh * gpre_ref[...], xh, r_x)

    tile_spec = lambda wdt: pl.BlockSpec((TS, wdt), lambda i: (nt - 1 - i, 0))
    row = lambda wdt: pl.BlockSpec((1, wdt), lambda i: (0, 0))
    return pl.pallas_call(
        body, name="l0_bwd", grid=(nt,),
        in_specs=[tile_spec(d), tile_spec(d), tile_spec(d), tile_spec(pw), tile_spec(pw),
                  row(d), row(d), row(pw), ANY, ANY],
        out_specs=[tile_spec(d), tile_spec(2 * pw), tile_spec(pw), tile_spec(d), tile_spec(pw),
                   row(d), row(d), row(pw)],
        out_shape=[jax.ShapeDtypeStruct((s_len, d), F32), jax.ShapeDtypeStruct((s_len, 2 * pw), BF16),
                   jax.ShapeDtypeStruct((s_len, pw), BF16), jax.ShapeDtypeStruct((s_len, d), BF16),
                   jax.ShapeDtypeStruct((s_len, pw), BF16),
                   jax.ShapeDtypeStruct((1, d), F32), jax.ShapeDtypeStruct((1, d), F32),
                   jax.ShapeDtypeStruct((1, pw), F32)],
        scratch_shapes=[pltpu.VMEM((N_CHIPS, d, d), BF16), pltpu.VMEM((N_CHIPS, gw, d), BF16),
                        pltpu.VMEM((N_CHIPS, gw, gw), BF16), pltpu.VMEM((N_CHIPS, HALO, gw), F32),
                        pltpu.SemaphoreType.DMA((2 + N_CHIPS,))],
        compiler_params=_params(("arbitrary",)),
    )(dx1, x, y, mixed, z, g_pre, g_post, scale, wa, wg)


def _l1_inproj(x1, g_pre, wa):
    s_len, d = x1.shape
    nt = s_len // TS
    npad = PAD // TS

    def body(x_ref, gpre_ref, wa_ref, h_ref, q_ref, k_ref, v_ref, z_ref, win_s, sem):
        i = pl.program_id(0)

        @pl.when(i == 0)
        def _():
            _start_weight_copies([pltpu.make_async_copy(wa_ref.at[:, pl.ds(0, d), :], win_s, sem.at[0])])

        @pl.when(i < npad)
        def _():
            k_ref[...] = jnp.zeros_like(k_ref)
            v_ref[...] = jnp.zeros_like(v_ref)

        @pl.when(i >= npad)
        def _():
            xt = x_ref[...]
            h = ((xt * _rms(xt)) * gpre_ref[...]).astype(BF16)
            h_ref[...] = h
            q_ref[...] = _dot(h, win_s[0]).astype(BF16)
            k_ref[...] = _dot(h, win_s[1]).astype(BF16)
            v_ref[...] = _dot(h, win_s[2]).astype(BF16)
            z_ref[...] = _dot(h, win_s[3])

    tile = pl.BlockSpec((TS, d), lambda i: (jnp.maximum(i - npad, 0), 0))
    padded = pl.BlockSpec((TS, d), lambda i: (i, 0))
    return pl.pallas_call(
        body, name="l1_inproj", grid=(nt + npad,),
        in_specs=[tile, pl.BlockSpec((1, d), lambda i: (0, 0)), ANY],
        out_specs=[tile, tile, padded, padded, tile],
        out_shape=[jax.ShapeDtypeStruct((s_len, d), BF16), jax.ShapeDtypeStruct((s_len, d), BF16),
                   jax.ShapeDtypeStruct((PAD + s_len, d), BF16), jax.ShapeDtypeStruct((PAD + s_len, d), BF16),
                   jax.ShapeDtypeStruct((s_len, d), F32)],
        scratch_shapes=[pltpu.VMEM((N_CHIPS, d, d), BF16), pltpu.SemaphoreType.DMA((1,))],
        compiler_params=_params(("arbitrary",)),
    )(x1, g_pre, wa)


SKEW = QB + KB


def _bias_tiles(rel_bias):
    nh = rel_bias.shape[0]
    assert QB == MAX_REL
    by_column = jnp.concatenate([jnp.broadcast_to(rel_bias[:, 2 * MAX_REL:], (nh, PAD + 1)),
                                 jnp.flip(rel_bias[:, 1:2 * MAX_REL], axis=1)], axis=1).astype(F32)

    def body(col_ref, out_ref):
        rows = jnp.broadcast_to(col_ref[pl.ds(pl.program_id(0), 1), :], (QB, SKEW))
        tile = pltpu.roll(rows, SKEW - QB, 1, stride=1, stride_axis=0)[:, :KB]
        i = lax.broadcasted_iota(jnp.int32, (QB, KB), 0)
        j = lax.broadcasted_iota(jnp.int32, (QB, KB), 1)
        first = (i // CHUNK) * CHUNK
        out_ref[...] = jnp.where((j >= first) & (j < first + PAD + CHUNK), tile, NEG)

    return pl.pallas_call(
        body, name="bias_tiles", grid=(nh,),
        in_specs=[pl.BlockSpec((nh, SKEW), lambda h: (0, 0))],
        out_specs=pl.BlockSpec((None, QB, KB), lambda h: (h, 0, 0)),
        out_shape=jax.ShapeDtypeStruct((nh, QB, KB), F32),
        compiler_params=_params(("arbitrary",)),
    )(by_column)


ROWS = 16


BLOCKS_PER_STEP = 4


def _row_blocks(rows=ROWS):
    return [pl.ds(r * rows, rows) for r in range(QB // rows)]


def _attn_fwd(q, k, v, bias):
    s_len, d = q.shape
    nhp = d // HEAD_PAIR
    per_step = min(BLOCKS_PER_STEP, s_len // QB)
    nq = s_len // (QB * per_step)

    def body(q_ref, k_ref, v_ref, b_ref, o_ref, lse_ref, s_s, p_s, l_s):
        head_of_lane = lax.broadcasted_iota(jnp.int32, (1, HEAD_PAIR), 1) // HEAD_DIM

        units = [(u, hh) for u in range(per_step) for hh in range(2)]

        def band(u):
            return pl.ds(pl.multiple_of((pl.program_id(1) * per_step + u) * QB, QB), KB)

        def issue_scores(n):
            u, hh = units[n]
            qt = q_ref[pl.ds(u * QB, QB), :] * (HEAD_DIM ** -0.5)
            s_s[n % 2] = _dot_nt(jnp.where(head_of_lane == hh, qt, jnp.zeros_like(qt)), k_ref[band(u), :])

        issue_scores(0)
        outs, lses = [], []
        for n, (u, hh) in enumerate(units):
            if n + 1 < len(units):
                issue_scores(n + 1)
            first_key = (pl.program_id(1) * per_step + u) * QB
            colvalid = (first_key + lax.broadcasted_iota(jnp.int32, (1, KB), 1)) >= PAD
            for rows in _row_blocks():
                s = jnp.where(colvalid, s_s[n % 2, rows, :] + b_ref[hh, rows, :], NEG)
                m = jnp.max(s, axis=-1, keepdims=True)
                e = jnp.exp(s - m)
                l = jnp.sum(e, axis=-1, keepdims=True)
                p_s[n % 2, rows, :] = (e * (1.0 / l)).astype(BF16)
                l_s[n % 2, rows, :] = m + jnp.log(l)
            outs.append(_dot(p_s[n % 2], v_ref[band(u), :]))
            lses.append(l_s[n % 2])
            if hh == 1:
                here = pl.ds(u * QB, QB)
                o_ref[here, :] = jnp.where(head_of_lane == 0, outs[-2], outs[-1])
                lse_ref[here, :] = jnp.where(head_of_lane == 0, lses[-2], lses[-1])

    blk = pl.BlockSpec((per_step * QB, HEAD_PAIR), lambda hp, b: (b, hp))
    whole = pl.BlockSpec((PAD + s_len, HEAD_PAIR), lambda hp, b: (0, hp))
    return pl.pallas_call(
        body, name="attn_fwd", grid=(nhp, nq),
        in_specs=[blk, whole, whole, pl.BlockSpec((2, QB, KB), lambda hp, b: (hp, 0, 0))],
        out_specs=[blk, blk],
        out_shape=[jax.ShapeDtypeStruct((s_len, d), F32), jax.ShapeDtypeStruct((s_len, d), F32)],
        scratch_shapes=[pltpu.VMEM((2, QB, KB), F32), pltpu.VMEM((2, QB, KB), BF16), pltpu.VMEM((2, QB, 1), F32)],
        compiler_params=_params(("arbitrary", "arbitrary")),
    )(q, k, v, bias)


def _attn_bwd(q, k, v, bias, d_o, o, lse):
    s_len, d = q.shape
    nhp = d // HEAD_PAIR
    per_step = min(BLOCKS_PER_STEP, s_len // QB)
    nq = s_len // (QB * per_step)
    qk_scale = HEAD_DIM ** -0.5

    def body(q_ref, k_ref, v_ref, b_ref, do_ref, o_ref, lse_ref, dq_ref, dk_ref, dv_ref, db_ref,
             s_s, dp_s, p_s, ds_s, lse_s, delta_s):
        @pl.when(pl.program_id(1) == 0)
        def _():
            dk_ref[...] = jnp.zeros_like(dk_ref)
            dv_ref[...] = jnp.zeros_like(dv_ref)
            db_ref[...] = jnp.zeros_like(db_ref)

        head_of_lane = lax.broadcasted_iota(jnp.int32, (1, HEAD_PAIR), 1) // HEAD_DIM

        units = [(u, hh) for u in range(per_step) for hh in range(2)]

        def band(u):
            return pl.ds(pl.multiple_of((pl.program_id(1) * per_step + u) * QB, QB), KB)

        def masked(ref, u, hh, factor=None):
            x = ref[pl.ds(u * QB, QB), :]
            x = x if factor is None else x * factor
            return jnp.where(head_of_lane == hh, x, jnp.zeros_like(x))

        def issue_tiles(n):
            u, hh = units[n]
            here = pl.ds(u * QB, QB)
            do_o = do_ref[here, :].astype(F32) * o_ref[here, :]
            delta_s[n % 2] = jnp.sum(jnp.where(head_of_lane == hh, do_o, 0.0), axis=-1, keepdims=True)
            lse_s[n % 2] = lse_ref[here, hh * HEAD_DIM:hh * HEAD_DIM + 1]
            s_s[n % 2] = _dot_nt(masked(q_ref, u, hh, qk_scale), k_ref[band(u), :])
            dp_s[n % 2] = _dot_nt(masked(do_ref, u, hh), v_ref[band(u), :])

        issue_tiles(0)
        dq_heads, dk_band, dv_band = [], None, None
        for n, (u, hh) in enumerate(units):
            if n + 1 < len(units):
                issue_tiles(n + 1)
            first_key = (pl.program_id(1) * per_step + u) * QB
            colvalid = (first_key + lax.broadcasted_iota(jnp.int32, (1, KB), 1)) >= PAD
            for rows in _row_blocks():
                t = jnp.where(colvalid, s_s[n % 2, rows, :] + b_ref[hh, rows, :] - lse_s[n % 2, rows, :], NEG)
                p = jnp.exp(t)
                ds = p * (dp_s[n % 2, rows, :] - delta_s[n % 2, rows, :])
                db_ref[hh, rows, :] += ds
                p_s[n % 2, rows, :] = p.astype(BF16)
                ds_s[n % 2, rows, :] = ds.astype(BF16)
            q_m = masked(q_ref, u, hh, qk_scale)
            dv_unit = _dot_tn(p_s[n % 2], masked(do_ref, u, hh))
            dq_heads.append(_dot(ds_s[n % 2], k_ref[band(u), :]) * qk_scale)
            dk_unit = _dot_tn(ds_s[n % 2], q_m)
            if hh == 0:
                dk_band, dv_band = dk_unit, dv_unit
            else:
                dq_ref[pl.ds(u * QB, QB), :] = jnp.where(head_of_lane == 0, dq_heads[-2], dq_heads[-1]).astype(BF16)
                dk_ref[band(u), :] += dk_band + dk_unit
                dv_ref[band(u), :] += dv_band + dv_unit

    blk = pl.BlockSpec((per_step * QB, HEAD_PAIR), lambda hp, b: (b, hp))
    whole = pl.BlockSpec((PAD + s_len, HEAD_PAIR), lambda hp, b: (0, hp))
    btile = pl.BlockSpec((2, QB, KB), lambda hp, b: (hp, 0, 0))
    return pl.pallas_call(
        body, name="attn_bwd", grid=(nhp, nq),
        in_specs=[blk, whole, whole, btile, blk, blk, blk],
        out_specs=[blk, whole, whole, btile],
        out_shape=[jax.ShapeDtypeStruct((s_len, d), BF16),
                   jax.ShapeDtypeStruct((PAD + s_len, d), F32), jax.ShapeDtypeStruct((PAD + s_len, d), F32),
                   jax.ShapeDtypeStruct(bias.shape, F32)],
        scratch_shapes=[pltpu.VMEM((2, QB, KB), F32), pltpu.VMEM((2, QB, KB), F32),
                        pltpu.VMEM((2, QB, KB), BF16), pltpu.VMEM((2, QB, KB), BF16),
                        pltpu.VMEM((2, QB, 1), F32), pltpu.VMEM((2, QB, 1), F32)],
        compiler_params=_params(("arbitrary", "arbitrary")),
    )(q, k, v, bias, d_o, o, lse)


def _rel_bias_grad(db):
    nh = db.shape[0]
    assert QB == MAX_REL

    def body(db_ref, out_ref):
        i0 = lax.broadcasted_iota(jnp.int32, (QB, QB), 0)
        i1 = lax.broadcasted_iota(jnp.int32, (QB, QB), 1)
        exchange = jnp.where(i0 + i1 == QB - 1, 1.0, 0.0).astype(BF16)
        rest = db_ref[...]
        flipped = jnp.zeros((QB, KB), F32)
        for _ in range(3):
            piece = rest.astype(BF16)
            flipped = flipped + _dot(exchange, piece)
            rest = rest - piece.astype(F32)
        m = jnp.concatenate([flipped, jnp.zeros((QB, SKEW - KB), F32)], axis=1)
        diag = jnp.sum(pltpu.roll(m, 1, 1, stride=1, stride_axis=0), axis=0, keepdims=True)
        c = lax.broadcasted_iota(jnp.int32, (1, SKEW), 1)
        clipped = jnp.sum(jnp.where(c <= PAD, diag, 0.0), axis=1, keepdims=True)
        out_ref[...] = jnp.where(c == 0, clipped, diag)

    diag = pl.pallas_call(
        body, name="bias_diagonals", grid=(nh,),
        in_specs=[pl.BlockSpec((None, QB, KB), lambda h: (h, 0, 0))],
        out_specs=pl.BlockSpec((None, 1, SKEW), lambda h: (h, 0, 0)),
        out_shape=jax.ShapeDtypeStruct((nh, 1, SKEW), F32),
        compiler_params=_params(("arbitrary",)),
    )(db)[:, 0]
    return jnp.concatenate([jnp.zeros((nh, 1), F32), jnp.flip(diag[:, PAD + 1:], axis=1), diag[:, :1]], axis=1)


def _l1_out(o, z, x1, target, g_post, wa):
    s_len, d = o.shape
    nt = s_len // TS
    slab = d // N_CHIPS

    def body(o_ref, z_ref, x1_ref, t_ref, gpost_ref, wa_ref,
             g2_ref, gated_ref, dy_ref, do_ref, dz_ref, dgpost_ref, sq_ref, wout_s, sems):
        i = pl.program_id(0)

        @pl.when(i == 0)
        def _():
            _start_weight_copies([
                pltpu.make_async_copy(wa_ref.at[j, pl.ds(d, slab), :],
                                      wout_s.at[pl.ds(j * slab, slab), :], sems.at[j])
                for j in range(N_CHIPS)])
            dgpost_ref[...] = jnp.zeros_like(dgpost_ref)
            sq_ref[...] = jnp.zeros_like(sq_ref)

        ot = o_ref[...]
        sz, dsz = _silu_parts(z_ref[...])
        gated = (ot * sz).astype(BF16)
        gated_ref[...] = gated
        y = _dot(gated, wout_s[...])
        r_y = _rms(y)
        yh = y * r_y
        err = x1_ref[...] + yh * gpost_ref[...] - t_ref[...]
        sq_ref[...] += jnp.sum(err * err, axis=0, keepdims=True)
        g2 = err * (1.0 / d)
        g2_ref[...] = g2
        dgpost_ref[...] += jnp.sum(g2 * yh, axis=0, keepdims=True)
        dy = _rms_bwd(g2 * gpost_ref[...], yh, r_y).astype(BF16)
        dy_ref[...] = dy
        dgated = _dot_nt(dy, wout_s[...])
        do_ref[...] = (dgated * sz).astype(BF16)
        dz_ref[...] = (dgated * ot * dsz).astype(BF16)

    tile = pl.BlockSpec((TS, d), lambda i: (i, 0))
    row = pl.BlockSpec((1, d), lambda i: (0, 0))
    return pl.pallas_call(
        body, name="l1_out", grid=(nt,),
        in_specs=[tile, tile, tile, tile, row, ANY],
        out_specs=[tile, tile, tile, tile, tile, row, row],
        out_shape=[jax.ShapeDtypeStruct((s_len, d), F32)] + [jax.ShapeDtypeStruct((s_len, d), BF16)] * 4
                  + [jax.ShapeDtypeStruct((1, d), F32)] * 2,
        scratch_shapes=[pltpu.VMEM((d, d), BF16), pltpu.SemaphoreType.DMA((N_CHIPS,))],
        compiler_params=_params(("arbitrary",)),
    )(o, z, x1, target, g_post, wa)


def _l1_in_bwd(dq, dk, dv, dz, x1, g2, g_pre, wa):
    s_len, d = x1.shape
    nt = s_len // TS
    npad = PAD // TS

    def body(dq_ref, dk_ref, dv_ref, dz_ref, x_ref, g2_ref, gpre_ref, wa_ref,
             dx_ref, du_ref, dgpre_ref, win_s, sem):
        i = pl.program_id(0)

        @pl.when(i == 0)
        def _():
            _start_weight_copies([pltpu.make_async_copy(wa_ref.at[:, pl.ds(0, d), :], win_s, sem.at[0])])
            dgpre_ref[...] = jnp.zeros_like(dgpre_ref)

        parts = [dq_ref[...], dk_ref[...].astype(BF16), dv_ref[...].astype(BF16), dz_ref[...]]
        dh = jnp.zeros((TS, d), F32)
        for j, part in enumerate(parts):
            du_ref[:, j * d:(j + 1) * d] = part
            dh = dh + _dot_nt(part, win_s[j])
        xt = x_ref[...]
        r_x = _rms(xt)
        xh = xt * r_x
        dgpre_ref[...] += jnp.sum(dh * xh, axis=0, keepdims=True)
        dx_ref[...] = g2_ref[...] + _rms_bwd(dh * gpre_ref[...], xh, r_x)

    tile = pl.BlockSpec((TS, d), lambda i: (i, 0))
    padded = pl.BlockSpec((TS, d), lambda i: (i + npad, 0))
    row = pl.BlockSpec((1, d), lambda i: (0, 0))
    return pl.pallas_call(
        body, name="l1_in_bwd", grid=(nt,),
        in_specs=[tile, padded, padded, tile, tile, tile, row, ANY],
        out_specs=[tile, pl.BlockSpec((TS, 4 * d), lambda i: (i, 0)), row],
        out_shape=[jax.ShapeDtypeStruct((s_len, d), F32), jax.ShapeDtypeStruct((s_len, 4 * d), BF16),
                   jax.ShapeDtypeStruct((1, d), F32)],
        scratch_shapes=[pltpu.VMEM((N_CHIPS, d, d), BF16), pltpu.SemaphoreType.DMA((1,))],
        compiler_params=_params(("arbitrary",)),
    )(dq, dk, dv, dz, x1, g2, g_pre, wa)


def _wgrad(a, b, name, n_out, a_width, b_width, a_block, b_block, out_shape, out_spec, into=None, rider=None):
    s_len = a.shape[0]
    tk = min(1024, s_len)
    nk = s_len // tk
    n_in = 2 + (into is not None) + (rider is not None)

    def body(*refs):
        a_ref, b_ref = refs[:2]
        out_ref = refs[n_in]
        if rider is not None:
            plan = lambda: rider[0]([refs[n_in - 1]], [refs[n_in + 1]], *refs[n_in + 2:])
            first = (pl.program_id(0) == 0) & (pl.program_id(1) == 0)
            last = (pl.program_id(0) == n_out - 1) & (pl.program_id(1) == nk - 1)

            @pl.when(first)
            def _():
                plan().send()

        @pl.when(pl.program_id(1) == 0)
        def _():
            out_ref[...] = jnp.zeros_like(out_ref)

        out_ref[...] += _dot_tn(a_ref[...], b_ref[...]).reshape(out_ref.shape)

        if rider is not None:
            @pl.when(last)
            def _():
                plan().finish()

    operands = [a, b] + ([into] if into is not None else []) + ([rider[1]] if rider is not None else [])
    results = pl.pallas_call(
        body, name=name, grid=(n_out, nk),
        in_specs=[pl.BlockSpec((tk, a_width), lambda n, kk: (kk, a_block(n))),
                  pl.BlockSpec((tk, b_width), lambda n, kk: (kk, b_block(n)))] + [ANY] * (n_in - 2),
        out_specs=[out_spec] + ([ANY] if rider is not None else []),
        out_shape=[jax.ShapeDtypeStruct(out_shape, F32)]
                  + ([jax.ShapeDtypeStruct(rider[2], rider[1].dtype)] if rider is not None else []),
        scratch_shapes=[pltpu.SemaphoreType.DMA((n,)) for n in rider[3]] if rider is not None else [],
        input_output_aliases={2: 0} if into is not None else {},
        compiler_params=_params(("arbitrary", "arbitrary")),
    )(*operands)
    return results if rider is not None else results[0]


def _place():
    x, y, c = lax.axis_index("x"), lax.axis_index("y"), lax.axis_index("c")
    others = [(1 - x, y), (x, 1 - y), (1 - x, 1 - y)]
    return x, y, c, others


class _GatherPlan:
    def __init__(self, src, dst, send, recv, fwd_send, fwd_recv, local):
        x, y, c, others = _place()
        me = 2 * x + y
        sibling = (x, y, 1 - c)

        def half(ref, cc):
            rows = ref.shape[0] // 2
            return ref.at[pl.ds(cc * rows, rows), :]

        self.mine = [pltpu.make_async_copy(src[a], dst[a].at[me], local.at[a]) for a in range(len(src))]
        self.first, self.passed, self.arrive, self.arrive_fwd = [], [], [], []
        for a in range(len(src)):
            for k, (ox, oy) in enumerate(others):
                sem = a * 3 + k
                self.first.append(pltpu.make_async_remote_copy(
                    src_ref=half(src[a], c), dst_ref=half(dst[a].at[me], c),
                    send_sem=send.at[sem], recv_sem=recv.at[sem], device_id=(ox, oy, c), device_id_type=MESH))
                theirs = half(dst[a].at[2 * ox + oy], c)
                self.arrive.append(pltpu.make_async_remote_copy(
                    src_ref=theirs, dst_ref=theirs, send_sem=send.at[sem], recv_sem=recv.at[sem],
                    device_id=(ox, oy, c), device_id_type=MESH))
                self.passed.append(pltpu.make_async_remote_copy(
                    src_ref=theirs, dst_ref=theirs, send_sem=fwd_send.at[sem], recv_sem=fwd_recv.at[sem],
                    device_id=sibling, device_id_type=MESH))
                other_half = half(dst[a].at[2 * ox + oy], 1 - c)
                self.arrive_fwd.append(pltpu.make_async_remote_copy(
                    src_ref=other_half, dst_ref=other_half, send_sem=fwd_send.at[sem], recv_sem=fwd_recv.at[sem],
                    device_id=sibling, device_id_type=MESH))

    def send(self):
        for cp in self.mine + self.first:
            cp.start()

    def pass_on(self):
        for got, onward in zip(self.arrive, self.passed):
            got.wait_recv()
            onward.start()

    def finish(self):
        for got in self.arrive_fwd:
            got.wait_recv()
        for cp in self.first + self.passed:
            cp.wait_send()
        for cp in self.mine:
            cp.wait()


def _gather_sems(n):
    return [pltpu.SemaphoreType.DMA((3 * n,))] * 4 + [pltpu.SemaphoreType.DMA((n,))]


def _gather_weights(shards):
    n = len(shards)

    def body(*refs):
        plan = _GatherPlan(refs[:n], refs[n:2 * n], *refs[2 * n:])
        plan.send()
        plan.pass_on()
        plan.finish()

    return pl.pallas_call(
        body, name="gather_weights",
        in_specs=[ANY] * n, out_specs=[ANY] * n,
        out_shape=[jax.ShapeDtypeStruct((N_CHIPS,) + s.shape, s.dtype) for s in shards],
        scratch_shapes=_gather_sems(n),
    )(*shards)


class _SwapPlan:
    def __init__(self, src, dst, send, recv, send_half):
        x, y, c, _ = _place()
        sibling = (x, y, 1 - c)
        self.out_copies, self.in_copies = [], []
        for a in range(len(src)):
            rows = src[a].shape[-2] // 2
            lead = (slice(None),) * (len(src[a].shape) - 2)
            going = lead + (pl.ds(send_half(c) * rows, rows), slice(None))
            coming = lead + (pl.ds((1 - send_half(c)) * rows, rows), slice(None))
            self.out_copies.append(pltpu.make_async_remote_copy(
                src_ref=src[a].at[going], dst_ref=dst[a].at[going], send_sem=send.at[a], recv_sem=recv.at[a],
                device_id=sibling, device_id_type=MESH))
            self.in_copies.append(pltpu.make_async_remote_copy(
                src_ref=src[a].at[coming], dst_ref=dst[a].at[coming], send_sem=send.at[a], recv_sem=recv.at[a],
                device_id=sibling, device_id_type=MESH))

    def send(self):
        for cp in self.out_copies:
            cp.start()

    def finish(self):
        for cp in self.in_copies:
            cp.wait_recv()
        for cp in self.out_copies:
            cp.wait_send()


def _keep_own_half(c):
    return 1 - c


def _swap_halves(arrays, name, send_half, in_place):
    n = len(arrays)

    def body(*refs):
        src, dst = refs[:n], refs[n:2 * n]
        plan = _SwapPlan(dst if in_place else src, dst, *refs[2 * n:], send_half=send_half)
        plan.send()
        plan.finish()

    return pl.pallas_call(
        body, name=name,
        in_specs=[ANY] * n, out_specs=[ANY] * n,
        out_shape=[jax.ShapeDtypeStruct(a.shape, a.dtype) for a in arrays],
        scratch_shapes=[pltpu.SemaphoreType.DMA((n,)), pltpu.SemaphoreType.DMA((n,))],
        input_output_aliases={a: a for a in range(n)} if in_place else {},
    )(*arrays)


class _ScatterPlan:
    def __init__(self, src, dst, send, recv):
        x, y, c, others = _place()
        self.out_copies, self.in_copies = [], []
        for a in range(len(src)):
            rows = src[a].shape[1] // 2
            mine = pl.ds(c * rows, rows)
            for k, (ox, oy) in enumerate(others):
                sem = a * 3 + k
                self.out_copies.append(pltpu.make_async_remote_copy(
                    src_ref=src[a].at[2 * ox + oy, mine, :], dst_ref=dst[a].at[k, mine, :],
                    send_sem=send.at[sem], recv_sem=recv.at[sem], device_id=(ox, oy, c), device_id_type=MESH))
                self.in_copies.append(pltpu.make_async_remote_copy(
                    src_ref=dst[a].at[k, mine, :], dst_ref=dst[a].at[k, mine, :],
                    send_sem=send.at[sem], recv_sem=recv.at[sem], device_id=(ox, oy, c), device_id_type=MESH))

    def send(self):
        for cp in self.out_copies:
            cp.start()

    def finish(self):
        for cp in self.in_copies:
            cp.wait_recv()
        for cp in self.out_copies:
            cp.wait_send()


def _scatter_to_chips(arrays):
    n = len(arrays)

    def body(*refs):
        plan = _ScatterPlan(refs[:n], refs[n:2 * n], *refs[2 * n:])
        plan.send()
        plan.finish()

    return pl.pallas_call(
        body, name="scatter_to_chips",
        in_specs=[ANY] * n, out_specs=[ANY] * n,
        out_shape=[jax.ShapeDtypeStruct((3,) + a.shape[1:], a.dtype) for a in arrays],
        scratch_shapes=[pltpu.SemaphoreType.DMA((3 * n,)), pltpu.SemaphoreType.DMA((3 * n,))],
    )(*arrays)


def _allreduce_small(part):
    rows, width = part.shape

    def body(p_ref, out_ref, all_ref, send, recv):
        x, y, c, _ = _place()
        me = 4 * x + 2 * y + c
        all_ref[me] = p_ref[...]
        copies = []
        for k in range(1, N_DEV):
            px, py, pc = x ^ (k >> 2), y ^ ((k >> 1) & 1), c ^ (k & 1)
            copies.append(pltpu.make_async_remote_copy(
                src_ref=p_ref, dst_ref=all_ref.at[me], send_sem=send.at[k - 1], recv_sem=recv.at[k - 1],
                device_id=(px, py, pc), device_id_type=MESH))
        for cp in copies:
            cp.start()
        for cp in copies:
            cp.wait()
        total = all_ref[0]
        for k in range(1, N_DEV):
            total = total + all_ref[k]
        out_ref[...] = total

    return pl.pallas_call(
        body, name="allreduce_small",
        in_specs=[pl.BlockSpec(memory_space=pltpu.VMEM)],
        out_specs=pl.BlockSpec(memory_space=pltpu.VMEM),
        out_shape=jax.ShapeDtypeStruct((rows, width), F32),
        scratch_shapes=[pltpu.VMEM((N_DEV, rows, width), F32),
                        pltpu.SemaphoreType.DMA((N_DEV - 1,)), pltpu.SemaphoreType.DMA((N_DEV - 1,))],
    )(part)


def _row_tile(rows):
    t = min(rows, 256)
    while rows % t:
        t //= 2
    return t


def _core_and_chip():
    return jnp.stack([lax.axis_index("c"), 2 * lax.axis_index("x") + lax.axis_index("y")]).astype(jnp.int32)


def _sum_siblings(own, got, name):
    _, rows, cols = own.shape
    half = rows // 2
    t = _row_tile(half)
    nb = half // t

    def body(place_ref, own_ref, got_ref, mine_ref, out_ref):
        total = own_ref[...] + got_ref[...]
        out_ref[...] = total.astype(BF16)

        @pl.when(pl.program_id(1) == place_ref[1])
        def _():
            mine_ref[...] = total

    slab_blk = pl.BlockSpec((None, t, cols), lambda r, j, place: (j, place[0] * nb + r, 0))
    return pl.pallas_call(
        body, name=name,
        grid_spec=pltpu.PrefetchScalarGridSpec(
            num_scalar_prefetch=1, grid=(nb, N_CHIPS),
            in_specs=[slab_blk, slab_blk],
            out_specs=[pl.BlockSpec((t, cols), lambda r, j, place: (r, 0)), slab_blk]),
        out_shape=[jax.ShapeDtypeStruct((half, cols), F32), jax.ShapeDtypeStruct(own.shape, BF16)],
        compiler_params=_params(("arbitrary", "arbitrary")),
    )(_core_and_chip(), own, got)


def _sum_chips(mine, got, name):
    half, cols = mine.shape
    t = _row_tile(half)
    nb = half // t

    def body(place_ref, mine_ref, got_ref, out_ref):
        total = mine_ref[...]
        for k in range(3):
            total = total + got_ref[k].astype(F32)
        out_ref[...] = total

    return pl.pallas_call(
        body, name=name,
        grid_spec=pltpu.PrefetchScalarGridSpec(
            num_scalar_prefetch=1, grid=(nb,),
            in_specs=[pl.BlockSpec((t, cols), lambda r, place: (r, 0)),
                      pl.BlockSpec((3, t, cols), lambda r, place: (0, place[0] * nb + r, 0))],
            out_specs=pl.BlockSpec((t, cols), lambda r, place: (place[0] * nb + r, 0))),
        out_shape=jax.ShapeDtypeStruct((2 * half, cols), F32),
        compiler_params=_params(("arbitrary",)),
    )(_core_and_chip(), mine, got)


def _adamw(w, g, m, v, name, g_row0=0):
    rows, cols = w.shape
    t = _row_tile(rows)
    assert g_row0 % t == 0
    off = g_row0 // t

    def body(w_ref, g_ref, m_ref, v_ref, go_ref, d_ref, mo_ref, vo_ref):
        g_t = g_ref[...]
        m_new = ADAM_B1 * m_ref[...] + (1.0 - ADAM_B1) * g_t
        v_new = ADAM_B2 * v_ref[...] + (1.0 - ADAM_B2) * (g_t * g_t)
        m_hat = m_new / (1.0 - ADAM_B1 ** ADAM_STEP)
        v_hat = v_new / (1.0 - ADAM_B2 ** ADAM_STEP)
        go_ref[...] = g_t
        d_ref[...] = -ADAM_LR * (m_hat / (jnp.sqrt(v_hat) + ADAM_EPS) + ADAM_WD * w_ref[...])
        mo_ref[...] = m_new
        vo_ref[...] = v_new

    blk = pl.BlockSpec((t, cols), lambda r: (r, 0))
    return pl.pallas_call(
        body, name=name, grid=(rows // t,),
        in_specs=[blk, pl.BlockSpec((t, cols), lambda r: (r + off, 0)), blk, blk],
        out_specs=[blk] * 4,
        out_shape=[jax.ShapeDtypeStruct((rows, cols), F32)] * 4,
        compiler_params=_params(("arbitrary",)),
    )(w, g, m, v)


def _pack_small(d, vectors):
    rows = []
    for vec in vectors:
        flat = vec.reshape(-1)
        n_rows = -(-flat.shape[0] // d)
        rows.append(jnp.pad(flat, (0, n_rows * d - flat.shape[0])).reshape(n_rows, d))
    return jnp.concatenate(rows, axis=0)


def _unpack_small(packed, d, shapes):
    out, row = [], 0
    for shape in shapes:
        size = 1
        for s in shape:
            size *= s
        n_rows = -(-size // d)
        out.append(packed[row:row + n_rows].reshape(-1)[:size].reshape(shape))
        row += n_rows
    return out


def kernel(x, norm_pre, norm_post, pool_w_in, pool_w_group, pool_scale, pool_w_out, att_w_in, att_rel_bias, att_w_out, loss_target, m_norm_pre, m_norm_post, m_pool_w_in, m_pool_w_group, m_pool_scale, m_pool_w_out, m_att_w_in, m_att_rel_bias, m_att_w_out, v_norm_pre, v_norm_post, v_pool_w_in, v_pool_w_group, v_pool_scale, v_pool_w_out, v_att_w_in, v_att_rel_bias, v_att_w_out):
    _, s_len, d = x.shape
    gw = d // 2
    q = gw // N_CHIPS
    x2d = x.reshape(s_len, d)
    target = loss_target.reshape(s_len, d)

    def pack_g(p_group):
        return p_group.reshape(N_CHIPS * q, gw)

    wp_shard = jnp.concatenate([pool_w_in[0], pool_w_out[0]], axis=0).astype(BF16)
    wt_shard = jnp.concatenate([att_w_in[0], att_w_out[0]], axis=0).astype(BF16)
    wp, wg = _gather_weights([wp_shard, pack_g(pool_w_group).astype(BF16)])
    wg = wg.reshape(N_CHIPS, N_CHIPS, q, gw)

    x1, h0, mixed, z0, y0, wt = _l0_fwd(x2d, norm_pre[0:1], norm_post[0:1], pool_scale, wp, wg, wt_shard)
    h1, q_, k_, v_, z1 = _l1_inproj(x1, norm_pre[1:2], wt)
    bias = _bias_tiles(att_rel_bias[0])
    o, lse = _attn_fwd(q_, k_, v_, bias)

    g2, gated1, dy1, d_o, dz1, dgpost1, sq = _l1_out(o, z1, x1, target, norm_post[1:2], wt)
    dq, dk, dv, dbias = _attn_bwd(q_, k_, v_, bias, d_o, o, lse)
    dx1, du1, dgpre1 = _l1_in_bwd(dq, dk, dv, dz1, x1, g2, norm_pre[1:2], wt)
    d_rel = _rel_bias_grad(dbias)

    blk = lambda n: n
    zero = lambda n: 0
    slab_t = d + d // N_CHIPS
    whole_in = pl.BlockSpec((None, d, d), lambda n, kk: (n, 0, 0))
    gt = _wgrad(h1, du1, "wgrad_att_in", N_CHIPS, d, d, zero, blk, (N_CHIPS, slab_t, d), whole_in)
    gt = _wgrad(gated1, dy1, "wgrad_att_out", 1, d, d, zero, zero, (N_CHIPS, slab_t, d),
                pl.BlockSpec((N_CHIPS, d // N_CHIPS, d), lambda n, kk: (0, N_CHIPS, 0)), into=gt)

    grad_x, du0, gated0, dy0, dmm, dgpre0, dgpost0, dscale = _l0_bwd(
        dx1, x2d, y0, mixed, z0, norm_pre[0:1], norm_post[0:1], pool_scale, wp, wg)

    swap_rider = functools.partial(_SwapPlan, send_half=_keep_own_half)
    scatter_rider = lambda send: (_ScatterPlan, send, (3,) + send.shape[1:], (3, 3))
    gi, got_t = _wgrad(h0, du0, "wgrad_pool_in", N_CHIPS, d, d, zero, blk, (N_CHIPS, d, d), whole_in,
                       rider=(swap_rider, gt, gt.shape, (1, 1)))
    mine_t, send_t = _sum_siblings(gt, got_t, "sum_siblings_att")
    go, recv_t = _wgrad(gated0, dy0, "wgrad_pool_out", N_CHIPS, gw, d, blk, zero, (N_CHIPS, gw, d),
                        pl.BlockSpec((None, gw, d), lambda n, kk: (n, 0, 0)), rider=scatter_rider(send_t))
    (got_i,) = _swap_halves([gi], "swap_core_partials_pool_in", _keep_own_half, in_place=False)
    mine_i, send_i = _sum_siblings(gi, got_i, "sum_siblings_pool_in")
    gg, recv_i = _wgrad(mixed, dmm, "wgrad_pool_group", N_CHIPS, gw, gw, blk, blk, (N_CHIPS, N_CHIPS, q, gw),
                        pl.BlockSpec((N_CHIPS, None, q, gw), lambda n, kk: (0, n, 0, 0)), rider=scatter_rider(send_i))
    gg = gg.reshape(N_CHIPS, N_CHIPS * q, gw)
    got_o, got_g = _swap_halves([go, gg], "swap_core_partials_pool", _keep_own_half, in_place=False)
    mine_o, send_o = _sum_siblings(go, got_o, "sum_siblings_pool_out")
    mine_g, send_g = _sum_siblings(gg, got_g, "sum_siblings_group")
    recv_o, recv_g = _scatter_to_chips([send_o, send_g])
    red_t = _sum_chips(mine_t, recv_t, "sum_chips_att")
    red_i = _sum_chips(mine_i, recv_i, "sum_chips_pool_in")
    red_o = _sum_chips(mine_o, recv_o, "sum_chips_pool_out")
    red_g = _sum_chips(mine_g, recv_g, "sum_chips_group")
    grad_t, grad_i, grad_o, grad_g = _swap_halves([red_t, red_i, red_o, red_g], "swap_reduced_halves",
                                                  lambda c: c, in_place=True)

    small_shapes = [norm_pre.shape, norm_post.shape, pool_scale.shape, att_rel_bias.shape]
    part = _pack_small(d, [jnp.concatenate([dgpre0, dgpre1], axis=0), jnp.concatenate([dgpost0, dgpost1], axis=0),
                           dscale, d_rel, sq])
    total = _allreduce_small(part)
    loss = (0.5 / d) * jnp.sum(total[-1])
    w_small = _pack_small(d, [norm_pre, norm_post, pool_scale, att_rel_bias, jnp.zeros((d,), F32)])
    m_small = _pack_small(d, [m_norm_pre, m_norm_post, m_pool_scale, m_att_rel_bias, jnp.zeros((d,), F32)])
    v_small = _pack_small(d, [v_norm_pre, v_norm_post, v_pool_scale, v_att_rel_bias, jnp.ones((d,), F32)])
    small_out = [_unpack_small(a, d, small_shapes) for a in _adamw(w_small, total, m_small, v_small, "adamw_small")]

    big = {}
    for name, w, m, v, grad, row0 in [("pool_w_in", pool_w_in, m_pool_w_in, v_pool_w_in, grad_i, 0),
                                      ("pool_w_out", pool_w_out, m_pool_w_out, v_pool_w_out, grad_o, 0),
                                      ("att_w_in", att_w_in, m_att_w_in, v_att_w_in, grad_t, 0),
                                      ("att_w_out", att_w_out, m_att_w_out, v_att_w_out, grad_t, d)]:
        outs = _adamw(w[0], grad, m[0], v[0], "adamw_" + name, g_row0=row0)
        big[name] = [a.reshape(w.shape) for a in outs]
    outs = _adamw(pack_g(pool_w_group), grad_g, pack_g(m_pool_w_group), pack_g(v_pool_w_group), "adamw_pool_w_group")
    big["pool_w_group"] = [a.reshape(pool_w_group.shape) for a in outs]

    def leaf(kind):
        return (small_out[kind][0], small_out[kind][1], big["pool_w_in"][kind], big["pool_w_group"][kind],
                small_out[kind][2], big["pool_w_out"][kind], big["att_w_in"][kind], small_out[kind][3],
                big["att_w_out"][kind])

    return (loss, grad_x.reshape(x.shape), *leaf(0), *leaf(1), *leaf(2), *leaf(3))
```

```python
import functools

import jax
import jax.numpy as jnp
from jax import lax
from jax.experimental import pallas as pl
from jax.experimental.pallas import tpu as pltpu

F32 = jnp.float32
BF16 = jnp.bfloat16

RMS_EPS = 1e-6
CHUNK = 64
HEAD_DIM = 64
LEFT_CHUNKS = 8
PAD = LEFT_CHUNKS * CHUNK
MAX_REL = 256
POOL_WINDOWS = (2, 4, 8, 16)
HALO = 16
TS = 256
TS1 = 512
QB = 256
KB = QB + PAD
HEAD_PAIR = 2 * HEAD_DIM
NEG = -1e30
N_CHIPS = 4
N_DEV = 8

ADAM_LR = 0.001
ADAM_B1 = 0.9
ADAM_B2 = 0.999
ADAM_EPS = 1e-08
ADAM_WD = 0.01
ADAM_STEP = 10

VMEM_LIMIT = 56 * 1024 * 1024
MESH = pl.DeviceIdType.MESH
ANY = pl.BlockSpec(memory_space=pl.ANY)


def _dot(a, b):
    return jnp.dot(a, b, preferred_element_type=F32)


def _dot_nt(a, b):
    return lax.dot_general(a, b, (((1,), (1,)), ((), ())), preferred_element_type=F32)


def _dot_tn(a, b):
    return lax.dot_general(a, b, (((0,), (0,)), ((), ())), preferred_element_type=F32)


def _params(sem, limit=VMEM_LIMIT):
    return pltpu.CompilerParams(dimension_semantics=sem, vmem_limit_bytes=limit)


def _rms(x):
    return lax.rsqrt(jnp.mean(x * x, axis=-1, keepdims=True) + RMS_EPS)


def _rms_bwd(dyh, xh, r):
    return r * (dyh - xh * jnp.mean(dyh * xh, axis=-1, keepdims=True))


def _silu_parts(z):
    sig = jax.nn.sigmoid(z)
    return z * sig, sig * (1.0 + z * (1.0 - sig))


def _inv_count(tile, rows, w):
    t = tile * rows + lax.broadcasted_iota(jnp.int32, (rows, 1), 0)
    return 1.0 / jnp.minimum(t + 1, w).astype(F32)


def _start_weight_copies(copies):
    for c in copies:
        c.start()
    for c in copies:
        c.wait()


def _l0_fwd(x, g_pre, g_post, scale, wa, wg, next_shard):
    s_len, d = x.shape
    pw, gw = 2 * d, d // 2
    q = gw // N_CHIPS
    nt = s_len // TS
    assert nt >= 3

    def body(x_ref, gpre_ref, gpost_ref, sc_ref, wa_ref, wg_ref, shard_ref,
             x1_ref, h_ref, mixed_ref, z_ref, y_ref, next_ref,
             win_s, wout_s, wg_s, halo_s, sems, *gather_sems):
        i = pl.program_id(0)

        @pl.when(i == 0)
        def _():
            _GatherPlan([shard_ref], [next_ref], *gather_sems).send()
            copies = [pltpu.make_async_copy(wa_ref.at[:, pl.ds(0, d), :], win_s, sems.at[0]),
                      pltpu.make_async_copy(wa_ref.at[:, pl.ds(d, gw), :], wout_s, sems.at[1])]
            copies += [pltpu.make_async_copy(wg_ref.at[j], wg_s.at[:, pl.ds(j * q, q), :], sems.at[2 + j])
                       for j in range(N_CHIPS)]
            _start_weight_copies(copies)
            halo_s[...] = jnp.zeros_like(halo_s)

        @pl.when(i == nt // 2)
        def _():
            _GatherPlan([shard_ref], [next_ref], *gather_sems).pass_on()

        @pl.when(i == nt - 1)
        def _():
            _GatherPlan([shard_ref], [next_ref], *gather_sems).finish()

        xt = x_ref[...]
        h = ((xt * _rms(xt)) * gpre_ref[...]).astype(BF16)
        h_ref[...] = h
        a_blocks = [_dot(h, win_s[0]), _dot(h, win_s[1])]
        y = jnp.zeros((TS, d), F32)
        for g, w in enumerate(POOL_WINDOWS):
            a_g = a_blocks[g // 2][:, (g % 2) * gw:(g % 2 + 1) * gw]
            ext = jnp.concatenate([halo_s[g], a_g], axis=0)
            shift = 1
            while shift < w:
                ext = ext + pltpu.roll(ext, shift, 0)
                shift *= 2
            mixed = (ext[HALO:] * _inv_count(i, TS, w) - a_g).astype(BF16)
            halo_s[g] = a_g[TS - HALO:]
            mixed_ref[:, g * gw:(g + 1) * gw] = mixed
            z_g = _dot(h, win_s[2 + g // 2, :, (g % 2) * gw:(g % 2 + 1) * gw])
            z_ref[:, g * gw:(g + 1) * gw] = z_g
            ms = _dot(mixed, wg_s[g]) * sc_ref[:, g * gw:(g + 1) * gw]
            gated = (ms * _silu_parts(z_g)[0]).astype(BF16)
            y = y + _dot(gated, wout_s[g])
        y_ref[...] = y
        x1_ref[...] = xt + (y * _rms(y)) * gpost_ref[...]

    tile = lambda wdt: pl.BlockSpec((TS, wdt), lambda i: (i, 0))
    row = lambda wdt: pl.BlockSpec((1, wdt), lambda i: (0, 0))
    return pl.pallas_call(
        body, name="l0_fwd", grid=(nt,),
        in_specs=[tile(d), row(d), row(d), row(pw), ANY, ANY, ANY],
        out_specs=[tile(d), tile(d), tile(pw), tile(pw), tile(d), ANY],
        out_shape=[jax.ShapeDtypeStruct((s_len, d), F32), jax.ShapeDtypeStruct((s_len, d), BF16),
                   jax.ShapeDtypeStruct((s_len, pw), BF16), jax.ShapeDtypeStruct((s_len, pw), F32),
                   jax.ShapeDtypeStruct((s_len, d), F32),
                   jax.ShapeDtypeStruct((N_CHIPS,) + next_shard.shape, next_shard.dtype)],
        scratch_shapes=[pltpu.VMEM((N_CHIPS, d, d), BF16), pltpu.VMEM((N_CHIPS, gw, d), BF16),
                        pltpu.VMEM((N_CHIPS, gw, gw), BF16), pltpu.VMEM((N_CHIPS, HALO, gw), F32),
                        pltpu.SemaphoreType.DMA((2 + N_CHIPS,))] + _gather_sems(1),
        compiler_params=_params(("arbitrary",)),
    )(x, g_pre, g_post, scale, wa, wg, next_shard)


def _l0_bwd(dx1, x, y, mixed, z, g_pre, g_post, scale, wa, wg):
    s_len, d = x.shape
    pw, gw = 2 * d, d // 2
    q = gw // N_CHIPS
    nt = s_len // TS

    def body(dx1_ref, x_ref, y_ref, mixed_ref, z_ref, gpre_ref, gpost_ref, sc_ref, wa_ref, wg_ref,
             gx_ref, du_ref, gated_ref, dy_ref, dmm_ref, dgpre_ref, dgpost_ref, dsc_ref,
             win_s, wout_s, wg_s, halo_s, sems):
        i = pl.program_id(0)
        tile = nt - 1 - i

        @pl.when(i == 0)
        def _():
            copies = [pltpu.make_async_copy(wa_ref.at[:, pl.ds(0, d), :], win_s, sems.at[0]),
                      pltpu.make_async_copy(wa_ref.at[:, pl.ds(d, gw), :], wout_s, sems.at[1])]
            copies += [pltpu.make_async_copy(wg_ref.at[j], wg_s.at[:, pl.ds(j * q, q), :], sems.at[2 + j])
                       for j in range(N_CHIPS)]
            _start_weight_copies(copies)
            halo_s[...] = jnp.zeros_like(halo_s)
            dgpre_ref[...] = jnp.zeros_like(dgpre_ref)
            dgpost_ref[...] = jnp.zeros_like(dgpost_ref)
            dsc_ref[...] = jnp.zeros_like(dsc_ref)

        g_in = dx1_ref[...]
        yt = y_ref[...]
        r_y = _rms(yt)
        yh = yt * r_y
        dgpost_ref[...] += jnp.sum(g_in * yh, axis=0, keepdims=True)
        dy = _rms_bwd(g_in * gpost_ref[...], yh, r_y).astype(BF16)
        dy_ref[...] = dy
        dh = jnp.zeros((TS, d), F32)
        for g, w in enumerate(POOL_WINDOWS):
            cols = slice(g * gw, (g + 1) * gw)
            dgated = _dot_nt(dy, wout_s[g])
            mm = _dot(mixed_ref[:, cols], wg_s[g])
            sc = sc_ref[:, cols]
            ms = mm * sc
            z_g = z_ref[:, cols]
            sz, dsz = _silu_parts(z_g)
            gated_ref[:, cols] = (ms * sz).astype(BF16)
            dms = dgated * sz
            dz = (dgated * ms * dsz).astype(BF16)
            dsc_ref[:, cols] += jnp.sum(dms * mm, axis=0, keepdims=True)
            dmm = (dms * sc).astype(BF16)
            dmm_ref[:, cols] = dmm
            dmixed = _dot_nt(dmm, wg_s[g])
            e = dmixed * _inv_count(tile, TS, w)
            ext = jnp.concatenate([e, halo_s[g]], axis=0)
            shift = 1
            while shift < w:
                ext = ext + pltpu.roll(ext, TS + HALO - shift, 0)
                shift *= 2
            da = (ext[:TS] - dmixed).astype(BF16)
            halo_s[g] = e[:HALO]
            du_ref[:, cols] = da
            du_ref[:, pw + g * gw:pw + (g + 1) * gw] = dz
            wa_blk = win_s[g // 2, :, (g % 2) * gw:(g % 2 + 1) * gw]
            wz_blk = win_s[2 + g // 2, :, (g % 2) * gw:(g % 2 + 1) * gw]
            dh = dh + _dot_nt(da, wa_blk) + _dot_nt(dz, wz_blk)
        xt = x_ref[...]
        r_x = _rms(xt)
        xh = xt * r_x
        dgpre_ref[...] += jnp.sum(dh * xh, axis=0, keepdims=True)
        gx_ref[...] = g_in + _rms_bwd(dh * gpre_ref[...], xh, r_x)

    tile_spec = lambda wdt: pl.BlockSpec((TS, wdt), lambda i: (nt - 1 - i, 0))
    row = lambda wdt: pl.BlockSpec((1, wdt), lambda i: (0, 0))
    return pl.pallas_call(
        body, name="l0_bwd", grid=(nt,),
        in_specs=[tile_spec(d), tile_spec(d), tile_spec(d), tile_spec(pw), tile_spec(pw),
                  row(d), row(d), row(pw), ANY, ANY],
        out_specs=[tile_spec(d), tile_spec(2 * pw), tile_spec(pw), tile_spec(d), tile_spec(pw),
                   row(d), row(d), row(pw)],
        out_shape=[jax.ShapeDtypeStruct((s_len, d), F32), jax.ShapeDtypeStruct((s_len, 2 * pw), BF16),
                   jax.ShapeDtypeStruct((s_len, pw), BF16), jax.ShapeDtypeStruct((s_len, d), BF16),
                   jax.ShapeDtypeStruct((s_len, pw), BF16),
                   jax.ShapeDtypeStruct((1, d), F32), jax.ShapeDtypeStruct((1, d), F32),
                   jax.ShapeDtypeStruct((1, pw), F32)],
        scratch_shapes=[pltpu.VMEM((N_CHIPS, d, d), BF16), pltpu.VMEM((N_CHIPS, gw, d), BF16),
                        pltpu.VMEM((N_CHIPS, gw, gw), BF16), pltpu.VMEM((N_CHIPS, HALO, gw), F32),
                        pltpu.SemaphoreType.DMA((2 + N_CHIPS,))],
        compiler_params=_params(("arbitrary",)),
    )(dx1, x, y, mixed, z, g_pre, g_post, scale, wa, wg)


def _l1_inproj(x1, g_pre, wa):
    s_len, d = x1.shape
    nt = s_len // TS1
    npad = PAD // TS1

    def body(x_ref, gpre_ref, wa_ref, h_ref, q_ref, k_ref, v_ref, z_ref, win_s, sem):
        i = pl.program_id(0)

        @pl.when(i == 0)
        def _():
            _start_weight_copies([pltpu.make_async_copy(wa_ref.at[:, pl.ds(0, d), :], win_s, sem.at[0])])

        @pl.when(i < npad)
        def _():
            k_ref[...] = jnp.zeros_like(k_ref)
            v_ref[...] = jnp.zeros_like(v_ref)

        @pl.when(i >= npad)
        def _():
            xt = x_ref[...]
            h = ((xt * _rms(xt)) * gpre_ref[...]).astype(BF16)
            h_ref[...] = h
            q_ref[...] = _dot(h, win_s[0]).astype(BF16)
            k_ref[...] = _dot(h, win_s[1]).astype(BF16)
            v_ref[...] = _dot(h, win_s[2]).astype(BF16)
            z_ref[...] = _dot(h, win_s[3])

    tile = pl.BlockSpec((TS1, d), lambda i: (jnp.maximum(i - npad, 0), 0))
    padded = pl.BlockSpec((TS1, d), lambda i: (i, 0))
    return pl.pallas_call(
        body, name="l1_inproj", grid=(nt + npad,),
        in_specs=[tile, pl.BlockSpec((1, d), lambda i: (0, 0)), ANY],
        out_specs=[tile, tile, padded, padded, tile],
        out_shape=[jax.ShapeDtypeStruct((s_len, d), BF16), jax.ShapeDtypeStruct((s_len, d), BF16),
                   jax.ShapeDtypeStruct((PAD + s_len, d), BF16), jax.ShapeDtypeStruct((PAD + s_len, d), BF16),
                   jax.ShapeDtypeStruct((s_len, d), F32)],
        scratch_shapes=[pltpu.VMEM((N_CHIPS, d, d), BF16), pltpu.SemaphoreType.DMA((1,))],
        compiler_params=_params(("arbitrary",)),
    )(x1, g_pre, wa)


SKEW = QB + KB


def _bias_tiles(rel_bias):
    nh = rel_bias.shape[0]
    assert QB == MAX_REL
    by_column = jnp.concatenate([jnp.broadcast_to(rel_bias[:, 2 * MAX_REL:], (nh, PAD + 1)),
                                 jnp.flip(rel_bias[:, 1:2 * MAX_REL], axis=1)], axis=1).astype(F32)

    def body(col_ref, out_ref):
        rows = jnp.broadcast_to(col_ref[pl.ds(pl.program_id(0), 1), :], (QB, SKEW))
        tile = pltpu.roll(rows, SKEW - QB, 1, stride=1, stride_axis=0)[:, :KB]
        i = lax.broadcasted_iota(jnp.int32, (QB, KB), 0)
        j = lax.broadcasted_iota(jnp.int32, (QB, KB), 1)
        first = (i // CHUNK) * CHUNK
        out_ref[...] = jnp.where((j >= first) & (j < first + PAD + CHUNK), tile, NEG)

    return pl.pallas_call(
        body, name="bias_tiles", grid=(nh,),
        in_specs=[pl.BlockSpec((nh, SKEW), lambda h: (0, 0))],
        out_specs=pl.BlockSpec((None, QB, KB), lambda h: (h, 0, 0)),
        out_shape=jax.ShapeDtypeStruct((nh, QB, KB), F32),
        compiler_params=_params(("arbitrary",)),
    )(by_column)


ROWS = 16


BLOCKS_PER_STEP = 4


def _row_blocks(rows=ROWS):
    return [pl.ds(r * rows, rows) for r in range(QB // rows)]


def _attn_fwd(q, k, v, bias):
    s_len, d = q.shape
    nhp = d // HEAD_PAIR
    per_step = min(BLOCKS_PER_STEP, s_len // QB)
    nq = s_len // (QB * per_step)

    def body(q_ref, k_ref, v_ref, b_ref, o_ref, lse_ref, s_s, p_s, l_s):
        head_of_lane = lax.broadcasted_iota(jnp.int32, (1, HEAD_PAIR), 1) // HEAD_DIM

        units = [(u, hh) for u in range(per_step) for hh in range(2)]

        def band(u):
            return pl.ds(pl.multiple_of((pl.program_id(1) * per_step + u) * QB, QB), KB)

        def issue_scores(n):
            u, hh = units[n]
            qt = q_ref[pl.ds(u * QB, QB), :] * (HEAD_DIM ** -0.5)
            s_s[n % 2] = _dot_nt(jnp.where(head_of_lane == hh, qt, jnp.zeros_like(qt)), k_ref[band(u), :])

        issue_scores(0)
        outs, lses = [], []
        for n, (u, hh) in enumerate(units):
            if n + 1 < len(units):
                issue_scores(n + 1)
            first_key = (pl.program_id(1) * per_step + u) * QB
            colvalid = (first_key + lax.broadcasted_iota(jnp.int32, (1, KB), 1)) >= PAD
            for rows in _row_blocks():
                s = jnp.where(colvalid, s_s[n % 2, rows, :] + b_ref[hh, rows, :], NEG)
                m = jnp.max(s, axis=-1, keepdims=True)
                e = jnp.exp(s - m)
                l = jnp.sum(e, axis=-1, keepdims=True)
                p_s[n % 2, rows, :] = (e * (1.0 / l)).astype(BF16)
                l_s[n % 2, rows, :] = m + jnp.log(l)
            outs.append(_dot(p_s[n % 2], v_ref[band(u), :]))
            lses.append(l_s[n % 2])
            if hh == 1:
                here = pl.ds(u * QB, QB)
                o_ref[here, :] = jnp.where(head_of_lane == 0, outs[-2], outs[-1])
                lse_ref[here, :] = jnp.where(head_of_lane == 0, lses[-2], lses[-1])

    blk = pl.BlockSpec((per_step * QB, HEAD_PAIR), lambda hp, b: (b, hp))
    whole = pl.BlockSpec((PAD + s_len, HEAD_PAIR), lambda hp, b: (0, hp))
    return pl.pallas_call(
        body, name="attn_fwd", grid=(nhp, nq),
        in_specs=[blk, whole, whole, pl.BlockSpec((2, QB, KB), lambda hp, b: (hp, 0, 0))],
        out_specs=[blk, blk],
        out_shape=[jax.ShapeDtypeStruct((s_len, d), F32), jax.ShapeDtypeStruct((s_len, d), F32)],
        scratch_shapes=[pltpu.VMEM((2, QB, KB), F32), pltpu.VMEM((2, QB, KB), BF16), pltpu.VMEM((2, QB, 1), F32)],
        compiler_params=_params(("arbitrary", "arbitrary")),
    )(q, k, v, bias)


def _attn_bwd(q, k, v, bias, d_o, o, lse):
    s_len, d = q.shape
    nhp = d // HEAD_PAIR
    per_step = min(BLOCKS_PER_STEP, s_len // QB)
    nq = s_len // (QB * per_step)
    qk_scale = HEAD_DIM ** -0.5

    def body(q_ref, k_ref, v_ref, b_ref, do_ref, o_ref, lse_ref, dq_ref, dk_out_ref, dv_out_ref, db_ref,
             s_s, dp_s, p_s, ds_s, lse_s, delta_s, dk_ref, dv_ref):
        @pl.when(pl.program_id(1) == 0)
        def _():
            dk_ref[...] = jnp.zeros_like(dk_ref)
            dv_ref[...] = jnp.zeros_like(dv_ref)
            db_ref[...] = jnp.zeros_like(db_ref)

        head_of_lane = lax.broadcasted_iota(jnp.int32, (1, HEAD_PAIR), 1) // HEAD_DIM

        units = [(u, hh) for u in range(per_step) for hh in range(2)]

        def band(u):
            return pl.ds(pl.multiple_of((pl.program_id(1) * per_step + u) * QB, QB), KB)

        def masked(ref, u, hh, factor=None):
            x = ref[pl.ds(u * QB, QB), :]
            x = x if factor is None else x * factor
            return jnp.where(head_of_lane == hh, x, jnp.zeros_like(x))

        def issue_tiles(n):
            u, hh = units[n]
            here = pl.ds(u * QB, QB)
            do_o = do_ref[here, :].astype(F32) * o_ref[here, :]
            delta_s[n % 2] = jnp.sum(jnp.where(head_of_lane == hh, do_o, 0.0), axis=-1, keepdims=True)
            lse_s[n % 2] = lse_ref[here, hh * HEAD_DIM:hh * HEAD_DIM + 1]
            s_s[n % 2] = _dot_nt(masked(q_ref, u, hh, qk_scale), k_ref[band(u), :])
            dp_s[n % 2] = _dot_nt(masked(do_ref, u, hh), v_ref[band(u), :])

        issue_tiles(0)
        dq_heads, dk_band, dv_band = [], None, None
        for n, (u, hh) in enumerate(units):
            if n + 1 < len(units):
                issue_tiles(n + 1)
            first_key = (pl.program_id(1) * per_step + u) * QB
            colvalid = (first_key + lax.broadcasted_iota(jnp.int32, (1, KB), 1)) >= PAD
            for rows in _row_blocks():
                t = jnp.where(colvalid, s_s[n % 2, rows, :] + b_ref[hh, rows, :] - lse_s[n % 2, rows, :], NEG)
                p = jnp.exp(t)
                ds = p * (dp_s[n % 2, rows, :] - delta_s[n % 2, rows, :])
                db_ref[hh, rows, :] += ds
                p_s[n % 2, rows, :] = p.astype(BF16)
                ds_s[n % 2, rows, :] = ds.astype(BF16)
            q_m = masked(q_ref, u, hh, qk_scale)
            dv_unit = _dot_tn(p_s[n % 2], masked(do_ref, u, hh))
            dq_heads.append(_dot(ds_s[n % 2], k_ref[band(u), :]) * qk_scale)
            dk_unit = _dot_tn(ds_s[n % 2], q_m)
            if hh == 0:
                dk_band, dv_band = dk_unit, dv_unit
            else:
                dq_ref[pl.ds(u * QB, QB), :] = jnp.where(head_of_lane == 0, dq_heads[-2], dq_heads[-1]).astype(BF16)
                dk_ref[band(u), :] += dk_band + dk_unit
                dv_ref[band(u), :] += dv_band + dv_unit

        @pl.when(pl.program_id(1) == nq - 1)
        def _():
            dk_out_ref[...] = dk_ref[...].astype(BF16)
            dv_out_ref[...] = dv_ref[...].astype(BF16)

    blk = pl.BlockSpec((per_step * QB, HEAD_PAIR), lambda hp, b: (b, hp))
    whole = pl.BlockSpec((PAD + s_len, HEAD_PAIR), lambda hp, b: (0, hp))
    btile = pl.BlockSpec((2, QB, KB), lambda hp, b: (hp, 0, 0))
    return pl.pallas_call(
        body, name="attn_bwd", grid=(nhp, nq),
        in_specs=[blk, whole, whole, btile, blk, blk, blk],
        out_specs=[blk, whole, whole, btile],
        out_shape=[jax.ShapeDtypeStruct((s_len, d), BF16),
                   jax.ShapeDtypeStruct((PAD + s_len, d), BF16), jax.ShapeDtypeStruct((PAD + s_len, d), BF16),
                   jax.ShapeDtypeStruct(bias.shape, F32)],
        scratch_shapes=[pltpu.VMEM((2, QB, KB), F32), pltpu.VMEM((2, QB, KB), F32),
                        pltpu.VMEM((2, QB, KB), BF16), pltpu.VMEM((2, QB, KB), BF16),
                        pltpu.VMEM((2, QB, 1), F32), pltpu.VMEM((2, QB, 1), F32),
                        pltpu.VMEM((PAD + s_len, HEAD_PAIR), F32), pltpu.VMEM((PAD + s_len, HEAD_PAIR), F32)],
        compiler_params=_params(("arbitrary", "arbitrary")),
    )(q, k, v, bias, d_o, o, lse)


def _rel_bias_grad(db):
    nh = db.shape[0]
    assert QB == MAX_REL

    def body(db_ref, out_ref):
        i0 = lax.broadcasted_iota(jnp.int32, (QB, QB), 0)
        i1 = lax.broadcasted_iota(jnp.int32, (QB, QB), 1)
        exchange = jnp.where(i0 + i1 == QB - 1, 1.0, 0.0).astype(BF16)
        rest = db_ref[...]
        flipped = jnp.zeros((QB, KB), F32)
        for _ in range(3):
            piece = rest.astype(BF16)
            flipped = flipped + _dot(exchange, piece)
            rest = rest - piece.astype(F32)
        m = jnp.concatenate([flipped, jnp.zeros((QB, SKEW - KB), F32)], axis=1)
        diag = jnp.sum(pltpu.roll(m, 1, 1, stride=1, stride_axis=0), axis=0, keepdims=True)
        c = lax.broadcasted_iota(jnp.int32, (1, SKEW), 1)
        clipped = jnp.sum(jnp.where(c <= PAD, diag, 0.0), axis=1, keepdims=True)
        out_ref[...] = jnp.where(c == 0, clipped, diag)

    diag = pl.pallas_call(
        body, name="bias_diagonals", grid=(nh,),
        in_specs=[pl.BlockSpec((None, QB, KB), lambda h: (h, 0, 0))],
        out_specs=pl.BlockSpec((None, 1, SKEW), lambda h: (h, 0, 0)),
        out_shape=jax.ShapeDtypeStruct((nh, 1, SKEW), F32),
        compiler_params=_params(("arbitrary",)),
    )(db)[:, 0]
    return jnp.concatenate([jnp.zeros((nh, 1), F32), jnp.flip(diag[:, PAD + 1:], axis=1), diag[:, :1]], axis=1)


def _l1_out(o, z, x1, target, g_post, wa):
    s_len, d = o.shape
    nt = s_len // TS
    slab = d // N_CHIPS

    def body(o_ref, z_ref, x1_ref, t_ref, gpost_ref, wa_ref,
             g2_ref, gated_ref, dy_ref, do_ref, dz_ref, dgpost_ref, sq_ref, wout_s, sems):
        i = pl.program_id(0)

        @pl.when(i == 0)
        def _():
            _start_weight_copies([
                pltpu.make_async_copy(wa_ref.at[j, pl.ds(d, slab), :],
                                      wout_s.at[pl.ds(j * slab, slab), :], sems.at[j])
                for j in range(N_CHIPS)])
            dgpost_ref[...] = jnp.zeros_like(dgpost_ref)
            sq_ref[...] = jnp.zeros_like(sq_ref)

        ot = o_ref[...]
        sz, dsz = _silu_parts(z_ref[...])
        gated = (ot * sz).astype(BF16)
        gated_ref[...] = gated
        y = _dot(gated, wout_s[...])
        r_y = _rms(y)
        yh = y * r_y
        err = x1_ref[...] + yh * gpost_ref[...] - t_ref[...]
        sq_ref[...] += jnp.sum(err * err, axis=0, keepdims=True)
        g2 = err * (1.0 / d)
        g2_ref[...] = g2
        dgpost_ref[...] += jnp.sum(g2 * yh, axis=0, keepdims=True)
        dy = _rms_bwd(g2 * gpost_ref[...], yh, r_y).astype(BF16)
        dy_ref[...] = dy
        dgated = _dot_nt(dy, wout_s[...])
        do_ref[...] = (dgated * sz).astype(BF16)
        dz_ref[...] = (dgated * ot * dsz).astype(BF16)

    tile = pl.BlockSpec((TS, d), lambda i: (i, 0))
    row = pl.BlockSpec((1, d), lambda i: (0, 0))
    return pl.pallas_call(
        body, name="l1_out", grid=(nt,),
        in_specs=[tile, tile, tile, tile, row, ANY],
        out_specs=[tile, tile, tile, tile, tile, row, row],
        out_shape=[jax.ShapeDtypeStruct((s_len, d), F32)] + [jax.ShapeDtypeStruct((s_len, d), BF16)] * 4
                  + [jax.ShapeDtypeStruct((1, d), F32)] * 2,
        scratch_shapes=[pltpu.VMEM((d, d), BF16), pltpu.SemaphoreType.DMA((N_CHIPS,))],
        compiler_params=_params(("arbitrary",)),
    )(o, z, x1, target, g_post, wa)


def _l1_in_bwd(dq, dk, dv, dz, x1, g2, g_pre, wa):
    s_len, d = x1.shape
    nt = s_len // TS1
    npad = PAD // TS1

    def body(dq_ref, dk_ref, dv_ref, dz_ref, x_ref, g2_ref, gpre_ref, wa_ref,
             dx_ref, du_ref, dgpre_ref, win_s, sem):
        i = pl.program_id(0)

        @pl.when(i == 0)
        def _():
            _start_weight_copies([pltpu.make_async_copy(wa_ref.at[:, pl.ds(0, d), :], win_s, sem.at[0])])
            dgpre_ref[...] = jnp.zeros_like(dgpre_ref)

        parts = [dq_ref[...], dk_ref[...], dv_ref[...], dz_ref[...]]
        dh = jnp.zeros((TS1, d), F32)
        for j, part in enumerate(parts):
            du_ref[:, j * d:(j + 1) * d] = part
            dh = dh + _dot_nt(part, win_s[j])
        xt = x_ref[...]
        r_x = _rms(xt)
        xh = xt * r_x
        dgpre_ref[...] += jnp.sum(dh * xh, axis=0, keepdims=True)
        dx_ref[...] = g2_ref[...] + _rms_bwd(dh * gpre_ref[...], xh, r_x)

    tile = pl.BlockSpec((TS1, d), lambda i: (i, 0))
    padded = pl.BlockSpec((TS1, d), lambda i: (i + npad, 0))
    row = pl.BlockSpec((1, d), lambda i: (0, 0))
    return pl.pallas_call(
        body, name="l1_in_bwd", grid=(nt,),
        in_specs=[tile, padded, padded, tile, tile, tile, row, ANY],
        out_specs=[tile, pl.BlockSpec((TS1, 4 * d), lambda i: (i, 0)), row],
        out_shape=[jax.ShapeDtypeStruct((s_len, d), F32), jax.ShapeDtypeStruct((s_len, 4 * d), BF16),
                   jax.ShapeDtypeStruct((1, d), F32)],
        scratch_shapes=[pltpu.VMEM((N_CHIPS, d, d), BF16), pltpu.SemaphoreType.DMA((1,))],
        compiler_params=_params(("arbitrary",)),
    )(dq, dk, dv, dz, x1, g2, g_pre, wa)


def _wgrad(a, b, name, n_out, a_width, b_width, a_block, b_block, out_shape, out_spec, into=None, rider=None):
    s_len = a.shape[0]
    tk = min(2048, s_len)
    nk = s_len // tk
    n_in = 2 + (into is not None) + (rider is not None)

    def body(*refs):
        a_ref, b_ref = refs[:2]
        out_ref = refs[n_in]
        if rider is not None:
            plan = lambda: rider[0]([refs[n_in - 1]], [refs[n_in + 1]], *refs[n_in + 2:])
            first = (pl.program_id(0) == 0) & (pl.program_id(1) == 0)
            last = (pl.program_id(0) == n_out - 1) & (pl.program_id(1) == nk - 1)

            @pl.when(first)
            def _():
                plan().send()

        @pl.when(pl.program_id(1) == 0)
        def _():
            out_ref[...] = jnp.zeros_like(out_ref)

        out_ref[...] += _dot_tn(a_ref[...], b_ref[...]).reshape(out_ref.shape)

        if rider is not None:
            @pl.when(last)
            def _():
                plan().finish()

    operands = [a, b] + ([into] if into is not None else []) + ([rider[1]] if rider is not None else [])
    results = pl.pallas_call(
        body, name=name, grid=(n_out, nk),
        in_specs=[pl.BlockSpec((tk, a_width), lambda n, kk: (kk, a_block(n))),
                  pl.BlockSpec((tk, b_width), lambda n, kk: (kk, b_block(n)))] + [ANY] * (n_in - 2),
        out_specs=[out_spec] + ([ANY] if rider is not None else []),
        out_shape=[jax.ShapeDtypeStruct(out_shape, F32)]
                  + ([jax.ShapeDtypeStruct(rider[2], rider[1].dtype)] if rider is not None else []),
        scratch_shapes=[pltpu.SemaphoreType.DMA((n,)) for n in rider[3]] if rider is not None else [],
        input_output_aliases={2: 0} if into is not None else {},
        compiler_params=_params(("arbitrary", "arbitrary")),
    )(*operands)
    return results if rider is not None else results[0]


def _place():
    x, y, c = lax.axis_index("x"), lax.axis_index("y"), lax.axis_index("c")
    others = [(1 - x, y), (x, 1 - y), (1 - x, 1 - y)]
    return x, y, c, others


class _GatherPlan:
    def __init__(self, src, dst, send, recv, fwd_send, fwd_recv, local):
        x, y, c, others = _place()
        me = 2 * x + y
        sibling = (x, y, 1 - c)

        def half(ref, cc):
            rows = ref.shape[0] // 2
            return ref.at[pl.ds(cc * rows, rows), :]

        self.mine = [pltpu.make_async_copy(src[a], dst[a].at[me], local.at[a]) for a in range(len(src))]
        self.first, self.passed, self.arrive, self.arrive_fwd = [], [], [], []
        for a in range(len(src)):
            for k, (ox, oy) in enumerate(others):
                sem = a * 3 + k
                self.first.append(pltpu.make_async_remote_copy(
                    src_ref=half(src[a], c), dst_ref=half(dst[a].at[me], c),
                    send_sem=send.at[sem], recv_sem=recv.at[sem], device_id=(ox, oy, c), device_id_type=MESH))
                theirs = half(dst[a].at[2 * ox + oy], c)
                self.arrive.append(pltpu.make_async_remote_copy(
                    src_ref=theirs, dst_ref=theirs, send_sem=send.at[sem], recv_sem=recv.at[sem],
                    device_id=(ox, oy, c), device_id_type=MESH))
                self.passed.append(pltpu.make_async_remote_copy(
                    src_ref=theirs, dst_ref=theirs, send_sem=fwd_send.at[sem], recv_sem=fwd_recv.at[sem],
                    device_id=sibling, device_id_type=MESH))
                other_half = half(dst[a].at[2 * ox + oy], 1 - c)
                self.arrive_fwd.append(pltpu.make_async_remote_copy(
                    src_ref=other_half, dst_ref=other_half, send_sem=fwd_send.at[sem], recv_sem=fwd_recv.at[sem],
                    device_id=sibling, device_id_type=MESH))

    def send(self):
        for cp in self.mine + self.first:
            cp.start()

    def pass_on(self):
        for got, onward in zip(self.arrive, self.passed):
            got.wait_recv()
            onward.start()

    def finish(self):
        for got in self.arrive_fwd:
            got.wait_recv()
        for cp in self.first + self.passed:
            cp.wait_send()
        for cp in self.mine:
            cp.wait()


def _gather_sems(n):
    return [pltpu.SemaphoreType.DMA((3 * n,))] * 4 + [pltpu.SemaphoreType.DMA((n,))]


def _gather_weights(shards):
    n = len(shards)

    def body(*refs):
        plan = _GatherPlan(refs[:n], refs[n:2 * n], *refs[2 * n:])
        plan.send()
        plan.pass_on()
        plan.finish()

    return pl.pallas_call(
        body, name="gather_weights",
        in_specs=[ANY] * n, out_specs=[ANY] * n,
        out_shape=[jax.ShapeDtypeStruct((N_CHIPS,) + s.shape, s.dtype) for s in shards],
        scratch_shapes=_gather_sems(n),
    )(*shards)


class _SwapPlan:
    def __init__(self, src, dst, send, recv, send_half):
        x, y, c, _ = _place()
        sibling = (x, y, 1 - c)
        self.out_copies, self.in_copies = [], []
        for a in range(len(src)):
            rows = src[a].shape[-2] // 2
            lead = (slice(None),) * (len(src[a].shape) - 2)
            going = lead + (pl.ds(send_half(c) * rows, rows), slice(None))
            coming = lead + (pl.ds((1 - send_half(c)) * rows, rows), slice(None))
            self.out_copies.append(pltpu.make_async_remote_copy(
                src_ref=src[a].at[going], dst_ref=dst[a].at[going], send_sem=send.at[a], recv_sem=recv.at[a],
                device_id=sibling, device_id_type=MESH))
            self.in_copies.append(pltpu.make_async_remote_copy(
                src_ref=src[a].at[coming], dst_ref=dst[a].at[coming], send_sem=send.at[a], recv_sem=recv.at[a],
                device_id=sibling, device_id_type=MESH))

    def send(self):
        for cp in self.out_copies:
            cp.start()

    def finish(self):
        for cp in self.in_copies:
            cp.wait_recv()
        for cp in self.out_copies:
            cp.wait_send()


def _keep_own_half(c):
    return 1 - c


def _swap_halves(arrays, name, send_half, in_place):
    n = len(arrays)

    def body(*refs):
        src, dst = refs[:n], refs[n:2 * n]
        plan = _SwapPlan(dst if in_place else src, dst, *refs[2 * n:], send_half=send_half)
        plan.send()
        plan.finish()

    return pl.pallas_call(
        body, name=name,
        in_specs=[ANY] * n, out_specs=[ANY] * n,
        out_shape=[jax.ShapeDtypeStruct(a.shape, a.dtype) for a in arrays],
        scratch_shapes=[pltpu.SemaphoreType.DMA((n,)), pltpu.SemaphoreType.DMA((n,))],
        input_output_aliases={a: a for a in range(n)} if in_place else {},
    )(*arrays)


class _ScatterPlan:
    def __init__(self, src, dst, send, recv):
        x, y, c, others = _place()
        self.out_copies, self.in_copies = [], []
        for a in range(len(src)):
            rows = src[a].shape[1] // 2
            mine = pl.ds(c * rows, rows)
            for k, (ox, oy) in enumerate(others):
                sem = a * 3 + k
                self.out_copies.append(pltpu.make_async_remote_copy(
                    src_ref=src[a].at[2 * ox + oy, mine, :], dst_ref=dst[a].at[k, mine, :],
                    send_sem=send.at[sem], recv_sem=recv.at[sem], device_id=(ox, oy, c), device_id_type=MESH))
                self.in_copies.append(pltpu.make_async_remote_copy(
                    src_ref=dst[a].at[k, mine, :], dst_ref=dst[a].at[k, mine, :],
                    send_sem=send.at[sem], recv_sem=recv.at[sem], device_id=(ox, oy, c), device_id_type=MESH))

    def send(self):
        for cp in self.out_copies:
            cp.start()

    def finish(self):
        for cp in self.in_copies:
            cp.wait_recv()
        for cp in self.out_copies:
            cp.wait_send()


def _scatter_to_chips(arrays):
    n = len(arrays)

    def body(*refs):
        plan = _ScatterPlan(refs[:n], refs[n:2 * n], *refs[2 * n:])
        plan.send()
        plan.finish()

    return pl.pallas_call(
        body, name="scatter_to_chips",
        in_specs=[ANY] * n, out_specs=[ANY] * n,
        out_shape=[jax.ShapeDtypeStruct((3,) + a.shape[1:], a.dtype) for a in arrays],
        scratch_shapes=[pltpu.SemaphoreType.DMA((3 * n,)), pltpu.SemaphoreType.DMA((3 * n,))],
    )(*arrays)


def _allreduce_small(part):
    rows, width = part.shape

    def body(p_ref, out_ref, all_ref, send, recv):
        x, y, c, _ = _place()
        me = 4 * x + 2 * y + c
        all_ref[me] = p_ref[...]
        copies = []
        for k in range(1, N_DEV):
            px, py, pc = x ^ (k >> 2), y ^ ((k >> 1) & 1), c ^ (k & 1)
            copies.append(pltpu.make_async_remote_copy(
                src_ref=p_ref, dst_ref=all_ref.at[me], send_sem=send.at[k - 1], recv_sem=recv.at[k - 1],
                device_id=(px, py, pc), device_id_type=MESH))
        for cp in copies:
            cp.start()
        for cp in copies:
            cp.wait()
        total = all_ref[0]
        for k in range(1, N_DEV):
            total = total + all_ref[k]
        out_ref[...] = total

    return pl.pallas_call(
        body, name="allreduce_small",
        in_specs=[pl.BlockSpec(memory_space=pltpu.VMEM)],
        out_specs=pl.BlockSpec(memory_space=pltpu.VMEM),
        out_shape=jax.ShapeDtypeStruct((rows, width), F32),
        scratch_shapes=[pltpu.VMEM((N_DEV, rows, width), F32),
                        pltpu.SemaphoreType.DMA((N_DEV - 1,)), pltpu.SemaphoreType.DMA((N_DEV - 1,))],
    )(part)


def _row_tile(rows):
    t = min(rows, 256)
    while rows % t:
        t //= 2
    return t


def _core_and_chip():
    return jnp.stack([lax.axis_index("c"), 2 * lax.axis_index("x") + lax.axis_index("y")]).astype(jnp.int32)


def _sum_siblings(own, got, name):
    _, rows, cols = own.shape
    half = rows // 2
    t = _row_tile(half)
    nb = half // t

    def body(place_ref, own_ref, got_ref, mine_ref, out_ref):
        total = own_ref[...] + got_ref[...]
        out_ref[...] = total.astype(BF16)

        @pl.when(pl.program_id(1) == place_ref[1])
        def _():
            mine_ref[...] = total

    slab_blk = pl.BlockSpec((None, t, cols), lambda r, j, place: (j, place[0] * nb + r, 0))
    return pl.pallas_call(
        body, name=name,
        grid_spec=pltpu.PrefetchScalarGridSpec(
            num_scalar_prefetch=1, grid=(nb, N_CHIPS),
            in_specs=[slab_blk, slab_blk],
            out_specs=[pl.BlockSpec((t, cols), lambda r, j, place: (r, 0)), slab_blk]),
        out_shape=[jax.ShapeDtypeStruct((half, cols), F32), jax.ShapeDtypeStruct(own.shape, BF16)],
        compiler_params=_params(("arbitrary", "arbitrary")),
    )(_core_and_chip(), own, got)


def _sum_chips(mine, got, name):
    half, cols = mine.shape
    t = _row_tile(half)
    nb = half // t

    def body(place_ref, mine_ref, got_ref, out_ref):
        total = mine_ref[...]
        for k in range(3):
            total = total + got_ref[k].astype(F32)
        out_ref[...] = total

    return pl.pallas_call(
        body, name=name,
        grid_spec=pltpu.PrefetchScalarGridSpec(
            num_scalar_prefetch=1, grid=(nb,),
            in_specs=[pl.BlockSpec((t, cols), lambda r, place: (r, 0)),
                      pl.BlockSpec((3, t, cols), lambda r, place: (0, place[0] * nb + r, 0))],
            out_specs=pl.BlockSpec((t, cols), lambda r, place: (place[0] * nb + r, 0))),
        out_shape=jax.ShapeDtypeStruct((2 * half, cols), F32),
        compiler_params=_params(("arbitrary",)),
    )(_core_and_chip(), mine, got)


def _adamw(w, g, m, v, name, g_row0=0):
    rows, cols = w.shape
    t = _row_tile(rows)
    assert g_row0 % t == 0
    off = g_row0 // t

    def body(w_ref, g_ref, m_ref, v_ref, go_ref, d_ref, mo_ref, vo_ref):
        g_t = g_ref[...]
        m_new = ADAM_B1 * m_ref[...] + (1.0 - ADAM_B1) * g_t
        v_new = ADAM_B2 * v_ref[...] + (1.0 - ADAM_B2) * (g_t * g_t)
        m_hat = m_new / (1.0 - ADAM_B1 ** ADAM_STEP)
        v_hat = v_new / (1.0 - ADAM_B2 ** ADAM_STEP)
        go_ref[...] = g_t
        d_ref[...] = -ADAM_LR * (m_hat / (jnp.sqrt(v_hat) + ADAM_EPS) + ADAM_WD * w_ref[...])
        mo_ref[...] = m_new
        vo_ref[...] = v_new

    blk = pl.BlockSpec((t, cols), lambda r: (r, 0))
    return pl.pallas_call(
        body, name=name, grid=(rows // t,),
        in_specs=[blk, pl.BlockSpec((t, cols), lambda r: (r + off, 0)), blk, blk],
        out_specs=[blk] * 4,
        out_shape=[jax.ShapeDtypeStruct((rows, cols), F32)] * 4,
        compiler_params=_params(("arbitrary",)),
    )(w, g, m, v)


def _pack_small(d, vectors):
    rows = []
    for vec in vectors:
        flat = vec.reshape(-1)
        n_rows = -(-flat.shape[0] // d)
        rows.append(jnp.pad(flat, (0, n_rows * d - flat.shape[0])).reshape(n_rows, d))
    return jnp.concatenate(rows, axis=0)


def _unpack_small(packed, d, shapes):
    out, row = [], 0
    for shape in shapes:
        size = 1
        for s in shape:
            size *= s
        n_rows = -(-size // d)
        out.append(packed[row:row + n_rows].reshape(-1)[:size].reshape(shape))
        row += n_rows
    return out


def kernel(x, norm_pre, norm_post, pool_w_in, pool_w_group, pool_scale, pool_w_out, att_w_in, att_rel_bias, att_w_out, loss_target, m_norm_pre, m_norm_post, m_pool_w_in, m_pool_w_group, m_pool_scale, m_pool_w_out, m_att_w_in, m_att_rel_bias, m_att_w_out, v_norm_pre, v_norm_post, v_pool_w_in, v_pool_w_group, v_pool_scale, v_pool_w_out, v_att_w_in, v_att_rel_bias, v_att_w_out):
    _, s_len, d = x.shape
    gw = d // 2
    q = gw // N_CHIPS
    x2d = x.reshape(s_len, d)
    target = loss_target.reshape(s_len, d)

    def pack_g(p_group):
        return p_group.reshape(N_CHIPS * q, gw)

    wp_shard = jnp.concatenate([pool_w_in[0], pool_w_out[0]], axis=0).astype(BF16)
    wt_shard = jnp.concatenate([att_w_in[0], att_w_out[0]], axis=0).astype(BF16)
    wp, wg = _gather_weights([wp_shard, pack_g(pool_w_group).astype(BF16)])
    wg = wg.reshape(N_CHIPS, N_CHIPS, q, gw)

    x1, h0, mixed, z0, y0, wt = _l0_fwd(x2d, norm_pre[0:1], norm_post[0:1], pool_scale, wp, wg, wt_shard)
    h1, q_, k_, v_, z1 = _l1_inproj(x1, norm_pre[1:2], wt)
    bias = _bias_tiles(att_rel_bias[0])
    o, lse = _attn_fwd(q_, k_, v_, bias)

    g2, gated1, dy1, d_o, dz1, dgpost1, sq = _l1_out(o, z1, x1, target, norm_post[1:2], wt)
    dq, dk, dv, dbias = _attn_bwd(q_, k_, v_, bias, d_o, o, lse)
    dx1, du1, dgpre1 = _l1_in_bwd(dq, dk, dv, dz1, x1, g2, norm_pre[1:2], wt)
    d_rel = _rel_bias_grad(dbias)

    blk = lambda n: n
    zero = lambda n: 0
    slab_t = d + d // N_CHIPS
    whole_in = pl.BlockSpec((None, d, d), lambda n, kk: (n, 0, 0))
    gt = _wgrad(h1, du1, "wgrad_att_in", N_CHIPS, d, d, zero, blk, (N_CHIPS, slab_t, d), whole_in)
    gt = _wgrad(gated1, dy1, "wgrad_att_out", 1, d, d, zero, zero, (N_CHIPS, slab_t, d),
                pl.BlockSpec((N_CHIPS, d // N_CHIPS, d), lambda n, kk: (0, N_CHIPS, 0)), into=gt)

    grad_x, du0, gated0, dy0, dmm, dgpre0, dgpost0, dscale = _l0_bwd(
        dx1, x2d, y0, mixed, z0, norm_pre[0:1], norm_post[0:1], pool_scale, wp, wg)

    swap_rider = functools.partial(_SwapPlan, send_half=_keep_own_half)
    scatter_rider = lambda send: (_ScatterPlan, send, (3,) + send.shape[1:], (3, 3))
    gi, got_t = _wgrad(h0, du0, "wgrad_pool_in", N_CHIPS, d, d, zero, blk, (N_CHIPS, d, d), whole_in,
                       rider=(swap_rider, gt, gt.shape, (1, 1)))
    mine_t, send_t = _sum_siblings(gt, got_t, "sum_siblings_att")
    go, recv_t = _wgrad(gated0, dy0, "wgrad_pool_out", N_CHIPS, gw, d, blk, zero, (N_CHIPS, gw, d),
                        pl.BlockSpec((None, gw, d), lambda n, kk: (n, 0, 0)), rider=scatter_rider(send_t))
    (got_i,) = _swap_halves([gi], "swap_core_partials_pool_in", _keep_own_half, in_place=False)
    mine_i, send_i = _sum_siblings(gi, got_i, "sum_siblings_pool_in")
    gg, recv_i = _wgrad(mixed, dmm, "wgrad_pool_group", N_CHIPS, gw, gw, blk, blk, (N_CHIPS, N_CHIPS, q, gw),
                        pl.BlockSpec((N_CHIPS, None, q, gw), lambda n, kk: (0, n, 0, 0)), rider=scatter_rider(send_i))
    gg = gg.reshape(N_CHIPS, N_CHIPS * q, gw)
    got_o, got_g = _swap_halves([go, gg], "swap_core_partials_pool", _keep_own_half, in_place=False)
    mine_o, send_o = _sum_siblings(go, got_o, "sum_siblings_pool_out")
    mine_g, send_g = _sum_siblings(gg, got_g, "sum_siblings_group")
    recv_o, recv_g = _scatter_to_chips([send_o, send_g])
    red_t = _sum_chips(mine_t, recv_t, "sum_chips_att")
    red_i = _sum_chips(mine_i, recv_i, "sum_chips_pool_in")
    red_o = _sum_chips(mine_o, recv_o, "sum_chips_pool_out")
    red_g = _sum_chips(mine_g, recv_g, "sum_chips_group")
    grad_t, grad_i, grad_o, grad_g = _swap_halves([red_t, red_i, red_o, red_g], "swap_reduced_halves",
                                                  lambda c: c, in_place=True)

    small_shapes = [norm_pre.shape, norm_post.shape, pool_scale.shape, att_rel_bias.shape]
    part = _pack_small(d, [jnp.concatenate([dgpre0, dgpre1], axis=0), jnp.concatenate([dgpost0, dgpost1], axis=0),
                           dscale, d_rel, sq])
    total = _allreduce_small(part)
    loss = (0.5 / d) * jnp.sum(total[-1])
    w_small = _pack_small(d, [norm_pre, norm_post, pool_scale, att_rel_bias, jnp.zeros((d,), F32)])
    m_small = _pack_small(d, [m_norm_pre, m_norm_post, m_pool_scale, m_att_rel_bias, jnp.zeros((d,), F32)])
    v_small = _pack_small(d, [v_norm_pre, v_norm_post, v_pool_scale, v_att_rel_bias, jnp.ones((d,), F32)])
    small_out = [_unpack_small(a, d, small_shapes) for a in _adamw(w_small, total, m_small, v_small, "adamw_small")]

    big = {}
    for name, w, m, v, grad, row0 in [("pool_w_in", pool_w_in, m_pool_w_in, v_pool_w_in, grad_i, 0),
                                      ("pool_w_out", pool_w_out, m_pool_w_out, v_pool_w_out, grad_o, 0),
                                      ("att_w_in", att_w_in, m_att_w_in, v_att_w_in, grad_t, 0),
                                      ("att_w_out", att_w_out, m_att_w_out, v_att_w_out, grad_t, d)]:
        outs = _adamw(w[0], grad, m[0], v[0], "adamw_" + name, g_row0=row0)
        big[name] = [a.reshape(w.shape) for a in outs]
    outs = _adamw(pack_g(pool_w_group), grad_g, pack_g(m_pool_w_group), pack_g(v_pool_w_group), "adamw_pool_w_group")
    big["pool_w_group"] = [a.reshape(pool_w_group.shape) for a in outs]

    def leaf(kind):
        return (small_out[kind][0], small_out[kind][1], big["pool_w_in"][kind], big["pool_w_group"][kind],
                small_out[kind][2], big["pool_w_out"][kind], big["att_w_in"][kind], small_out[kind][3],
                big["att_w_out"][kind])

    return (loss, grad_x.reshape(x.shape), *leaf(0), *leaf(1), *leaf(2), *leaf(3))
```

```python
import functools

import jax
import jax.numpy as jnp
from jax import lax
from jax.experimental import pallas as pl
from jax.experimental.pallas import tpu as pltpu

F32 = jnp.float32
BF16 = jnp.bfloat16

RMS_EPS = 1e-6
CHUNK = 64
HEAD_DIM = 64
LEFT_CHUNKS = 8
PAD = LEFT_CHUNKS * CHUNK
MAX_REL = 256
POOL_WINDOWS = (2, 4, 8, 16)
HALO = 16
TS = 256
TS1 = 512
QB = 256
KB = QB + PAD
HEAD_PAIR = 2 * HEAD_DIM
NEG = -1e30
N_CHIPS = 4
N_DEV = 8

ADAM_LR = 0.001
ADAM_B1 = 0.9
ADAM_B2 = 0.999
ADAM_EPS = 1e-08
ADAM_WD = 0.01
ADAM_STEP = 10

VMEM_LIMIT = 56 * 1024 * 1024
MESH = pl.DeviceIdType.MESH
ANY = pl.BlockSpec(memory_space=pl.ANY)


def _dot(a, b):
    return jnp.dot(a, b, preferred_element_type=F32)


def _dot_nt(a, b):
    return lax.dot_general(a, b, (((1,), (1,)), ((), ())), preferred_element_type=F32)


def _dot_tn(a, b):
    return lax.dot_general(a, b, (((0,), (0,)), ((), ())), preferred_element_type=F32)


def _params(sem, limit=VMEM_LIMIT):
    return pltpu.CompilerParams(dimension_semantics=sem, vmem_limit_bytes=limit)


def _rms(x):
    return lax.rsqrt(jnp.mean(x * x, axis=-1, keepdims=True) + RMS_EPS)


def _rms_bwd(dyh, xh, r):
    return r * (dyh - xh * jnp.mean(dyh * xh, axis=-1, keepdims=True))


def _silu_parts(z):
    sig = jax.nn.sigmoid(z)
    return z * sig, sig * (1.0 + z * (1.0 - sig))


def _inv_count(tile, rows, w):
    t = tile * rows + lax.broadcasted_iota(jnp.int32, (rows, 1), 0)
    return 1.0 / jnp.minimum(t + 1, w).astype(F32)


def _start_weight_copies(copies):
    for c in copies:
        c.start()
    for c in copies:
        c.wait()


def _l0_fwd(x, g_pre, g_post, scale, wa, wg, next_shard):
    s_len, d = x.shape
    pw, gw = 2 * d, d // 2
    q = gw // N_CHIPS
    nt = s_len // TS
    assert nt >= 3

    def body(x_ref, gpre_ref, gpost_ref, sc_ref, wa_ref, wg_ref, shard_ref,
             x1_ref, h_ref, mixed_ref, z_ref, y_ref, next_ref,
             win_s, wout_s, wg_s, halo_s, sems, *gather_sems):
        i = pl.program_id(0)

        @pl.when(i == 0)
        def _():
            _GatherPlan([shard_ref], [next_ref], *gather_sems).send()
            copies = [pltpu.make_async_copy(wa_ref.at[:, pl.ds(0, d), :], win_s, sems.at[0]),
                      pltpu.make_async_copy(wa_ref.at[:, pl.ds(d, gw), :], wout_s, sems.at[1])]
            copies += [pltpu.make_async_copy(wg_ref.at[j], wg_s.at[:, pl.ds(j * q, q), :], sems.at[2 + j])
                       for j in range(N_CHIPS)]
            _start_weight_copies(copies)
            halo_s[...] = jnp.zeros_like(halo_s)

        @pl.when(i == nt // 2)
        def _():
            _GatherPlan([shard_ref], [next_ref], *gather_sems).pass_on()

        @pl.when(i == nt - 1)
        def _():
            _GatherPlan([shard_ref], [next_ref], *gather_sems).finish()

        xt = x_ref[...]
        h = ((xt * _rms(xt)) * gpre_ref[...]).astype(BF16)
        h_ref[...] = h
        a_blocks = [_dot(h, win_s[0]), _dot(h, win_s[1])]
        y = jnp.zeros((TS, d), F32)
        for g, w in enumerate(POOL_WINDOWS):
            a_g = a_blocks[g // 2][:, (g % 2) * gw:(g % 2 + 1) * gw]
            ext = jnp.concatenate([halo_s[g], a_g], axis=0)
            shift = 1
            while shift < w:
                ext = ext + pltpu.roll(ext, shift, 0)
                shift *= 2
            mixed = (ext[HALO:] * _inv_count(i, TS, w) - a_g).astype(BF16)
            halo_s[g] = a_g[TS - HALO:]
            mixed_ref[:, g * gw:(g + 1) * gw] = mixed
            z_g = _dot(h, win_s[2 + g // 2, :, (g % 2) * gw:(g % 2 + 1) * gw])
            z_ref[:, g * gw:(g + 1) * gw] = z_g
            ms = _dot(mixed, wg_s[g]) * sc_ref[:, g * gw:(g + 1) * gw]
            gated = (ms * _silu_parts(z_g)[0]).astype(BF16)
            y = y + _dot(gated, wout_s[g])
        y_ref[...] = y
        x1_ref[...] = xt + (y * _rms(y)) * gpost_ref[...]

    tile = lambda wdt: pl.BlockSpec((TS, wdt), lambda i: (i, 0))
    row = lambda wdt: pl.BlockSpec((1, wdt), lambda i: (0, 0))
    return pl.pallas_call(
        body, name="l0_fwd", grid=(nt,),
        in_specs=[tile(d), row(d), row(d), row(pw), ANY, ANY, ANY],
        out_specs=[tile(d), tile(d), tile(pw), tile(pw), tile(d), ANY],
        out_shape=[jax.ShapeDtypeStruct((s_len, d), F32), jax.ShapeDtypeStruct((s_len, d), BF16),
                   jax.ShapeDtypeStruct((s_len, pw), BF16), jax.ShapeDtypeStruct((s_len, pw), F32),
                   jax.ShapeDtypeStruct((s_len, d), F32),
                   jax.ShapeDtypeStruct((N_CHIPS,) + next_shard.shape, next_shard.dtype)],
        scratch_shapes=[pltpu.VMEM((N_CHIPS, d, d), BF16), pltpu.VMEM((N_CHIPS, gw, d), BF16),
                        pltpu.VMEM((N_CHIPS, gw, gw), BF16), pltpu.VMEM((N_CHIPS, HALO, gw), F32),
                        pltpu.SemaphoreType.DMA((2 + N_CHIPS,))] + _gather_sems(1),
        compiler_params=_params(("arbitrary",)),
    )(x, g_pre, g_post, scale, wa, wg, next_shard)


def _l0_bwd(dx1, x, y, mixed, z, g_pre, g_post, scale, wa, wg):
    s_len, d = x.shape
    pw, gw = 2 * d, d // 2
    q = gw // N_CHIPS
    nt = s_len // TS

    def body(dx1_ref, x_ref, y_ref, mixed_ref, z_ref, gpre_ref, gpost_ref, sc_ref, wa_ref, wg_ref,
             gx_ref, du_ref, gated_ref, dy_ref, dmm_ref, dgpre_ref, dgpost_ref, dsc_ref,
             win_s, wout_s, wg_s, halo_s, sems):
        i = pl.program_id(0)
        tile = nt - 1 - i

        @pl.when(i == 0)
        def _():
            copies = [pltpu.make_async_copy(wa_ref.at[:, pl.ds(0, d), :], win_s, sems.at[0]),
                      pltpu.make_async_copy(wa_ref.at[:, pl.ds(d, gw), :], wout_s, sems.at[1])]
            copies += [pltpu.make_async_copy(wg_ref.at[j], wg_s.at[:, pl.ds(j * q, q), :], sems.at[2 + j])
                       for j in range(N_CHIPS)]
            _start_weight_copies(copies)
            halo_s[...] = jnp.zeros_like(halo_s)
            dgpre_ref[...] = jnp.zeros_like(dgpre_ref)
            dgpost_ref[...] = jnp.zeros_like(dgpost_ref)
            dsc_ref[...] = jnp.zeros_like(dsc_ref)

        g_in = dx1_ref[...]
        yt = y_ref[...]
        r_y = _rms(yt)
        yh = yt * r_y
        dgpost_ref[...] += jnp.sum(g_in * yh, axis=0, keepdims=True)
        dy = _rms_bwd(g_in * gpost_ref[...], yh, r_y).astype(BF16)
        dy_ref[...] = dy
        dh = jnp.zeros((TS, d), F32)

        for g, w in enumerate(POOL_WINDOWS):
            cols = slice(g * gw, (g + 1) * gw)
            dgated = _dot_nt(dy, wout_s[g])
            mm = _dot(mixed_ref[:, cols], wg_s[g])
            sc = sc_ref[:, cols]
            ms = mm * sc
            z_g = z_ref[:, cols]
            sz, dsz = _silu_parts(z_g)
            gated_ref[:, cols] = (ms * sz).astype(BF16)
            dms = dgated * sz
            dz = (dgated * ms * dsz).astype(BF16)
            dsc_ref[:, cols] += jnp.sum(dms * mm, axis=0, keepdims=True)
            dmm = (dms * sc).astype(BF16)
            dmm_ref[:, cols] = dmm
            dmixed = _dot_nt(dmm, wg_s[g])
            e = dmixed * _inv_count(tile, TS, w)
            ext = jnp.concatenate([e, halo_s[g]], axis=0)
            shift = 1
            while shift < w:
                ext = ext + pltpu.roll(ext, TS + HALO - shift, 0)
                shift *= 2
            da = (ext[:TS] - dmixed).astype(BF16)
            halo_s[g] = e[:HALO]
            du_ref[:, cols] = da
            du_ref[:, pw + g * gw:pw + (g + 1) * gw] = dz
            wa_blk = win_s[g // 2, :, (g % 2) * gw:(g % 2 + 1) * gw]
            wz_blk = win_s[2 + g // 2, :, (g % 2) * gw:(g % 2 + 1) * gw]
            dh = dh + _dot_nt(da, wa_blk) + _dot_nt(dz, wz_blk)
        xt = x_ref[...]
        r_x = _rms(xt)
        xh = xt * r_x
        dgpre_ref[...] += jnp.sum(dh * xh, axis=0, keepdims=True)
        gx_ref[...] = g_in + _rms_bwd(dh * gpre_ref[...], xh, r_x)

    tile_spec = lambda wdt: pl.BlockSpec((TS, wdt), lambda i: (nt - 1 - i, 0))
    row = lambda wdt: pl.BlockSpec((1, wdt), lambda i: (0, 0))
    return pl.pallas_call(
        body, name="l0_bwd", grid=(nt,),
        in_specs=[tile_spec(d), tile_spec(d), tile_spec(d), tile_spec(pw), tile_spec(pw),
                  row(d), row(d), row(pw), ANY, ANY],
        out_specs=[tile_spec(d), tile_spec(2 * pw), tile_spec(pw), tile_spec(d), tile_spec(pw),
                   row(d), row(d), row(pw)],
        out_shape=[jax.ShapeDtypeStruct((s_len, d), F32), jax.ShapeDtypeStruct((s_len, 2 * pw), BF16),
                   jax.ShapeDtypeStruct((s_len, pw), BF16), jax.ShapeDtypeStruct((s_len, d), BF16),
                   jax.ShapeDtypeStruct((s_len, pw), BF16),
                   jax.ShapeDtypeStruct((1, d), F32), jax.ShapeDtypeStruct((1, d), F32),
                   jax.ShapeDtypeStruct((1, pw), F32)],
        scratch_shapes=[pltpu.VMEM((N_CHIPS, d, d), BF16), pltpu.VMEM((N_CHIPS, gw, d), BF16),
                        pltpu.VMEM((N_CHIPS, gw, gw), BF16), pltpu.VMEM((N_CHIPS, HALO, gw), F32),
                        pltpu.SemaphoreType.DMA((2 + N_CHIPS,))],
        compiler_params=_params(("arbitrary",)),
    )(dx1, x, y, mixed, z, g_pre, g_post, scale, wa, wg)


def _l1_inproj(x1, g_pre, wa):
    s_len, d = x1.shape
    nt = s_len // TS1
    npad = PAD // TS1

    def body(x_ref, gpre_ref, wa_ref, h_ref, q_ref, k_ref, v_ref, z_ref, win_s, sem):
        i = pl.program_id(0)

        @pl.when(i == 0)
        def _():
            _start_weight_copies([pltpu.make_async_copy(wa_ref.at[:, pl.ds(0, d), :], win_s, sem.at[0])])

        @pl.when(i < npad)
        def _():
            k_ref[...] = jnp.zeros_like(k_ref)
            v_ref[...] = jnp.zeros_like(v_ref)

        @pl.when(i >= npad)
        def _():
            xt = x_ref[...]
            h = ((xt * _rms(xt)) * gpre_ref[...]).astype(BF16)
            h_ref[...] = h
            q_ref[...] = _dot(h, win_s[0]).astype(BF16)
            k_ref[...] = _dot(h, win_s[1]).astype(BF16)
            v_ref[...] = _dot(h, win_s[2]).astype(BF16)
            z_ref[...] = _dot(h, win_s[3])

    tile = pl.BlockSpec((TS1, d), lambda i: (jnp.maximum(i - npad, 0), 0))
    padded = pl.BlockSpec((TS1, d), lambda i: (i, 0))
    return pl.pallas_call(
        body, name="l1_inproj", grid=(nt + npad,),
        in_specs=[tile, pl.BlockSpec((1, d), lambda i: (0, 0)), ANY],
        out_specs=[tile, tile, padded, padded, tile],
        out_shape=[jax.ShapeDtypeStruct((s_len, d), BF16), jax.ShapeDtypeStruct((s_len, d), BF16),
                   jax.ShapeDtypeStruct((PAD + s_len, d), BF16), jax.ShapeDtypeStruct((PAD + s_len, d), BF16),
                   jax.ShapeDtypeStruct((s_len, d), F32)],
        scratch_shapes=[pltpu.VMEM((N_CHIPS, d, d), BF16), pltpu.SemaphoreType.DMA((1,))],
        compiler_params=_params(("arbitrary",)),
    )(x1, g_pre, wa)


SKEW = QB + KB


def _bias_tiles(rel_bias):
    nh = rel_bias.shape[0]
    assert QB == MAX_REL
    by_column = jnp.concatenate([jnp.broadcast_to(rel_bias[:, 2 * MAX_REL:], (nh, PAD + 1)),
                                 jnp.flip(rel_bias[:, 1:2 * MAX_REL], axis=1)], axis=1).astype(F32)

    def body(col_ref, out_ref):
        rows = jnp.broadcast_to(col_ref[pl.ds(pl.program_id(0), 1), :], (QB, SKEW))
        tile = pltpu.roll(rows, SKEW - QB, 1, stride=1, stride_axis=0)[:, :KB]
        i = lax.broadcasted_iota(jnp.int32, (QB, KB), 0)
        j = lax.broadcasted_iota(jnp.int32, (QB, KB), 1)
        first = (i // CHUNK) * CHUNK
        out_ref[...] = jnp.where((j >= first) & (j < first + PAD + CHUNK), tile, NEG)

    return pl.pallas_call(
        body, name="bias_tiles", grid=(nh,),
        in_specs=[pl.BlockSpec((nh, SKEW), lambda h: (0, 0))],
        out_specs=pl.BlockSpec((None, QB, KB), lambda h: (h, 0, 0)),
        out_shape=jax.ShapeDtypeStruct((nh, QB, KB), F32),
        compiler_params=_params(("arbitrary",)),
    )(by_column)


ROWS = 16


BLOCKS_PER_STEP = 8


def _row_blocks(rows=ROWS):
    return [pl.ds(r * rows, rows) for r in range(QB // rows)]


def _attn_fwd(q, k, v, bias):
    s_len, d = q.shape
    nhp = d // HEAD_PAIR
    per_step = min(BLOCKS_PER_STEP, s_len // QB)
    nq = s_len // (QB * per_step)

    def body(q_ref, k_ref, v_ref, b_ref, o_ref, lse_ref, s_s, p_s, l_s):
        head_of_lane = lax.broadcasted_iota(jnp.int32, (1, HEAD_PAIR), 1) // HEAD_DIM

        units = [(u, hh) for u in range(per_step) for hh in range(2)]

        def band(u):
            return pl.ds(pl.multiple_of((pl.program_id(1) * per_step + u) * QB, QB), KB)

        def issue_scores(n):
            u, hh = units[n]
            qt = q_ref[pl.ds(u * QB, QB), :] * (HEAD_DIM ** -0.5)
            s_s[n % 2] = _dot_nt(jnp.where(head_of_lane == hh, qt, jnp.zeros_like(qt)), k_ref[band(u), :])

        issue_scores(0)
        outs, lses = [], []
        for n, (u, hh) in enumerate(units):
            if n + 1 < len(units):
                issue_scores(n + 1)
            first_key = (pl.program_id(1) * per_step + u) * QB
            colvalid = (first_key + lax.broadcasted_iota(jnp.int32, (1, KB), 1)) >= PAD
            for rows in _row_blocks():
                s = jnp.where(colvalid, s_s[n % 2, rows, :] + b_ref[hh, rows, :], NEG)
                m = jnp.max(s, axis=-1, keepdims=True)
                e = jnp.exp(s - m)
                l = jnp.sum(e, axis=-1, keepdims=True)
                p_s[n % 2, rows, :] = (e * (1.0 / l)).astype(BF16)
                l_s[n % 2, rows, :] = m + jnp.log(l)
            outs.append(_dot(p_s[n % 2], v_ref[band(u), :]))
            lses.append(l_s[n % 2])
            if hh == 1:
                here = pl.ds(u * QB, QB)
                o_ref[here, :] = jnp.where(head_of_lane == 0, outs[-2], outs[-1])
                lse_ref[here, :] = jnp.where(head_of_lane == 0, lses[-2], lses[-1])

    blk = pl.BlockSpec((per_step * QB, HEAD_PAIR), lambda hp, b: (b, hp))
    whole = pl.BlockSpec((PAD + s_len, HEAD_PAIR), lambda hp, b: (0, hp))
    return pl.pallas_call(
        body, name="attn_fwd", grid=(nhp, nq),
        in_specs=[blk, whole, whole, pl.BlockSpec((2, QB, KB), lambda hp, b: (hp, 0, 0))],
        out_specs=[blk, blk],
        out_shape=[jax.ShapeDtypeStruct((s_len, d), F32), jax.ShapeDtypeStruct((s_len, d), F32)],
        scratch_shapes=[pltpu.VMEM((2, QB, KB), F32), pltpu.VMEM((2, QB, KB), BF16), pltpu.VMEM((2, QB, 1), F32)],
        compiler_params=_params(("arbitrary", "arbitrary")),
    )(q, k, v, bias)


def _attn_bwd(q, k, v, bias, d_o, o, lse):
    s_len, d = q.shape
    nhp = d // HEAD_PAIR
    per_step = min(BLOCKS_PER_STEP, s_len // QB)
    nq = s_len // (QB * per_step)
    qk_scale = HEAD_DIM ** -0.5

    def body(q_ref, k_ref, v_ref, b_ref, do_ref, o_ref, lse_ref, dq_ref, dk_out_ref, dv_out_ref, db_ref,
             s_s, dp_s, p_s, ds_s, lse_s, delta_s, dk_ref, dv_ref):
        @pl.when(pl.program_id(1) == 0)
        def _():
            dk_ref[...] = jnp.zeros_like(dk_ref)
            dv_ref[...] = jnp.zeros_like(dv_ref)
            db_ref[...] = jnp.zeros_like(db_ref)

        head_of_lane = lax.broadcasted_iota(jnp.int32, (1, HEAD_PAIR), 1) // HEAD_DIM

        units = [(u, hh) for u in range(per_step) for hh in range(2)]

        def band(u):
            return pl.ds(pl.multiple_of((pl.program_id(1) * per_step + u) * QB, QB), KB)

        def masked(ref, u, hh, factor=None):
            x = ref[pl.ds(u * QB, QB), :]
            x = x if factor is None else x * factor
            return jnp.where(head_of_lane == hh, x, jnp.zeros_like(x))

        def issue_tiles(n):
            u, hh = units[n]
            here = pl.ds(u * QB, QB)
            do_o = do_ref[here, :].astype(F32) * o_ref[here, :]
            delta_s[n % 2] = jnp.sum(jnp.where(head_of_lane == hh, do_o, 0.0), axis=-1, keepdims=True)
            lse_s[n % 2] = lse_ref[here, hh * HEAD_DIM:hh * HEAD_DIM + 1]
            s_s[n % 2] = _dot_nt(masked(q_ref, u, hh, qk_scale), k_ref[band(u), :])
            dp_s[n % 2] = _dot_nt(masked(do_ref, u, hh), v_ref[band(u), :])

        issue_tiles(0)
        dq_heads, dk_band, dv_band = [], None, None
        for n, (u, hh) in enumerate(units):
            if n + 1 < len(units):
                issue_tiles(n + 1)
            first_key = (pl.program_id(1) * per_step + u) * QB
            colvalid = (first_key + lax.broadcasted_iota(jnp.int32, (1, KB), 1)) >= PAD
            for rows in _row_blocks():
                t = jnp.where(colvalid, s_s[n % 2, rows, :] + b_ref[hh, rows, :] - lse_s[n % 2, rows, :], NEG)
                p = jnp.exp(t)
                ds = p * (dp_s[n % 2, rows, :] - delta_s[n % 2, rows, :])
                db_ref[hh, rows, :] += ds
                p_s[n % 2, rows, :] = p.astype(BF16)
                ds_s[n % 2, rows, :] = ds.astype(BF16)
            q_m = masked(q_ref, u, hh, qk_scale)
            dv_unit = _dot_tn(p_s[n % 2], masked(do_ref, u, hh))
            dq_heads.append(_dot(ds_s[n % 2], k_ref[band(u), :]) * qk_scale)
            dk_unit = _dot_tn(ds_s[n % 2], q_m)
            if hh == 0:
                dk_band, dv_band = dk_unit, dv_unit
            else:
                dq_ref[pl.ds(u * QB, QB), :] = jnp.where(head_of_lane == 0, dq_heads[-2], dq_heads[-1]).astype(BF16)
                dk_ref[band(u), :] += dk_band + dk_unit
                dv_ref[band(u), :] += dv_band + dv_unit

        @pl.when(pl.program_id(1) == nq - 1)
        def _():
            dk_out_ref[...] = dk_ref[...].astype(BF16)
            dv_out_ref[...] = dv_ref[...].astype(BF16)

    blk = pl.BlockSpec((per_step * QB, HEAD_PAIR), lambda hp, b: (b, hp))
    whole = pl.BlockSpec((PAD + s_len, HEAD_PAIR), lambda hp, b: (0, hp))
    btile = pl.BlockSpec((2, QB, KB), lambda hp, b: (hp, 0, 0))
    return pl.pallas_call(
        body, name="attn_bwd", grid=(nhp, nq),
        in_specs=[blk, whole, whole, btile, blk, blk, blk],
        out_specs=[blk, whole, whole, btile],
        out_shape=[jax.ShapeDtypeStruct((s_len, d), BF16),
                   jax.ShapeDtypeStruct((PAD + s_len, d), BF16), jax.ShapeDtypeStruct((PAD + s_len, d), BF16),
                   jax.ShapeDtypeStruct(bias.shape, F32)],
        scratch_shapes=[pltpu.VMEM((2, QB, KB), F32), pltpu.VMEM((2, QB, KB), F32),
                        pltpu.VMEM((2, QB, KB), BF16), pltpu.VMEM((2, QB, KB), BF16),
                        pltpu.VMEM((2, QB, 1), F32), pltpu.VMEM((2, QB, 1), F32),
                        pltpu.VMEM((PAD + s_len, HEAD_PAIR), F32), pltpu.VMEM((PAD + s_len, HEAD_PAIR), F32)],
        compiler_params=_params(("arbitrary", "arbitrary")),
    )(q, k, v, bias, d_o, o, lse)


def _rel_bias_grad(db):
    nh = db.shape[0]
    assert QB == MAX_REL

    def body(db_ref, out_ref):
        i0 = lax.broadcasted_iota(jnp.int32, (QB, QB), 0)
        i1 = lax.broadcasted_iota(jnp.int32, (QB, QB), 1)
        exchange = jnp.where(i0 + i1 == QB - 1, 1.0, 0.0).astype(BF16)
        rest = db_ref[...]
        flipped = jnp.zeros((QB, KB), F32)
        for _ in range(3):
            piece = rest.astype(BF16)
            flipped = flipped + _dot(exchange, piece)
            rest = rest - piece.astype(F32)
        m = jnp.concatenate([flipped, jnp.zeros((QB, SKEW - KB), F32)], axis=1)
        diag = jnp.sum(pltpu.roll(m, 1, 1, stride=1, stride_axis=0), axis=0, keepdims=True)
        c = lax.broadcasted_iota(jnp.int32, (1, SKEW), 1)
        clipped = jnp.sum(jnp.where(c <= PAD, diag, 0.0), axis=1, keepdims=True)
        out_ref[...] = jnp.where(c == 0, clipped, diag)

    diag = pl.pallas_call(
        body, name="bias_diagonals", grid=(nh,),
        in_specs=[pl.BlockSpec((None, QB, KB), lambda h: (h, 0, 0))],
        out_specs=pl.BlockSpec((None, 1, SKEW), lambda h: (h, 0, 0)),
        out_shape=jax.ShapeDtypeStruct((nh, 1, SKEW), F32),
        compiler_params=_params(("arbitrary",)),
    )(db)[:, 0]
    return jnp.concatenate([jnp.zeros((nh, 1), F32), jnp.flip(diag[:, PAD + 1:], axis=1), diag[:, :1]], axis=1)


def _l1_out(o, z, x1, target, g_post, wa):
    s_len, d = o.shape
    nt = s_len // TS
    slab = d // N_CHIPS

    def body(o_ref, z_ref, x1_ref, t_ref, gpost_ref, wa_ref,
             g2_ref, gated_ref, dy_ref, do_ref, dz_ref, dgpost_ref, sq_ref, wout_s, sems):
        i = pl.program_id(0)

        @pl.when(i == 0)
        def _():
            _start_weight_copies([
                pltpu.make_async_copy(wa_ref.at[j, pl.ds(d, slab), :],
                                      wout_s.at[pl.ds(j * slab, slab), :], sems.at[j])
                for j in range(N_CHIPS)])
            dgpost_ref[...] = jnp.zeros_like(dgpost_ref)
            sq_ref[...] = jnp.zeros_like(sq_ref)

        ot = o_ref[...]
        sz, dsz = _silu_parts(z_ref[...])
        gated = (ot * sz).astype(BF16)
        gated_ref[...] = gated
        y = _dot(gated, wout_s[...])
        r_y = _rms(y)
        yh = y * r_y
        err = x1_ref[...] + yh * gpost_ref[...] - t_ref[...]
        sq_ref[...] += jnp.sum(err * err, axis=0, keepdims=True)
        g2 = err * (1.0 / d)
        g2_ref[...] = g2
        dgpost_ref[...] += jnp.sum(g2 * yh, axis=0, keepdims=True)
        dy = _rms_bwd(g2 * gpost_ref[...], yh, r_y).astype(BF16)
        dy_ref[...] = dy
        dgated = _dot_nt(dy, wout_s[...])
        do_ref[...] = (dgated * sz).astype(BF16)
        dz_ref[...] = (dgated * ot * dsz).astype(BF16)

    tile = pl.BlockSpec((TS, d), lambda i: (i, 0))
    row = pl.BlockSpec((1, d), lambda i: (0, 0))
    return pl.pallas_call(
        body, name="l1_out", grid=(nt,),
        in_specs=[tile, tile, tile, tile, row, ANY],
        out_specs=[tile, tile, tile, tile, tile, row, row],
        out_shape=[jax.ShapeDtypeStruct((s_len, d), F32)] + [jax.ShapeDtypeStruct((s_len, d), BF16)] * 4
                  + [jax.ShapeDtypeStruct((1, d), F32)] * 2,
        scratch_shapes=[pltpu.VMEM((d, d), BF16), pltpu.SemaphoreType.DMA((N_CHIPS,))],
        compiler_params=_params(("arbitrary",)),
    )(o, z, x1, target, g_post, wa)


def _l1_in_bwd(dq, dk, dv, dz, x1, g2, g_pre, wa):
    s_len, d = x1.shape
    nt = s_len // TS1
    npad = PAD // TS1

    def body(dq_ref, dk_ref, dv_ref, dz_ref, x_ref, g2_ref, gpre_ref, wa_ref,
             dx_ref, du_ref, dgpre_ref, win_s, sem):
        i = pl.program_id(0)

        @pl.when(i == 0)
        def _():
            _start_weight_copies([pltpu.make_async_copy(wa_ref.at[:, pl.ds(0, d), :], win_s, sem.at[0])])
            dgpre_ref[...] = jnp.zeros_like(dgpre_ref)

        parts = [dq_ref[...], dk_ref[...], dv_ref[...], dz_ref[...]]
        dh = jnp.zeros((TS1, d), F32)
        for j, part in enumerate(parts):
            du_ref[:, j * d:(j + 1) * d] = part
            dh = dh + _dot_nt(part, win_s[j])
        xt = x_ref[...]
        r_x = _rms(xt)
        xh = xt * r_x
        dgpre_ref[...] += jnp.sum(dh * xh, axis=0, keepdims=True)
        dx_ref[...] = g2_ref[...] + _rms_bwd(dh * gpre_ref[...], xh, r_x)

    tile = pl.BlockSpec((TS1, d), lambda i: (i, 0))
    padded = pl.BlockSpec((TS1, d), lambda i: (i + npad, 0))
    row = pl.BlockSpec((1, d), lambda i: (0, 0))
    return pl.pallas_call(
        body, name="l1_in_bwd", grid=(nt,),
        in_specs=[tile, padded, padded, tile, tile, tile, row, ANY],
        out_specs=[tile, pl.BlockSpec((TS1, 4 * d), lambda i: (i, 0)), row],
        out_shape=[jax.ShapeDtypeStruct((s_len, d), F32), jax.ShapeDtypeStruct((s_len, 4 * d), BF16),
                   jax.ShapeDtypeStruct((1, d), F32)],
        scratch_shapes=[pltpu.VMEM((N_CHIPS, d, d), BF16), pltpu.SemaphoreType.DMA((1,))],
        compiler_params=_params(("arbitrary",)),
    )(dq, dk, dv, dz, x1, g2, g_pre, wa)


def _wgrad(a, b, name, n_out, a_width, b_width, a_block, b_block, out_shape, out_spec, into=None, riders=()):
    s_len = a.shape[0]
    tk = min(2048, s_len)
    nk = s_len // tk
    n_ride = len(riders)
    n_in = 2 + (into is not None) + n_ride

    def body(*refs):
        a_ref, b_ref = refs[:2]
        out_ref = refs[n_in]

        def plans():
            return [rider[0]([refs[n_in - n_ride + r]], [refs[n_in + 1 + r]],
                             *refs[n_in + 1 + n_ride + 2 * r:n_in + 3 + n_ride + 2 * r])
                    for r, rider in enumerate(riders)]

        if riders:
            @pl.when((pl.program_id(0) == 0) & (pl.program_id(1) == 0))
            def _():
                for plan in plans():
                    plan.send()

        @pl.when(pl.program_id(1) == 0)
        def _():
            out_ref[...] = jnp.zeros_like(out_ref)

        out_ref[...] += _dot_tn(a_ref[...], b_ref[...]).reshape(out_ref.shape)

        if riders:
            @pl.when((pl.program_id(0) == n_out - 1) & (pl.program_id(1) == nk - 1))
            def _():
                for plan in plans():
                    plan.finish()

    operands = [a, b] + ([into] if into is not None else []) + [rider[1] for rider in riders]
    results = pl.pallas_call(
        body, name=name, grid=(n_out, nk),
        in_specs=[pl.BlockSpec((tk, a_width), lambda n, kk: (kk, a_block(n))),
                  pl.BlockSpec((tk, b_width), lambda n, kk: (kk, b_block(n)))] + [ANY] * (n_in - 2),
        out_specs=[out_spec] + [ANY] * n_ride,
        out_shape=[jax.ShapeDtypeStruct(out_shape, F32)]
                  + [jax.ShapeDtypeStruct(rider[2], rider[1].dtype) for rider in riders],
        scratch_shapes=[pltpu.SemaphoreType.DMA((n,)) for rider in riders for n in rider[3]],
        input_output_aliases={2: 0} if into is not None else {},
        compiler_params=_params(("arbitrary", "arbitrary")),
    )(*operands)
    return results if riders else results[0]


def _place():
    x, y, c = lax.axis_index("x"), lax.axis_index("y"), lax.axis_index("c")
    others = [(1 - x, y), (x, 1 - y), (1 - x, 1 - y)]
    return x, y, c, others


class _GatherPlan:
    def __init__(self, src, dst, send, recv, fwd_send, fwd_recv, local):
        x, y, c, others = _place()
        me = 2 * x + y
        sibling = (x, y, 1 - c)

        def half(ref, cc):
            rows = ref.shape[0] // 2
            return ref.at[pl.ds(cc * rows, rows), :]

        self.mine = [pltpu.make_async_copy(src[a], dst[a].at[me], local.at[a]) for a in range(len(src))]
        self.first, self.passed, self.arrive, self.arrive_fwd = [], [], [], []
        for a in range(len(src)):
            for k, (ox, oy) in enumerate(others):
                sem = a * 3 + k
                self.first.append(pltpu.make_async_remote_copy(
                    src_ref=half(src[a], c), dst_ref=half(dst[a].at[me], c),
                    send_sem=send.at[sem], recv_sem=recv.at[sem], device_id=(ox, oy, c), device_id_type=MESH))
                theirs = half(dst[a].at[2 * ox + oy], c)
                self.arrive.append(pltpu.make_async_remote_copy(
                    src_ref=theirs, dst_ref=theirs, send_sem=send.at[sem], recv_sem=recv.at[sem],
                    device_id=(ox, oy, c), device_id_type=MESH))
                self.passed.append(pltpu.make_async_remote_copy(
                    src_ref=theirs, dst_ref=theirs, send_sem=fwd_send.at[sem], recv_sem=fwd_recv.at[sem],
                    device_id=sibling, device_id_type=MESH))
                other_half = half(dst[a].at[2 * ox + oy], 1 - c)
                self.arrive_fwd.append(pltpu.make_async_remote_copy(
                    src_ref=other_half, dst_ref=other_half, send_sem=fwd_send.at[sem], recv_sem=fwd_recv.at[sem],
                    device_id=sibling, device_id_type=MESH))

    def send(self):
        for cp in self.mine + self.first:
            cp.start()

    def pass_on(self):
        for got, onward in zip(self.arrive, self.passed):
            got.wait_recv()
            onward.start()

    def finish(self):
        for got in self.arrive_fwd:
            got.wait_recv()
        for cp in self.first + self.passed:
            cp.wait_send()
        for cp in self.mine:
            cp.wait()


def _gather_sems(n):
    return [pltpu.SemaphoreType.DMA((3 * n,))] * 4 + [pltpu.SemaphoreType.DMA((n,))]


def _gather_weights(shards):
    n = len(shards)

    def body(*refs):
        plan = _GatherPlan(refs[:n], refs[n:2 * n], *refs[2 * n:])
        plan.send()
        plan.pass_on()
        plan.finish()

    return pl.pallas_call(
        body, name="gather_weights",
        in_specs=[ANY] * n, out_specs=[ANY] * n,
        out_shape=[jax.ShapeDtypeStruct((N_CHIPS,) + s.shape, s.dtype) for s in shards],
        scratch_shapes=_gather_sems(n),
    )(*shards)


class _SwapPlan:
    def __init__(self, src, dst, send, recv, send_half):
        x, y, c, _ = _place()
        sibling = (x, y, 1 - c)
        self.out_copies, self.in_copies = [], []
        for a in range(len(src)):
            rows = src[a].shape[-2] // 2
            lead = (slice(None),) * (len(src[a].shape) - 2)
            going = lead + (pl.ds(send_half(c) * rows, rows), slice(None))
            coming = lead + (pl.ds((1 - send_half(c)) * rows, rows), slice(None))
            self.out_copies.append(pltpu.make_async_remote_copy(
                src_ref=src[a].at[going], dst_ref=dst[a].at[going], send_sem=send.at[a], recv_sem=recv.at[a],
                device_id=sibling, device_id_type=MESH))
            self.in_copies.append(pltpu.make_async_remote_copy(
                src_ref=src[a].at[coming], dst_ref=dst[a].at[coming], send_sem=send.at[a], recv_sem=recv.at[a],
                device_id=sibling, device_id_type=MESH))

    def send(self):
        for cp in self.out_copies:
            cp.start()

    def finish(self):
        for cp in self.in_copies:
            cp.wait_recv()
        for cp in self.out_copies:
            cp.wait_send()


def _keep_own_half(c):
    return 1 - c


def _swap_halves(arrays, name, send_half, in_place):
    n = len(arrays)

    def body(*refs):
        src, dst = refs[:n], refs[n:2 * n]
        plan = _SwapPlan(dst if in_place else src, dst, *refs[2 * n:], send_half=send_half)
        plan.send()
        plan.finish()

    return pl.pallas_call(
        body, name=name,
        in_specs=[ANY] * n, out_specs=[ANY] * n,
        out_shape=[jax.ShapeDtypeStruct(a.shape, a.dtype) for a in arrays],
        scratch_shapes=[pltpu.SemaphoreType.DMA((n,)), pltpu.SemaphoreType.DMA((n,))],
        input_output_aliases={a: a for a in range(n)} if in_place else {},
    )(*arrays)


class _ScatterPlan:
    def __init__(self, src, dst, send, recv):
        x, y, c, others = _place()
        self.out_copies, self.in_copies = [], []
        for a in range(len(src)):
            rows = src[a].shape[1] // 2
            mine = pl.ds(c * rows, rows)
            for k, (ox, oy) in enumerate(others):
                sem = a * 3 + k
                self.out_copies.append(pltpu.make_async_remote_copy(
                    src_ref=src[a].at[2 * ox + oy, mine, :], dst_ref=dst[a].at[k, mine, :],
                    send_sem=send.at[sem], recv_sem=recv.at[sem], device_id=(ox, oy, c), device_id_type=MESH))
                self.in_copies.append(pltpu.make_async_remote_copy(
                    src_ref=dst[a].at[k, mine, :], dst_ref=dst[a].at[k, mine, :],
                    send_sem=send.at[sem], recv_sem=recv.at[sem], device_id=(ox, oy, c), device_id_type=MESH))

    def send(self):
        for cp in self.out_copies:
            cp.start()

    def finish(self):
        for cp in self.in_copies:
            cp.wait_recv()
        for cp in self.out_copies:
            cp.wait_send()


def _scatter_to_chips(arrays):
    n = len(arrays)

    def body(*refs):
        plan = _ScatterPlan(refs[:n], refs[n:2 * n], *refs[2 * n:])
        plan.send()
        plan.finish()

    return pl.pallas_call(
        body, name="scatter_to_chips",
        in_specs=[ANY] * n, out_specs=[ANY] * n,
        out_shape=[jax.ShapeDtypeStruct((3,) + a.shape[1:], a.dtype) for a in arrays],
        scratch_shapes=[pltpu.SemaphoreType.DMA((3 * n,)), pltpu.SemaphoreType.DMA((3 * n,))],
    )(*arrays)


def _allreduce_small(part):
    rows, width = part.shape

    def body(p_ref, out_ref, all_ref, send, recv):
        x, y, c, _ = _place()
        me = 4 * x + 2 * y + c
        all_ref[me] = p_ref[...]
        copies = []
        for k in range(1, N_DEV):
            px, py, pc = x ^ (k >> 2), y ^ ((k >> 1) & 1), c ^ (k & 1)
            copies.append(pltpu.make_async_remote_copy(
                src_ref=p_ref, dst_ref=all_ref.at[me], send_sem=send.at[k - 1], recv_sem=recv.at[k - 1],
                device_id=(px, py, pc), device_id_type=MESH))
        for cp in copies:
            cp.start()
        for cp in copies:
            cp.wait()
        total = all_ref[0]
        for k in range(1, N_DEV):
            total = total + all_ref[k]
        out_ref[...] = total

    return pl.pallas_call(
        body, name="allreduce_small",
        in_specs=[pl.BlockSpec(memory_space=pltpu.VMEM)],
        out_specs=pl.BlockSpec(memory_space=pltpu.VMEM),
        out_shape=jax.ShapeDtypeStruct((rows, width), F32),
        scratch_shapes=[pltpu.VMEM((N_DEV, rows, width), F32),
                        pltpu.SemaphoreType.DMA((N_DEV - 1,)), pltpu.SemaphoreType.DMA((N_DEV - 1,))],
    )(part)


def _row_tile(rows):
    t = min(rows, 256)
    while rows % t:
        t //= 2
    return t


def _core_and_chip():
    return jnp.stack([lax.axis_index("c"), 2 * lax.axis_index("x") + lax.axis_index("y")]).astype(jnp.int32)


def _sum_siblings(own, got, name):
    _, rows, cols = own.shape
    half = rows // 2
    t = _row_tile(half)
    nb = half // t

    def body(place_ref, own_ref, got_ref, mine_ref, out_ref):
        total = own_ref[...] + got_ref[...]
        out_ref[...] = total.astype(BF16)

        @pl.when(pl.program_id(1) == place_ref[1])
        def _():
            mine_ref[...] = total

    slab_blk = pl.BlockSpec((None, t, cols), lambda r, j, place: (j, place[0] * nb + r, 0))
    return pl.pallas_call(
        body, name=name,
        grid_spec=pltpu.PrefetchScalarGridSpec(
            num_scalar_prefetch=1, grid=(nb, N_CHIPS),
            in_specs=[slab_blk, slab_blk],
            out_specs=[pl.BlockSpec((t, cols), lambda r, j, place: (r, 0)), slab_blk]),
        out_shape=[jax.ShapeDtypeStruct((half, cols), F32), jax.ShapeDtypeStruct(own.shape, BF16)],
        compiler_params=_params(("arbitrary", "arbitrary")),
    )(_core_and_chip(), own, got)


def _sum_chips(mine, got, name):
    half, cols = mine.shape
    t = _row_tile(half)
    nb = half // t

    def body(place_ref, mine_ref, got_ref, out_ref):
        total = mine_ref[...]
        for k in range(3):
            total = total + got_ref[k].astype(F32)
        out_ref[...] = total

    return pl.pallas_call(
        body, name=name,
        grid_spec=pltpu.PrefetchScalarGridSpec(
            num_scalar_prefetch=1, grid=(nb,),
            in_specs=[pl.BlockSpec((t, cols), lambda r, place: (r, 0)),
                      pl.BlockSpec((3, t, cols), lambda r, place: (0, place[0] * nb + r, 0))],
            out_specs=pl.BlockSpec((t, cols), lambda r, place: (place[0] * nb + r, 0))),
        out_shape=jax.ShapeDtypeStruct((2 * half, cols), F32),
        compiler_params=_params(("arbitrary",)),
    )(_core_and_chip(), mine, got)


def _adamw(w, g, m, v, name, g_row0=0):
    rows, cols = w.shape
    t = _row_tile(rows)
    assert g_row0 % t == 0
    off = g_row0 // t

    def body(w_ref, g_ref, m_ref, v_ref, go_ref, d_ref, mo_ref, vo_ref):
        g_t = g_ref[...]
        m_new = ADAM_B1 * m_ref[...] + (1.0 - ADAM_B1) * g_t
        v_new = ADAM_B2 * v_ref[...] + (1.0 - ADAM_B2) * (g_t * g_t)
        m_hat = m_new / (1.0 - ADAM_B1 ** ADAM_STEP)
        v_hat = v_new / (1.0 - ADAM_B2 ** ADAM_STEP)
        go_ref[...] = g_t
        d_ref[...] = -ADAM_LR * (m_hat / (jnp.sqrt(v_hat) + ADAM_EPS) + ADAM_WD * w_ref[...])
        mo_ref[...] = m_new
        vo_ref[...] = v_new

    blk = pl.BlockSpec((t, cols), lambda r: (r, 0))
    return pl.pallas_call(
        body, name=name, grid=(rows // t,),
        in_specs=[blk, pl.BlockSpec((t, cols), lambda r: (r + off, 0)), blk, blk],
        out_specs=[blk] * 4,
        out_shape=[jax.ShapeDtypeStruct((rows, cols), F32)] * 4,
        compiler_params=_params(("arbitrary",)),
    )(w, g, m, v)


def _pack_small(d, vectors):
    rows = []
    for vec in vectors:
        flat = vec.reshape(-1)
        n_rows = -(-flat.shape[0] // d)
        rows.append(jnp.pad(flat, (0, n_rows * d - flat.shape[0])).reshape(n_rows, d))
    return jnp.concatenate(rows, axis=0)


def _unpack_small(packed, d, shapes):
    out, row = [], 0
    for shape in shapes:
        size = 1
        for s in shape:
            size *= s
        n_rows = -(-size // d)
        out.append(packed[row:row + n_rows].reshape(-1)[:size].reshape(shape))
        row += n_rows
    return out


def kernel(x, norm_pre, norm_post, pool_w_in, pool_w_group, pool_scale, pool_w_out, att_w_in, att_rel_bias, att_w_out, loss_target, m_norm_pre, m_norm_post, m_pool_w_in, m_pool_w_group, m_pool_scale, m_pool_w_out, m_att_w_in, m_att_rel_bias, m_att_w_out, v_norm_pre, v_norm_post, v_pool_w_in, v_pool_w_group, v_pool_scale, v_pool_w_out, v_att_w_in, v_att_rel_bias, v_att_w_out):
    _, s_len, d = x.shape
    gw = d // 2
    q = gw // N_CHIPS
    x2d = x.reshape(s_len, d)
    target = loss_target.reshape(s_len, d)

    def pack_g(p_group):
        return p_group.reshape(N_CHIPS * q, gw)

    wp_shard = jnp.concatenate([pool_w_in[0], pool_w_out[0]], axis=0).astype(BF16)
    wt_shard = jnp.concatenate([att_w_in[0], att_w_out[0]], axis=0).astype(BF16)
    wp, wg = _gather_weights([wp_shard, pack_g(pool_w_group).astype(BF16)])
    wg = wg.reshape(N_CHIPS, N_CHIPS, q, gw)

    x1, h0, mixed, z0, y0, wt = _l0_fwd(x2d, norm_pre[0:1], norm_post[0:1], pool_scale, wp, wg, wt_shard)
    h1, q_, k_, v_, z1 = _l1_inproj(x1, norm_pre[1:2], wt)
    bias = _bias_tiles(att_rel_bias[0])
    o, lse = _attn_fwd(q_, k_, v_, bias)

    g2, gated1, dy1, d_o, dz1, dgpost1, sq = _l1_out(o, z1, x1, target, norm_post[1:2], wt)
    dq, dk, dv, dbias = _attn_bwd(q_, k_, v_, bias, d_o, o, lse)
    dx1, du1, dgpre1 = _l1_in_bwd(dq, dk, dv, dz1, x1, g2, norm_pre[1:2], wt)
    d_rel = _rel_bias_grad(dbias)

    blk = lambda n: n
    zero = lambda n: 0
    slab_t = d + d // N_CHIPS
    whole_in = pl.BlockSpec((None, d, d), lambda n, kk: (n, 0, 0))
    gt = _wgrad(h1, du1, "wgrad_att_in", N_CHIPS, d, d, zero, blk, (N_CHIPS, slab_t, d), whole_in)
    gt = _wgrad(gated1, dy1, "wgrad_att_out", 1, d, d, zero, zero, (N_CHIPS, slab_t, d),
                pl.BlockSpec((N_CHIPS, d // N_CHIPS, d), lambda n, kk: (0, N_CHIPS, 0)), into=gt)

    grad_x, du0, gated0, dy0, dmm, dgpre0, dgpost0, dscale = _l0_bwd(
        dx1, x2d, y0, mixed, z0, norm_pre[0:1], norm_post[0:1], pool_scale, wp, wg)

    swap_rider = lambda part: (functools.partial(_SwapPlan, send_half=_keep_own_half), part, part.shape, (1, 1))
    scatter_rider = lambda send: (_ScatterPlan, send, (3,) + send.shape[1:], (3, 3))
    gi, got_t = _wgrad(h0, du0, "wgrad_pool_in", N_CHIPS, d, d, zero, blk, (N_CHIPS, d, d), whole_in,
                       riders=[swap_rider(gt)])
    mine_t, send_t = _sum_siblings(gt, got_t, "sum_siblings_att")
    go, recv_t, got_i = _wgrad(gated0, dy0, "wgrad_pool_out", N_CHIPS, gw, d, blk, zero, (N_CHIPS, gw, d),
                               pl.BlockSpec((None, gw, d), lambda n, kk: (n, 0, 0)),
                               riders=[scatter_rider(send_t), swap_rider(gi)])
    mine_i, send_i = _sum_siblings(gi, got_i, "sum_siblings_pool_in")
    gg, recv_i, got_o = _wgrad(mixed, dmm, "wgrad_pool_group", N_CHIPS, gw, gw, blk, blk, (N_CHIPS, N_CHIPS, q, gw),
                               pl.BlockSpec((N_CHIPS, None, q, gw), lambda n, kk: (0, n, 0, 0)),
                               riders=[scatter_rider(send_i), swap_rider(go)])
    gg = gg.reshape(N_CHIPS, N_CHIPS * q, gw)
    mine_o, send_o = _sum_siblings(go, got_o, "sum_siblings_pool_out")
    (got_g,) = _swap_halves([gg], "swap_core_partials_group", _keep_own_half, in_place=False)
    mine_g, send_g = _sum_siblings(gg, got_g, "sum_siblings_group")
    recv_o, recv_g = _scatter_to_chips([send_o, send_g])
    red_t = _sum_chips(mine_t, recv_t, "sum_chips_att")
    red_i = _sum_chips(mine_i, recv_i, "sum_chips_pool_in")
    red_o = _sum_chips(mine_o, recv_o, "sum_chips_pool_out")
    red_g = _sum_chips(mine_g, recv_g, "sum_chips_group")
    grad_t, grad_i, grad_o, grad_g = _swap_halves([red_t, red_i, red_o, red_g], "swap_reduced_halves",
                                                  lambda c: c, in_place=True)

    small_shapes = [norm_pre.shape, norm_post.shape, pool_scale.shape, att_rel_bias.shape]
    part = _pack_small(d, [jnp.concatenate([dgpre0, dgpre1], axis=0), jnp.concatenate([dgpost0, dgpost1], axis=0),
                           dscale, d_rel, sq])
    total = _allreduce_small(part)
    loss = (0.5 / d) * jnp.sum(total[-1])
    w_small = _pack_small(d, [norm_pre, norm_post, pool_scale, att_rel_bias, jnp.zeros((d,), F32)])
    m_small = _pack_small(d, [m_norm_pre, m_norm_post, m_pool_scale, m_att_rel_bias, jnp.zeros((d,), F32)])
    v_small = _pack_small(d, [v_norm_pre, v_norm_post, v_pool_scale, v_att_rel_bias, jnp.ones((d,), F32)])
    small_out = [_unpack_small(a, d, small_shapes) for a in _adamw(w_small, total, m_small, v_small, "adamw_small")]

    big = {}
    for name, w, m, v, grad, row0 in [("pool_w_in", pool_w_in, m_pool_w_in, v_pool_w_in, grad_i, 0),
                                      ("pool_w_out", pool_w_out, m_pool_w_out, v_pool_w_out, grad_o, 0),
                                      ("att_w_in", att_w_in, m_att_w_in, v_att_w_in, grad_t, 0),
                                      ("att_w_out", att_w_out, m_att_w_out, v_att_w_out, grad_t, d)]:
        outs = _adamw(w[0], grad, m[0], v[0], "adamw_" + name, g_row0=row0)
        big[name] = [a.reshape(w.shape) for a in outs]
    outs = _adamw(pack_g(pool_w_group), grad_g, pack_g(m_pool_w_group), pack_g(v_pool_w_group), "adamw_pool_w_group")
    big["pool_w_group"] = [a.reshape(pool_w_group.shape) for a in outs]

    def leaf(kind):
        return (small_out[kind][0], small_out[kind][1], big["pool_w_in"][kind], big["pool_w_group"][kind],
                small_out[kind][2], big["pool_w_out"][kind], big["att_w_in"][kind], small_out[kind][3],
                big["att_w_out"][kind])

    return (loss, grad_x.reshape(x.shape), *leaf(0), *leaf(1), *leaf(2), *leaf(3))
```

```python
import functools

import jax
import jax.numpy as jnp
from jax import lax
from jax.experimental import pallas as pl
from jax.experimental.pallas import tpu as pltpu

F32 = jnp.float32
BF16 = jnp.bfloat16

RMS_EPS = 1e-6
CHUNK = 64
HEAD_DIM = 64
LEFT_CHUNKS = 8
PAD = LEFT_CHUNKS * CHUNK
MAX_REL = 256
POOL_WINDOWS = (2, 4, 8, 16)
HALO = 16
TS = 256
TS1 = 512
QB = 256
KB = QB + PAD
HEAD_PAIR = 2 * HEAD_DIM
NEG = -1e30
N_CHIPS = 4
N_DEV = 8

ADAM_LR = 0.001
ADAM_B1 = 0.9
ADAM_B2 = 0.999
ADAM_EPS = 1e-08
ADAM_WD = 0.01
ADAM_STEP = 10

VMEM_LIMIT = 56 * 1024 * 1024
MESH = pl.DeviceIdType.MESH
ANY = pl.BlockSpec(memory_space=pl.ANY)


def _dot(a, b):
    return jnp.dot(a, b, preferred_element_type=F32)


def _dot_nt(a, b):
    return lax.dot_general(a, b, (((1,), (1,)), ((), ())), preferred_element_type=F32)


def _dot_tn(a, b):
    return lax.dot_general(a, b, (((0,), (0,)), ((), ())), preferred_element_type=F32)


def _params(sem, limit=VMEM_LIMIT):
    return pltpu.CompilerParams(dimension_semantics=sem, vmem_limit_bytes=limit)


def _rms(x):
    return lax.rsqrt(jnp.mean(x * x, axis=-1, keepdims=True) + RMS_EPS)


def _rms_bwd(dyh, xh, r):
    return r * (dyh - xh * jnp.mean(dyh * xh, axis=-1, keepdims=True))


def _silu_parts(z):
    sig = jax.nn.sigmoid(z)
    return z * sig, sig * (1.0 + z * (1.0 - sig))


def _inv_count(tile, rows, w):
    t = tile * rows + lax.broadcasted_iota(jnp.int32, (rows, 1), 0)
    return 1.0 / jnp.minimum(t + 1, w).astype(F32)


def _start_weight_copies(copies):
    for c in copies:
        c.start()
    for c in copies:
        c.wait()


def _l0_fwd(x, g_pre, g_post, scale, wa, wg, next_shard):
    s_len, d = x.shape
    pw, gw = 2 * d, d // 2
    q = gw // N_CHIPS
    nt = s_len // TS
    assert nt >= 3

    def body(x_ref, gpre_ref, gpost_ref, sc_ref, wa_ref, wg_ref, shard_ref,
             x1_ref, h_ref, mixed_ref, z_ref, y_ref, next_ref,
             win_s, wout_s, wg_s, halo_s, sems, *gather_sems):
        i = pl.program_id(0)

        @pl.when(i == 0)
        def _():
            _GatherPlan([shard_ref], [next_ref], *gather_sems).send()
            copies = [pltpu.make_async_copy(wa_ref.at[:, pl.ds(0, d), :], win_s, sems.at[0]),
                      pltpu.make_async_copy(wa_ref.at[:, pl.ds(d, gw), :], wout_s, sems.at[1])]
            copies += [pltpu.make_async_copy(wg_ref.at[j], wg_s.at[:, pl.ds(j * q, q), :], sems.at[2 + j])
                       for j in range(N_CHIPS)]
            _start_weight_copies(copies)
            halo_s[...] = jnp.zeros_like(halo_s)

        @pl.when(i == nt // 2)
        def _():
            _GatherPlan([shard_ref], [next_ref], *gather_sems).pass_on()

        @pl.when(i == nt - 1)
        def _():
            _GatherPlan([shard_ref], [next_ref], *gather_sems).finish()

        xt = x_ref[...]
        h = ((xt * _rms(xt)) * gpre_ref[...]).astype(BF16)
        h_ref[...] = h
        a_blocks = [_dot(h, win_s[0]), _dot(h, win_s[1])]
        y = jnp.zeros((TS, d), F32)
        for g, w in enumerate(POOL_WINDOWS):
            a_g = a_blocks[g // 2][:, (g % 2) * gw:(g % 2 + 1) * gw]
            ext = jnp.concatenate([halo_s[g], a_g], axis=0)
            shift = 1
            while shift < w:
                ext = ext + pltpu.roll(ext, shift, 0)
                shift *= 2
            mixed = (ext[HALO:] * _inv_count(i, TS, w) - a_g).astype(BF16)
            halo_s[g] = a_g[TS - HALO:]
            mixed_ref[:, g * gw:(g + 1) * gw] = mixed
            z_g = _dot(h, win_s[2 + g // 2, :, (g % 2) * gw:(g % 2 + 1) * gw])
            z_ref[:, g * gw:(g + 1) * gw] = z_g
            ms = _dot(mixed, wg_s[g]) * sc_ref[:, g * gw:(g + 1) * gw]
            gated = (ms * _silu_parts(z_g)[0]).astype(BF16)
            y = y + _dot(gated, wout_s[g])
        y_ref[...] = y
        x1_ref[...] = xt + (y * _rms(y)) * gpost_ref[...]

    tile = lambda wdt: pl.BlockSpec((TS, wdt), lambda i: (i, 0))
    row = lambda wdt: pl.BlockSpec((1, wdt), lambda i: (0, 0))
    return pl.pallas_call(
        body, name="l0_fwd", grid=(nt,),
        in_specs=[tile(d), row(d), row(d), row(pw), ANY, ANY, ANY],
        out_specs=[tile(d), tile(d), tile(pw), tile(pw), tile(d), ANY],
        out_shape=[jax.ShapeDtypeStruct((s_len, d), F32), jax.ShapeDtypeStruct((s_len, d), BF16),
                   jax.ShapeDtypeStruct((s_len, pw), BF16), jax.ShapeDtypeStruct((s_len, pw), F32),
                   jax.ShapeDtypeStruct((s_len, d), F32),
                   jax.ShapeDtypeStruct((N_CHIPS,) + next_shard.shape, next_shard.dtype)],
        scratch_shapes=[pltpu.VMEM((N_CHIPS, d, d), BF16), pltpu.VMEM((N_CHIPS, gw, d), BF16),
                        pltpu.VMEM((N_CHIPS, gw, gw), BF16), pltpu.VMEM((N_CHIPS, HALO, gw), F32),
                        pltpu.SemaphoreType.DMA((2 + N_CHIPS,))] + _gather_sems(1),
        compiler_params=_params(("arbitrary",)),
    )(x, g_pre, g_post, scale, wa, wg, next_shard)


def _l0_bwd(dx1, x, y, mixed, z, g_pre, g_post, scale, wa, wg):
    s_len, d = x.shape
    pw, gw = 2 * d, d // 2
    q = gw // N_CHIPS
    nt = s_len // TS

    def body(dx1_ref, x_ref, y_ref, mixed_ref, z_ref, gpre_ref, gpost_ref, sc_ref, wa_ref, wg_ref,
             gx_ref, du_ref, gated_ref, dy_ref, dmm_ref, dgpre_ref, dgpost_ref, dsc_ref,
             win_s, wout_s, wg_s, halo_s, sems):
        i = pl.program_id(0)
        tile = nt - 1 - i

        @pl.when(i == 0)
        def _():
            copies = [pltpu.make_async_copy(wa_ref.at[:, pl.ds(0, d), :], win_s, sems.at[0]),
                      pltpu.make_async_copy(wa_ref.at[:, pl.ds(d, gw), :], wout_s, sems.at[1])]
            copies += [pltpu.make_async_copy(wg_ref.at[j], wg_s.at[:, pl.ds(j * q, q), :], sems.at[2 + j])
                       for j in range(N_CHIPS)]
            _start_weight_copies(copies)
            halo_s[...] = jnp.zeros_like(halo_s)
            dgpre_ref[...] = jnp.zeros_like(dgpre_ref)
            dgpost_ref[...] = jnp.zeros_like(dgpost_ref)
            dsc_ref[...] = jnp.zeros_like(dsc_ref)

        g_in = dx1_ref[...]
        yt = y_ref[...]
        r_y = _rms(yt)
        yh = yt * r_y
        dgpost_ref[...] += jnp.sum(g_in * yh, axis=0, keepdims=True)
        dy = _rms_bwd(g_in * gpost_ref[...], yh, r_y).astype(BF16)
        dy_ref[...] = dy
        dh = jnp.zeros((TS, d), F32)

        for g, w in enumerate(POOL_WINDOWS):
            cols = slice(g * gw, (g + 1) * gw)
            dgated = _dot_nt(dy, wout_s[g])
            mm = _dot(mixed_ref[:, cols], wg_s[g])
            sc = sc_ref[:, cols]
            ms = mm * sc
            z_g = z_ref[:, cols]
            sz, dsz = _silu_parts(z_g)
            gated_ref[:, cols] = (ms * sz).astype(BF16)
            dms = dgated * sz
            dz = (dgated * ms * dsz).astype(BF16)
            dsc_ref[:, cols] += jnp.sum(dms * mm, axis=0, keepdims=True)
            dmm = (dms * sc).astype(BF16)
            dmm_ref[:, cols] = dmm
            dmixed = _dot_nt(dmm, wg_s[g])
            e = dmixed * _inv_count(tile, TS, w)
            ext = jnp.concatenate([e, halo_s[g]], axis=0)
            shift = 1
            while shift < w:
                ext = ext + pltpu.roll(ext, TS + HALO - shift, 0)
                shift *= 2
            da = (ext[:TS] - dmixed).astype(BF16)
            halo_s[g] = e[:HALO]
            du_ref[:, cols] = da
            du_ref[:, pw + g * gw:pw + (g + 1) * gw] = dz
            wa_blk = win_s[g // 2, :, (g % 2) * gw:(g % 2 + 1) * gw]
            wz_blk = win_s[2 + g // 2, :, (g % 2) * gw:(g % 2 + 1) * gw]
            dh = dh + _dot_nt(da, wa_blk) + _dot_nt(dz, wz_blk)
        xt = x_ref[...]
        r_x = _rms(xt)
        xh = xt * r_x
        dgpre_ref[...] += jnp.sum(dh * xh, axis=0, keepdims=True)
        gx_ref[...] = g_in + _rms_bwd(dh * gpre_ref[...], xh, r_x)

    tile_spec = lambda wdt: pl.BlockSpec((TS, wdt), lambda i: (nt - 1 - i, 0))
    row = lambda wdt: pl.BlockSpec((1, wdt), lambda i: (0, 0))
    return pl.pallas_call(
        body, name="l0_bwd", grid=(nt,),
        in_specs=[tile_spec(d), tile_spec(d), tile_spec(d), tile_spec(pw), tile_spec(pw),
                  row(d), row(d), row(pw), ANY, ANY],
        out_specs=[tile_spec(d), tile_spec(2 * pw), tile_spec(pw), tile_spec(d), tile_spec(pw),
                   row(d), row(d), row(pw)],
        out_shape=[jax.ShapeDtypeStruct((s_len, d), F32), jax.ShapeDtypeStruct((s_len, 2 * pw), BF16),
                   jax.ShapeDtypeStruct((s_len, pw), BF16), jax.ShapeDtypeStruct((s_len, d), BF16),
                   jax.ShapeDtypeStruct((s_len, pw), BF16),
                   jax.ShapeDtypeStruct((1, d), F32), jax.ShapeDtypeStruct((1, d), F32),
                   jax.ShapeDtypeStruct((1, pw), F32)],
        scratch_shapes=[pltpu.VMEM((N_CHIPS, d, d), BF16), pltpu.VMEM((N_CHIPS, gw, d), BF16),
                        pltpu.VMEM((N_CHIPS, gw, gw), BF16), pltpu.VMEM((N_CHIPS, HALO, gw), F32),
                        pltpu.SemaphoreType.DMA((2 + N_CHIPS,))],
        compiler_params=_params(("arbitrary",)),
    )(dx1, x, y, mixed, z, g_pre, g_post, scale, wa, wg)


def _l1_inproj(x1, g_pre, wa):
    s_len, d = x1.shape
    nt = s_len // TS1
    npad = PAD // TS1

    def body(x_ref, gpre_ref, wa_ref, h_ref, q_ref, k_ref, v_ref, z_ref, win_s, sem):
        i = pl.program_id(0)

        @pl.when(i == 0)
        def _():
            _start_weight_copies([pltpu.make_async_copy(wa_ref.at[:, pl.ds(0, d), :], win_s, sem.at[0])])

        @pl.when(i < npad)
        def _():
            k_ref[...] = jnp.zeros_like(k_ref)
            v_ref[...] = jnp.zeros_like(v_ref)

        @pl.when(i >= npad)
        def _():
            xt = x_ref[...]
            h = ((xt * _rms(xt)) * gpre_ref[...]).astype(BF16)
            h_ref[...] = h
            q_ref[...] = _dot(h, win_s[0]).astype(BF16)
            k_ref[...] = _dot(h, win_s[1]).astype(BF16)
            v_ref[...] = _dot(h, win_s[2]).astype(BF16)
            z_ref[...] = _dot(h, win_s[3])

    tile = pl.BlockSpec((TS1, d), lambda i: (jnp.maximum(i - npad, 0), 0))
    padded = pl.BlockSpec((TS1, d), lambda i: (i, 0))
    return pl.pallas_call(
        body, name="l1_inproj", grid=(nt + npad,),
        in_specs=[tile, pl.BlockSpec((1, d), lambda i: (0, 0)), ANY],
        out_specs=[tile, tile, padded, padded, tile],
        out_shape=[jax.ShapeDtypeStruct((s_len, d), BF16), jax.ShapeDtypeStruct((s_len, d), BF16),
                   jax.ShapeDtypeStruct((PAD + s_len, d), BF16), jax.ShapeDtypeStruct((PAD + s_len, d), BF16),
                   jax.ShapeDtypeStruct((s_len, d), F32)],
        scratch_shapes=[pltpu.VMEM((N_CHIPS, d, d), BF16), pltpu.SemaphoreType.DMA((1,))],
        compiler_params=_params(("arbitrary",)),
    )(x1, g_pre, wa)


SKEW = QB + KB
N_CLIPPED = QB + PAD - MAX_REL + 1
assert QB <= MAX_REL


def _bias_tiles(rel_bias):
    nh = rel_bias.shape[0]
    by_column = jnp.concatenate([jnp.broadcast_to(rel_bias[:, 2 * MAX_REL:], (nh, N_CLIPPED)),
                                 jnp.flip(rel_bias[:, MAX_REL + 1 - QB:2 * MAX_REL], axis=1)], axis=1).astype(F32)

    def body(col_ref, out_ref):
        rows = jnp.broadcast_to(col_ref[pl.ds(pl.program_id(0), 1), :], (QB, SKEW))
        tile = pltpu.roll(rows, SKEW - QB, 1, stride=1, stride_axis=0)[:, :KB]
        i = lax.broadcasted_iota(jnp.int32, (QB, KB), 0)
        j = lax.broadcasted_iota(jnp.int32, (QB, KB), 1)
        first = (i // CHUNK) * CHUNK
        out_ref[...] = jnp.where((j >= first) & (j < first + PAD + CHUNK), tile, NEG)

    return pl.pallas_call(
        body, name="bias_tiles", grid=(nh,),
        in_specs=[pl.BlockSpec((nh, SKEW), lambda h: (0, 0))],
        out_specs=pl.BlockSpec((None, QB, KB), lambda h: (h, 0, 0)),
        out_shape=jax.ShapeDtypeStruct((nh, QB, KB), F32),
        compiler_params=_params(("arbitrary",)),
    )(by_column)


ROWS = 16


BLOCKS_PER_STEP = 8


def _row_blocks(rows=ROWS):
    return [pl.ds(r * rows, rows) for r in range(QB // rows)]


def _attn_fwd(q, k, v, bias):
    s_len, d = q.shape
    nhp = d // HEAD_PAIR
    per_step = min(BLOCKS_PER_STEP, s_len // QB)
    nq = s_len // (QB * per_step)

    def body(q_ref, k_ref, v_ref, b_ref, o_ref, lse_ref, s_s, p_s, l_s):
        head_of_lane = lax.broadcasted_iota(jnp.int32, (1, HEAD_PAIR), 1) // HEAD_DIM

        units = [(u, hh) for u in range(per_step) for hh in range(2)]

        def band(u):
            return pl.ds(pl.multiple_of((pl.program_id(1) * per_step + u) * QB, QB), KB)

        def issue_scores(n):
            u, hh = units[n]
            qt = q_ref[pl.ds(u * QB, QB), :] * (HEAD_DIM ** -0.5)
            s_s[n % 2] = _dot_nt(jnp.where(head_of_lane == hh, qt, jnp.zeros_like(qt)), k_ref[band(u), :])

        issue_scores(0)
        outs, lses = [], []
        for n, (u, hh) in enumerate(units):
            if n + 1 < len(units):
                issue_scores(n + 1)
            first_key = (pl.program_id(1) * per_step + u) * QB
            colvalid = (first_key + lax.broadcasted_iota(jnp.int32, (1, KB), 1)) >= PAD
            for rows in _row_blocks():
                s = jnp.where(colvalid, s_s[n % 2, rows, :] + b_ref[hh, rows, :], NEG)
                m = jnp.max(s, axis=-1, keepdims=True)
                e = jnp.exp(s - m)
                l = jnp.sum(e, axis=-1, keepdims=True)
                p_s[n % 2, rows, :] = (e * (1.0 / l)).astype(BF16)
                l_s[n % 2, rows, :] = m + jnp.log(l)
            outs.append(_dot(p_s[n % 2], v_ref[band(u), :]))
            lses.append(l_s[n % 2])
            if hh == 1:
                here = pl.ds(u * QB, QB)
                o_ref[here, :] = jnp.where(head_of_lane == 0, outs[-2], outs[-1])
                lse_ref[here, :] = jnp.where(head_of_lane == 0, lses[-2], lses[-1])

    blk = pl.BlockSpec((per_step * QB, HEAD_PAIR), lambda hp, b: (b, hp))
    whole = pl.BlockSpec((PAD + s_len, HEAD_PAIR), lambda hp, b: (0, hp))
    return pl.pallas_call(
        body, name="attn_fwd", grid=(nhp, nq),
        in_specs=[blk, whole, whole, pl.BlockSpec((2, QB, KB), lambda hp, b: (hp, 0, 0))],
        out_specs=[blk, blk],
        out_shape=[jax.ShapeDtypeStruct((s_len, d), F32), jax.ShapeDtypeStruct((s_len, d), F32)],
        scratch_shapes=[pltpu.VMEM((2, QB, KB), F32), pltpu.VMEM((2, QB, KB), BF16), pltpu.VMEM((2, QB, 1), F32)],
        compiler_params=_params(("arbitrary", "arbitrary")),
    )(q, k, v, bias)


def _attn_bwd(q, k, v, bias, d_o, o, lse):
    s_len, d = q.shape
    nhp = d // HEAD_PAIR
    per_step = min(BLOCKS_PER_STEP, s_len // QB)
    nq = s_len // (QB * per_step)
    qk_scale = HEAD_DIM ** -0.5

    def body(q_ref, k_ref, v_ref, b_ref, do_ref, o_ref, lse_ref, dq_ref, dk_out_ref, dv_out_ref, db_ref,
             s_s, dp_s, p_s, ds_s, lse_s, delta_s, dk_ref, dv_ref):
        @pl.when(pl.program_id(1) == 0)
        def _():
            dk_ref[...] = jnp.zeros_like(dk_ref)
            dv_ref[...] = jnp.zeros_like(dv_ref)
            db_ref[...] = jnp.zeros_like(db_ref)

        head_of_lane = lax.broadcasted_iota(jnp.int32, (1, HEAD_PAIR), 1) // HEAD_DIM

        units = [(u, hh) for u in range(per_step) for hh in range(2)]

        def band(u):
            return pl.ds(pl.multiple_of((pl.program_id(1) * per_step + u) * QB, QB), KB)

        def masked(ref, u, hh, factor=None):
            x = ref[pl.ds(u * QB, QB), :]
            x = x if factor is None else x * factor
            return jnp.where(head_of_lane == hh, x, jnp.zeros_like(x))

        def issue_tiles(n):
            u, hh = units[n]
            here = pl.ds(u * QB, QB)
            do_o = do_ref[here, :].astype(F32) * o_ref[here, :]
            delta_s[n % 2] = jnp.sum(jnp.where(head_of_lane == hh, do_o, 0.0), axis=-1, keepdims=True)
            lse_s[n % 2] = lse_ref[here, hh * HEAD_DIM:hh * HEAD_DIM + 1]
            s_s[n % 2] = _dot_nt(masked(q_ref, u, hh, qk_scale), k_ref[band(u), :])
            dp_s[n % 2] = _dot_nt(masked(do_ref, u, hh), v_ref[band(u), :])

        issue_tiles(0)
        dq_heads, dk_band, dv_band = [], None, None
        for n, (u, hh) in enumerate(units):
            if n + 1 < len(units):
                issue_tiles(n + 1)
            first_key = (pl.program_id(1) * per_step + u) * QB
            colvalid = (first_key + lax.broadcasted_iota(jnp.int32, (1, KB), 1)) >= PAD
            for rows in _row_blocks():
                t = jnp.where(colvalid, s_s[n % 2, rows, :] + b_ref[hh, rows, :] - lse_s[n % 2, rows, :], NEG)
                p = jnp.exp(t)
                ds = p * (dp_s[n % 2, rows, :] - delta_s[n % 2, rows, :])
                db_ref[hh, rows, :] += ds
                p_s[n % 2, rows, :] = p.astype(BF16)
                ds_s[n % 2, rows, :] = ds.astype(BF16)
            q_m = masked(q_ref, u, hh, qk_scale)
            dv_unit = _dot_tn(p_s[n % 2], masked(do_ref, u, hh))
            dq_heads.append(_dot(ds_s[n % 2], k_ref[band(u), :]) * qk_scale)
            dk_unit = _dot_tn(ds_s[n % 2], q_m)
            if hh == 0:
                dk_band, dv_band = dk_unit, dv_unit
            else:
                dq_ref[pl.ds(u * QB, QB), :] = jnp.where(head_of_lane == 0, dq_heads[-2], dq_heads[-1]).astype(BF16)
                dk_ref[band(u), :] += dk_band + dk_unit
                dv_ref[band(u), :] += dv_band + dv_unit

        @pl.when(pl.program_id(1) == nq - 1)
        def _():
            dk_out_ref[...] = dk_ref[...].astype(BF16)
            dv_out_ref[...] = dv_ref[...].astype(BF16)

    blk = pl.BlockSpec((per_step * QB, HEAD_PAIR), lambda hp, b: (b, hp))
    whole = pl.BlockSpec((PAD + s_len, HEAD_PAIR), lambda hp, b: (0, hp))
    btile = pl.BlockSpec((2, QB, KB), lambda hp, b: (hp, 0, 0))
    return pl.pallas_call(
        body, name="attn_bwd", grid=(nhp, nq),
        in_specs=[blk, whole, whole, btile, blk, blk, blk],
        out_specs=[blk, whole, whole, btile],
        out_shape=[jax.ShapeDtypeStruct((s_len, d), BF16),
                   jax.ShapeDtypeStruct((PAD + s_len, d), BF16), jax.ShapeDtypeStruct((PAD + s_len, d), BF16),
                   jax.ShapeDtypeStruct(bias.shape, F32)],
        scratch_shapes=[pltpu.VMEM((2, QB, KB), F32), pltpu.VMEM((2, QB, KB), F32),
                        pltpu.VMEM((2, QB, KB), BF16), pltpu.VMEM((2, QB, KB), BF16),
                        pltpu.VMEM((2, QB, 1), F32), pltpu.VMEM((2, QB, 1), F32),
                        pltpu.VMEM((PAD + s_len, HEAD_PAIR), F32), pltpu.VMEM((PAD + s_len, HEAD_PAIR), F32)],
        compiler_params=_params(("arbitrary", "arbitrary")),
    )(q, k, v, bias, d_o, o, lse)


def _rel_bias_grad(db, riders=()):
    nh = db.shape[0]
    n_ride = len(riders)

    def body(*refs):
        db_ref, out_ref = refs[0], refs[1 + n_ride]

        def plans():
            return [rider[0]([refs[1 + r]], [refs[2 + n_ride + r]],
                             *refs[2 + 2 * n_ride + 2 * r:4 + 2 * n_ride + 2 * r]) for r, rider in enumerate(riders)]

        if riders:
            @pl.when(pl.program_id(0) == 0)
            def _():
                for plan in plans():
                    plan.send()

            @pl.when(pl.program_id(0) == nh - 1)
            def _():
                for plan in plans():
                    plan.finish()

        i0 = lax.broadcasted_iota(jnp.int32, (QB, QB), 0)
        i1 = lax.broadcasted_iota(jnp.int32, (QB, QB), 1)
        exchange = jnp.where(i0 + i1 == QB - 1, 1.0, 0.0).astype(BF16)
        rest = db_ref[...]
        flipped = jnp.zeros((QB, KB), F32)
        for _ in range(3):
            piece = rest.astype(BF16)
            flipped = flipped + _dot(exchange, piece)
            rest = rest - piece.astype(F32)
        m = jnp.concatenate([flipped, jnp.zeros((QB, SKEW - KB), F32)], axis=1)
        diag = jnp.sum(pltpu.roll(m, 1, 1, stride=1, stride_axis=0), axis=0, keepdims=True)
        c = lax.broadcasted_iota(jnp.int32, (1, SKEW), 1)
        clipped = jnp.sum(jnp.where(c < N_CLIPPED, diag, 0.0), axis=1, keepdims=True)
        out_ref[...] = jnp.where(c == 0, clipped, diag)

    results = pl.pallas_call(
        body, name="bias_diagonals", grid=(nh,),
        in_specs=[pl.BlockSpec((None, QB, KB), lambda h: (h, 0, 0))] + [ANY] * n_ride,
        out_specs=[pl.BlockSpec((None, 1, SKEW), lambda h: (h, 0, 0))] + [ANY] * n_ride,
        out_shape=[jax.ShapeDtypeStruct((nh, 1, SKEW), F32)]
                  + [jax.ShapeDtypeStruct(rider[2], rider[1].dtype) for rider in riders],
        scratch_shapes=[pltpu.SemaphoreType.DMA((n,)) for rider in riders for n in rider[3]],
        compiler_params=_params(("arbitrary",)),
    )(db, *[rider[1] for rider in riders])
    diag = results[0][:, 0]
    d_rel = jnp.concatenate([jnp.zeros((nh, MAX_REL + 1 - QB), F32), jnp.flip(diag[:, N_CLIPPED:], axis=1),
                             diag[:, :1]], axis=1)
    return (d_rel, *results[1:]) if riders else d_rel


def _l1_out(o, z, x1, target, g_post, wa):
    s_len, d = o.shape
    nt = s_len // TS
    slab = d // N_CHIPS

    def body(o_ref, z_ref, x1_ref, t_ref, gpost_ref, wa_ref,
             g2_ref, gated_ref, dy_ref, do_ref, dz_ref, dgpost_ref, sq_ref, wout_s, sems):
        i = pl.program_id(0)

        @pl.when(i == 0)
        def _():
            _start_weight_copies([
                pltpu.make_async_copy(wa_ref.at[j, pl.ds(d, slab), :],
                                      wout_s.at[pl.ds(j * slab, slab), :], sems.at[j])
                for j in range(N_CHIPS)])
            dgpost_ref[...] = jnp.zeros_like(dgpost_ref)
            sq_ref[...] = jnp.zeros_like(sq_ref)

        ot = o_ref[...]
        sz, dsz = _silu_parts(z_ref[...])
        gated = (ot * sz).astype(BF16)
        gated_ref[...] = gated
        y = _dot(gated, wout_s[...])
        r_y = _rms(y)
        yh = y * r_y
        err = x1_ref[...] + yh * gpost_ref[...] - t_ref[...]
        sq_ref[...] += jnp.sum(err * err, axis=0, keepdims=True)
        g2 = err * (1.0 / d)
        g2_ref[...] = g2
        dgpost_ref[...] += jnp.sum(g2 * yh, axis=0, keepdims=True)
        dy = _rms_bwd(g2 * gpost_ref[...], yh, r_y).astype(BF16)
        dy_ref[...] = dy
        dgated = _dot_nt(dy, wout_s[...])
        do_ref[...] = (dgated * sz).astype(BF16)
        dz_ref[...] = (dgated * ot * dsz).astype(BF16)

    tile = pl.BlockSpec((TS, d), lambda i: (i, 0))
    row = pl.BlockSpec((1, d), lambda i: (0, 0))
    return pl.pallas_call(
        body, name="l1_out", grid=(nt,),
        in_specs=[tile, tile, tile, tile, row, ANY],
        out_specs=[tile, tile, tile, tile, tile, row, row],
        out_shape=[jax.ShapeDtypeStruct((s_len, d), F32)] + [jax.ShapeDtypeStruct((s_len, d), BF16)] * 4
                  + [jax.ShapeDtypeStruct((1, d), F32)] * 2,
        scratch_shapes=[pltpu.VMEM((d, d), BF16), pltpu.SemaphoreType.DMA((N_CHIPS,))],
        compiler_params=_params(("arbitrary",)),
    )(o, z, x1, target, g_post, wa)


def _l1_in_bwd(dq, dk, dv, dz, x1, g2, g_pre, wa):
    s_len, d = x1.shape
    nt = s_len // TS1
    npad = PAD // TS1

    def body(dq_ref, dk_ref, dv_ref, dz_ref, x_ref, g2_ref, gpre_ref, wa_ref,
             dx_ref, du_ref, dgpre_ref, win_s, sem):
        i = pl.program_id(0)

        @pl.when(i == 0)
        def _():
            _start_weight_copies([pltpu.make_async_copy(wa_ref.at[:, pl.ds(0, d), :], win_s, sem.at[0])])
            dgpre_ref[...] = jnp.zeros_like(dgpre_ref)

        parts = [dq_ref[...], dk_ref[...], dv_ref[...], dz_ref[...]]
        dh = jnp.zeros((TS1, d), F32)
        for j, part in enumerate(parts):
            du_ref[:, j * d:(j + 1) * d] = part
            dh = dh + _dot_nt(part, win_s[j])
        xt = x_ref[...]
        r_x = _rms(xt)
        xh = xt * r_x
        dgpre_ref[...] += jnp.sum(dh * xh, axis=0, keepdims=True)
        dx_ref[...] = g2_ref[...] + _rms_bwd(dh * gpre_ref[...], xh, r_x)

    tile = pl.BlockSpec((TS1, d), lambda i: (i, 0))
    padded = pl.BlockSpec((TS1, d), lambda i: (i + npad, 0))
    row = pl.BlockSpec((1, d), lambda i: (0, 0))
    return pl.pallas_call(
        body, name="l1_in_bwd", grid=(nt,),
        in_specs=[tile, padded, padded, tile, tile, tile, row, ANY],
        out_specs=[tile, pl.BlockSpec((TS1, 4 * d), lambda i: (i, 0)), row],
        out_shape=[jax.ShapeDtypeStruct((s_len, d), F32), jax.ShapeDtypeStruct((s_len, 4 * d), BF16),
                   jax.ShapeDtypeStruct((1, d), F32)],
        scratch_shapes=[pltpu.VMEM((N_CHIPS, d, d), BF16), pltpu.SemaphoreType.DMA((1,))],
        compiler_params=_params(("arbitrary",)),
    )(dq, dk, dv, dz, x1, g2, g_pre, wa)


def _wgrad(a, b, name, n_out, a_width, b_width, a_block, b_block, out_shape, out_spec, into=None, riders=()):
    s_len = a.shape[0]
    tk = min(2048, s_len)
    nk = s_len // tk
    n_ride = len(riders)
    n_in = 2 + (into is not None) + n_ride

    def body(*refs):
        a_ref, b_ref = refs[:2]
        out_ref = refs[n_in]

        def plans():
            return [rider[0]([refs[n_in - n_ride + r]], [refs[n_in + 1 + r]],
                             *refs[n_in + 1 + n_ride + 2 * r:n_in + 3 + n_ride + 2 * r])
                    for r, rider in enumerate(riders)]

        if riders:
            @pl.when((pl.program_id(0) == 0) & (pl.program_id(1) == 0))
            def _():
                for plan in plans():
                    plan.send()

        @pl.when(pl.program_id(1) == 0)
        def _():
            out_ref[...] = jnp.zeros_like(out_ref)

        out_ref[...] += _dot_tn(a_ref[...], b_ref[...]).reshape(out_ref.shape)

        if riders:
            @pl.when((pl.program_id(0) == n_out - 1) & (pl.program_id(1) == nk - 1))
            def _():
                for plan in plans():
                    plan.finish()

    operands = [a, b] + ([into] if into is not None else []) + [rider[1] for rider in riders]
    results = pl.pallas_call(
        body, name=name, grid=(n_out, nk),
        in_specs=[pl.BlockSpec((tk, a_width), lambda n, kk: (kk, a_block(n))),
                  pl.BlockSpec((tk, b_width), lambda n, kk: (kk, b_block(n)))] + [ANY] * (n_in - 2),
        out_specs=[out_spec] + [ANY] * n_ride,
        out_shape=[jax.ShapeDtypeStruct(out_shape, F32)]
                  + [jax.ShapeDtypeStruct(rider[2], rider[1].dtype) for rider in riders],
        scratch_shapes=[pltpu.SemaphoreType.DMA((n,)) for rider in riders for n in rider[3]],
        input_output_aliases={2: 0} if into is not None else {},
        compiler_params=_params(("arbitrary", "arbitrary")),
    )(*operands)
    return results if riders else results[0]


def _place():
    x, y, c = lax.axis_index("x"), lax.axis_index("y"), lax.axis_index("c")
    others = [(1 - x, y), (x, 1 - y), (1 - x, 1 - y)]
    return x, y, c, others


class _GatherPlan:
    def __init__(self, src, dst, send, recv, fwd_send, fwd_recv, local):
        x, y, c, others = _place()
        me = 2 * x + y
        sibling = (x, y, 1 - c)

        def half(ref, cc):
            rows = ref.shape[0] // 2
            return ref.at[pl.ds(cc * rows, rows), :]

        self.mine = [pltpu.make_async_copy(src[a], dst[a].at[me], local.at[a]) for a in range(len(src))]
        self.first, self.passed, self.arrive, self.arrive_fwd = [], [], [], []
        for a in range(len(src)):
            for k, (ox, oy) in enumerate(others):
                sem = a * 3 + k
                self.first.append(pltpu.make_async_remote_copy(
                    src_ref=half(src[a], c), dst_ref=half(dst[a].at[me], c),
                    send_sem=send.at[sem], recv_sem=recv.at[sem], device_id=(ox, oy, c), device_id_type=MESH))
                theirs = half(dst[a].at[2 * ox + oy], c)
                self.arrive.append(pltpu.make_async_remote_copy(
                    src_ref=theirs, dst_ref=theirs, send_sem=send.at[sem], recv_sem=recv.at[sem],
                    device_id=(ox, oy, c), device_id_type=MESH))
                self.passed.append(pltpu.make_async_remote_copy(
                    src_ref=theirs, dst_ref=theirs, send_sem=fwd_send.at[sem], recv_sem=fwd_recv.at[sem],
                    device_id=sibling, device_id_type=MESH))
                other_half = half(dst[a].at[2 * ox + oy], 1 - c)
                self.arrive_fwd.append(pltpu.make_async_remote_copy(
                    src_ref=other_half, dst_ref=other_half, send_sem=fwd_send.at[sem], recv_sem=fwd_recv.at[sem],
                    device_id=sibling, device_id_type=MESH))

    def send(self):
        for cp in self.mine + self.first:
            cp.start()

    def pass_on(self):
        for got, onward in zip(self.arrive, self.passed):
            got.wait_recv()
            onward.start()

    def finish(self):
        for got in self.arrive_fwd:
            got.wait_recv()
        for cp in self.first + self.passed:
            cp.wait_send()
        for cp in self.mine:
            cp.wait()


def _gather_sems(n):
    return [pltpu.SemaphoreType.DMA((3 * n,))] * 4 + [pltpu.SemaphoreType.DMA((n,))]


def _gather_weights(shards):
    n = len(shards)

    def body(*refs):
        plan = _GatherPlan(refs[:n], refs[n:2 * n], *refs[2 * n:])
        plan.send()
        plan.pass_on()
        plan.finish()

    return pl.pallas_call(
        body, name="gather_weights",
        in_specs=[ANY] * n, out_specs=[ANY] * n,
        out_shape=[jax.ShapeDtypeStruct((N_CHIPS,) + s.shape, s.dtype) for s in shards],
        scratch_shapes=_gather_sems(n),
    )(*shards)


class _SwapPlan:
    def __init__(self, src, dst, send, recv, send_half):
        x, y, c, _ = _place()
        sibling = (x, y, 1 - c)
        self.out_copies, self.in_copies = [], []
        for a in range(len(src)):
            rows = src[a].shape[-2] // 2
            lead = (slice(None),) * (len(src[a].shape) - 2)
            going = lead + (pl.ds(send_half(c) * rows, rows), slice(None))
            coming = lead + (pl.ds((1 - send_half(c)) * rows, rows), slice(None))
            self.out_copies.append(pltpu.make_async_remote_copy(
                src_ref=src[a].at[going], dst_ref=dst[a].at[going], send_sem=send.at[a], recv_sem=recv.at[a],
                device_id=sibling, device_id_type=MESH))
            self.in_copies.append(pltpu.make_async_remote_copy(
                src_ref=src[a].at[coming], dst_ref=dst[a].at[coming], send_sem=send.at[a], recv_sem=recv.at[a],
                device_id=sibling, device_id_type=MESH))

    def send(self):
        for cp in self.out_copies:
            cp.start()

    def finish(self):
        for cp in self.in_copies:
            cp.wait_recv()
        for cp in self.out_copies:
            cp.wait_send()


def _keep_own_half(c):
    return 1 - c


def _swap_halves(arrays, name, send_half, in_place):
    n = len(arrays)

    def body(*refs):
        src, dst = refs[:n], refs[n:2 * n]
        plan = _SwapPlan(dst if in_place else src, dst, *refs[2 * n:], send_half=send_half)
        plan.send()
        plan.finish()

    return pl.pallas_call(
        body, name=name,
        in_specs=[ANY] * n, out_specs=[ANY] * n,
        out_shape=[jax.ShapeDtypeStruct(a.shape, a.dtype) for a in arrays],
        scratch_shapes=[pltpu.SemaphoreType.DMA((n,)), pltpu.SemaphoreType.DMA((n,))],
        input_output_aliases={a: a for a in range(n)} if in_place else {},
    )(*arrays)


class _ScatterPlan:
    def __init__(self, src, dst, send, recv):
        x, y, c, others = _place()
        self.out_copies, self.in_copies = [], []
        for a in range(len(src)):
            rows = src[a].shape[1] // 2
            mine = pl.ds(c * rows, rows)
            for k, (ox, oy) in enumerate(others):
                sem = a * 3 + k
                self.out_copies.append(pltpu.make_async_remote_copy(
                    src_ref=src[a].at[2 * ox + oy, mine, :], dst_ref=dst[a].at[k, mine, :],
                    send_sem=send.at[sem], recv_sem=recv.at[sem], device_id=(ox, oy, c), device_id_type=MESH))
                self.in_copies.append(pltpu.make_async_remote_copy(
                    src_ref=dst[a].at[k, mine, :], dst_ref=dst[a].at[k, mine, :],
                    send_sem=send.at[sem], recv_sem=recv.at[sem], device_id=(ox, oy, c), device_id_type=MESH))

    def send(self):
        for cp in self.out_copies:
            cp.start()

    def finish(self):
        for cp in self.in_copies:
            cp.wait_recv()
        for cp in self.out_copies:
            cp.wait_send()


def _allreduce_small(part):
    rows, width = part.shape

    def body(p_ref, out_ref, all_ref, send, recv):
        x, y, c, _ = _place()
        me = 4 * x + 2 * y + c
        all_ref[me] = p_ref[...]
        copies = []
        for k in range(1, N_DEV):
            px, py, pc = x ^ (k >> 2), y ^ ((k >> 1) & 1), c ^ (k & 1)
            copies.append(pltpu.make_async_remote_copy(
                src_ref=p_ref, dst_ref=all_ref.at[me], send_sem=send.at[k - 1], recv_sem=recv.at[k - 1],
                device_id=(px, py, pc), device_id_type=MESH))
        for cp in copies:
            cp.start()
        for cp in copies:
            cp.wait()
        total = all_ref[0]
        for k in range(1, N_DEV):
            total = total + all_ref[k]
        out_ref[...] = total

    return pl.pallas_call(
        body, name="allreduce_small",
        in_specs=[pl.BlockSpec(memory_space=pltpu.VMEM)],
        out_specs=pl.BlockSpec(memory_space=pltpu.VMEM),
        out_shape=jax.ShapeDtypeStruct((rows, width), F32),
        scratch_shapes=[pltpu.VMEM((N_DEV, rows, width), F32),
                        pltpu.SemaphoreType.DMA((N_DEV - 1,)), pltpu.SemaphoreType.DMA((N_DEV - 1,))],
    )(part)


def _row_tile(rows):
    t = min(rows, 256)
    while rows % t:
        t //= 2
    return t


def _core_and_chip():
    return jnp.stack([lax.axis_index("c"), 2 * lax.axis_index("x") + lax.axis_index("y")]).astype(jnp.int32)


def _sum_siblings(own, got, name):
    _, rows, cols = own.shape
    half = rows // 2
    t = _row_tile(half)
    nb = half // t

    def body(place_ref, own_ref, got_ref, mine_ref, out_ref):
        total = own_ref[...] + got_ref[...]
        out_ref[...] = total.astype(BF16)

        @pl.when(pl.program_id(1) == place_ref[1])
        def _():
            mine_ref[...] = total

    slab_blk = pl.BlockSpec((None, t, cols), lambda r, j, place: (j, place[0] * nb + r, 0))
    return pl.pallas_call(
        body, name=name,
        grid_spec=pltpu.PrefetchScalarGridSpec(
            num_scalar_prefetch=1, grid=(nb, N_CHIPS),
            in_specs=[slab_blk, slab_blk],
            out_specs=[pl.BlockSpec((t, cols), lambda r, j, place: (r, 0)), slab_blk]),
        out_shape=[jax.ShapeDtypeStruct((half, cols), F32), jax.ShapeDtypeStruct(own.shape, BF16)],
        compiler_params=_params(("arbitrary", "arbitrary")),
    )(_core_and_chip(), own, got)


def _sum_chips(mine, got, name):
    half, cols = mine.shape
    t = _row_tile(half)
    nb = half // t

    def body(place_ref, mine_ref, got_ref, out_ref):
        total = mine_ref[...]
        for k in range(3):
            total = total + got_ref[k].astype(F32)
        out_ref[...] = total

    return pl.pallas_call(
        body, name=name,
        grid_spec=pltpu.PrefetchScalarGridSpec(
            num_scalar_prefetch=1, grid=(nb,),
            in_specs=[pl.BlockSpec((t, cols), lambda r, place: (r, 0)),
                      pl.BlockSpec((3, t, cols), lambda r, place: (0, place[0] * nb + r, 0))],
            out_specs=pl.BlockSpec((t, cols), lambda r, place: (place[0] * nb + r, 0))),
        out_shape=jax.ShapeDtypeStruct((2 * half, cols), F32),
        compiler_params=_params(("arbitrary",)),
    )(_core_and_chip(), mine, got)


def _adamw(w, g, m, v, name, g_row0=0):
    rows, cols = w.shape
    t = _row_tile(rows)
    assert g_row0 % t == 0
    off = g_row0 // t

    def body(w_ref, g_ref, m_ref, v_ref, go_ref, d_ref, mo_ref, vo_ref):
        g_t = g_ref[...]
        m_new = ADAM_B1 * m_ref[...] + (1.0 - ADAM_B1) * g_t
        v_new = ADAM_B2 * v_ref[...] + (1.0 - ADAM_B2) * (g_t * g_t)
        m_hat = m_new / (1.0 - ADAM_B1 ** ADAM_STEP)
        v_hat = v_new / (1.0 - ADAM_B2 ** ADAM_STEP)
        go_ref[...] = g_t
        d_ref[...] = -ADAM_LR * (m_hat / (jnp.sqrt(v_hat) + ADAM_EPS) + ADAM_WD * w_ref[...])
        mo_ref[...] = m_new
        vo_ref[...] = v_new

    blk = pl.BlockSpec((t, cols), lambda r: (r, 0))
    return pl.pallas_call(
        body, name=name, grid=(rows // t,),
        in_specs=[blk, pl.BlockSpec((t, cols), lambda r: (r + off, 0)), blk, blk],
        out_specs=[blk] * 4,
        out_shape=[jax.ShapeDtypeStruct((rows, cols), F32)] * 4,
        compiler_params=_params(("arbitrary",)),
    )(w, g, m, v)


def _pack_small(d, vectors):
    rows = []
    for vec in vectors:
        flat = vec.reshape(-1)
        n_rows = -(-flat.shape[0] // d)
        rows.append(jnp.pad(flat, (0, n_rows * d - flat.shape[0])).reshape(n_rows, d))
    return jnp.concatenate(rows, axis=0)


def _unpack_small(packed, d, shapes):
    out, row = [], 0
    for shape in shapes:
        size = 1
        for s in shape:
            size *= s
        n_rows = -(-size // d)
        out.append(packed[row:row + n_rows].reshape(-1)[:size].reshape(shape))
        row += n_rows
    return out


def kernel(x, norm_pre, norm_post, pool_w_in, pool_w_group, pool_scale, pool_w_out, att_w_in, att_rel_bias, att_w_out, loss_target, m_norm_pre, m_norm_post, m_pool_w_in, m_pool_w_group, m_pool_scale, m_pool_w_out, m_att_w_in, m_att_rel_bias, m_att_w_out, v_norm_pre, v_norm_post, v_pool_w_in, v_pool_w_group, v_pool_scale, v_pool_w_out, v_att_w_in, v_att_rel_bias, v_att_w_out):
    _, s_len, d = x.shape
    gw = d // 2
    q = gw // N_CHIPS
    x2d = x.reshape(s_len, d)
    target = loss_target.reshape(s_len, d)

    def pack_g(p_group):
        return p_group.reshape(N_CHIPS * q, gw)

    wp_shard = jnp.concatenate([pool_w_in[0], pool_w_out[0]], axis=0).astype(BF16)
    wt_shard = jnp.concatenate([att_w_in[0], att_w_out[0]], axis=0).astype(BF16)
    wp, wg = _gather_weights([wp_shard, pack_g(pool_w_group).astype(BF16)])
    wg = wg.reshape(N_CHIPS, N_CHIPS, q, gw)

    x1, h0, mixed, z0, y0, wt = _l0_fwd(x2d, norm_pre[0:1], norm_post[0:1], pool_scale, wp, wg, wt_shard)
    h1, q_, k_, v_, z1 = _l1_inproj(x1, norm_pre[1:2], wt)
    bias = _bias_tiles(att_rel_bias[0])
    o, lse = _attn_fwd(q_, k_, v_, bias)

    g2, gated1, dy1, d_o, dz1, dgpost1, sq = _l1_out(o, z1, x1, target, norm_post[1:2], wt)
    dq, dk, dv, dbias = _attn_bwd(q_, k_, v_, bias, d_o, o, lse)
    dx1, du1, dgpre1 = _l1_in_bwd(dq, dk, dv, dz1, x1, g2, norm_pre[1:2], wt)

    blk = lambda n: n
    zero = lambda n: 0
    slab_t = d + d // N_CHIPS
    whole_in = pl.BlockSpec((None, d, d), lambda n, kk: (n, 0, 0))
    gt = _wgrad(h1, du1, "wgrad_att_in", N_CHIPS, d, d, zero, blk, (N_CHIPS, slab_t, d), whole_in)
    gt = _wgrad(gated1, dy1, "wgrad_att_out", 1, d, d, zero, zero, (N_CHIPS, slab_t, d),
                pl.BlockSpec((N_CHIPS, d // N_CHIPS, d), lambda n, kk: (0, N_CHIPS, 0)), into=gt)

    grad_x, du0, gated0, dy0, dmm, dgpre0, dgpost0, dscale = _l0_bwd(
        dx1, x2d, y0, mixed, z0, norm_pre[0:1], norm_post[0:1], pool_scale, wp, wg)

    swap_rider = lambda part: (functools.partial(_SwapPlan, send_half=_keep_own_half), part, part.shape, (1, 1))
    scatter_rider = lambda send: (_ScatterPlan, send, (3,) + send.shape[1:], (3, 3))
    gi, got_t = _wgrad(h0, du0, "wgrad_pool_in", N_CHIPS, d, d, zero, blk, (N_CHIPS, d, d), whole_in,
                       riders=[swap_rider(gt)])
    mine_t, send_t = _sum_siblings(gt, got_t, "sum_siblings_att")
    go, recv_t, got_i = _wgrad(gated0, dy0, "wgrad_pool_out", N_CHIPS, gw, d, blk, zero, (N_CHIPS, gw, d),
                               pl.BlockSpec((None, gw, d), lambda n, kk: (n, 0, 0)),
                               riders=[scatter_rider(send_t), swap_rider(gi)])
    mine_i, send_i = _sum_siblings(gi, got_i, "sum_siblings_pool_in")
    gg, recv_i, got_o = _wgrad(mixed, dmm, "wgrad_pool_group", N_CHIPS, gw, gw, blk, blk, (N_CHIPS, N_CHIPS, q, gw),
                               pl.BlockSpec((N_CHIPS, None, q, gw), lambda n, kk: (0, n, 0, 0)),
                               riders=[scatter_rider(send_i), swap_rider(go)])
    gg = gg.reshape(N_CHIPS, N_CHIPS * q, gw)
    mine_o, send_o = _sum_siblings(go, got_o, "sum_siblings_pool_out")
    (got_g,) = _swap_halves([gg], "swap_core_partials_group", _keep_own_half, in_place=False)
    mine_g, send_g = _sum_siblings(gg, got_g, "sum_siblings_group")
    d_rel, recv_o, recv_g = _rel_bias_grad(dbias, riders=[scatter_rider(send_o), scatter_rider(send_g)])
    red_t = _sum_chips(mine_t, recv_t, "sum_chips_att")
    red_i = _sum_chips(mine_i, recv_i, "sum_chips_pool_in")
    red_o = _sum_chips(mine_o, recv_o, "sum_chips_pool_out")
    red_g = _sum_chips(mine_g, recv_g, "sum_chips_group")
    grad_t, grad_i, grad_o, grad_g = _swap_halves([red_t, red_i, red_o, red_g], "swap_reduced_halves",
                                                  lambda c: c, in_place=True)

    small_shapes = [norm_pre.shape, norm_post.shape, pool_scale.shape, att_rel_bias.shape]
    part = _pack_small(d, [jnp.concatenate([dgpre0, dgpre1], axis=0), jnp.concatenate([dgpost0, dgpost1], axis=0),
                           dscale, d_rel, sq])
    total = _allreduce_small(part)
    loss = (0.5 / d) * jnp.sum(total[-1])
    w_small = _pack_small(d, [norm_pre, norm_post, pool_scale, att_rel_bias, jnp.zeros((d,), F32)])
    m_small = _pack_small(d, [m_norm_pre, m_norm_post, m_pool_scale, m_att_rel_bias, jnp.zeros((d,), F32)])
    v_small = _pack_small(d, [v_norm_pre, v_norm_post, v_pool_scale, v_att_rel_bias, jnp.ones((d,), F32)])
    small_out = [_unpack_small(a, d, small_shapes) for a in _adamw(w_small, total, m_small, v_small, "adamw_small")]

    big = {}
    for name, w, m, v, grad, row0 in [("pool_w_in", pool_w_in, m_pool_w_in, v_pool_w_in, grad_i, 0),
                                      ("pool_w_out", pool_w_out, m_pool_w_out, v_pool_w_out, grad_o, 0),
                                      ("att_w_in", att_w_in, m_att_w_in, v_att_w_in, grad_t, 0),
                                      ("att_w_out", att_w_out, m_att_w_out, v_att_w_out, grad_t, d)]:
        outs = _adamw(w[0], grad, m[0], v[0], "adamw_" + name, g_row0=row0)
        big[name] = [a.reshape(w.shape) for a in outs]
    outs = _adamw(pack_g(pool_w_group), grad_g, pack_g(m_pool_w_group), pack_g(v_pool_w_group), "adamw_pool_w_group")
    big["pool_w_group"] = [a.reshape(pool_w_group.shape) for a in outs]

    def leaf(kind):
        return (small_out[kind][0], small_out[kind][1], big["pool_w_in"][kind], big["pool_w_group"][kind],
                small_out[kind][2], big["pool_w_out"][kind], big["att_w_in"][kind], small_out[kind][3],
                big["att_w_out"][kind])

    return (loss, grad_x.reshape(x.shape), *leaf(0), *leaf(1), *leaf(2), *leaf(3))
```

```python
import functools

import jax
import jax.numpy as jnp
from jax import lax
from jax.experimental import pallas as pl
from jax.experimental.pallas import tpu as pltpu

F32 = jnp.float32
BF16 = jnp.bfloat16

RMS_EPS = 1e-6
CHUNK = 64
HEAD_DIM = 64
LEFT_CHUNKS = 8
PAD = LEFT_CHUNKS * CHUNK
MAX_REL = 256
POOL_WINDOWS = (2, 4, 8, 16)
HALO = 16
TS = 256
TS1 = 512
QB = 256
KB = QB + PAD
HEAD_PAIR = 2 * HEAD_DIM
NEG = -1e30
N_CHIPS = 4
N_DEV = 8

ADAM_LR = 0.001
ADAM_B1 = 0.9
ADAM_B2 = 0.999
ADAM_EPS = 1e-08
ADAM_WD = 0.01
ADAM_STEP = 10

VMEM_LIMIT = 56 * 1024 * 1024
MESH = pl.DeviceIdType.MESH
ANY = pl.BlockSpec(memory_space=pl.ANY)


def _dot(a, b):
    return jnp.dot(a, b, preferred_element_type=F32)


def _dot_nt(a, b):
    return lax.dot_general(a, b, (((1,), (1,)), ((), ())), preferred_element_type=F32)


def _dot_tn(a, b):
    return lax.dot_general(a, b, (((0,), (0,)), ((), ())), preferred_element_type=F32)


def _params(sem, limit=VMEM_LIMIT):
    return pltpu.CompilerParams(dimension_semantics=sem, vmem_limit_bytes=limit)


def _rms(x):
    return lax.rsqrt(jnp.mean(x * x, axis=-1, keepdims=True) + RMS_EPS)


def _rms_bwd(dyh, xh, r):
    return r * (dyh - xh * jnp.mean(dyh * xh, axis=-1, keepdims=True))


def _silu_parts(z):
    sig = jax.nn.sigmoid(z)
    return z * sig, sig * (1.0 + z * (1.0 - sig))


def _inv_count(tile, rows, w):
    t = tile * rows + lax.broadcasted_iota(jnp.int32, (rows, 1), 0)
    return 1.0 / jnp.minimum(t + 1, w).astype(F32)


def _start_weight_copies(copies):
    for c in copies:
        c.start()
    for c in copies:
        c.wait()


def _l0_fwd(x, g_pre, g_post, scale, wa, wg, next_shard):
    s_len, d = x.shape
    pw, gw = 2 * d, d // 2
    q = gw // N_CHIPS
    ts = TS1
    nt = s_len // ts
    assert nt >= 2

    def body(x_ref, gpre_ref, gpost_ref, sc_ref, wa_ref, wg_ref, shard_ref,
             x1_ref, h_ref, mixed_ref, z_ref, y_ref, next_ref,
             win_s, wout_s, wg_s, halo_s, sems, *gather_sems):
        i = pl.program_id(0)

        @pl.when(i == 0)
        def _():
            _GatherPlan([shard_ref], [next_ref], *gather_sems).send()
            copies = [pltpu.make_async_copy(wa_ref.at[:, pl.ds(0, d), :], win_s, sems.at[0]),
                      pltpu.make_async_copy(wa_ref.at[:, pl.ds(d, gw), :], wout_s, sems.at[1])]
            copies += [pltpu.make_async_copy(wg_ref.at[j], wg_s.at[:, pl.ds(j * q, q), :], sems.at[2 + j])
                       for j in range(N_CHIPS)]
            _start_weight_copies(copies)
            halo_s[...] = jnp.zeros_like(halo_s)

        @pl.when(i == nt // 2)
        def _():
            _GatherPlan([shard_ref], [next_ref], *gather_sems).pass_on()

        @pl.when(i == nt - 1)
        def _():
            _GatherPlan([shard_ref], [next_ref], *gather_sems).finish()

        xt = x_ref[...]
        h = ((xt * _rms(xt)) * gpre_ref[...]).astype(BF16)
        h_ref[...] = h
        a_blocks = [_dot(h, win_s[0]), _dot(h, win_s[1])]
        y = jnp.zeros((ts, d), F32)
        for g, w in enumerate(POOL_WINDOWS):
            a_g = a_blocks[g // 2][:, (g % 2) * gw:(g % 2 + 1) * gw]
            ext = jnp.concatenate([halo_s[g], a_g], axis=0)
            shift = 1
            while shift < w:
                ext = ext + pltpu.roll(ext, shift, 0)
                shift *= 2
            mixed = (ext[HALO:] * _inv_count(i, ts, w) - a_g).astype(BF16)
            halo_s[g] = a_g[ts - HALO:]
            mixed_ref[:, g * gw:(g + 1) * gw] = mixed
            z_g = _dot(h, win_s[2 + g // 2, :, (g % 2) * gw:(g % 2 + 1) * gw])
            z_ref[:, g * gw:(g + 1) * gw] = z_g
            ms = _dot(mixed, wg_s[g]) * sc_ref[:, g * gw:(g + 1) * gw]
            gated = (ms * _silu_parts(z_g)[0]).astype(BF16)
            y = y + _dot(gated, wout_s[g])
        y_ref[...] = y
        x1_ref[...] = xt + (y * _rms(y)) * gpost_ref[...]

    tile = lambda wdt: pl.BlockSpec((ts, wdt), lambda i: (i, 0))
    row = lambda wdt: pl.BlockSpec((1, wdt), lambda i: (0, 0))
    return pl.pallas_call(
        body, name="l0_fwd", grid=(nt,),
        in_specs=[tile(d), row(d), row(d), row(pw), ANY, ANY, ANY],
        out_specs=[tile(d), tile(d), tile(pw), tile(pw), tile(d), ANY],
        out_shape=[jax.ShapeDtypeStruct((s_len, d), F32), jax.ShapeDtypeStruct((s_len, d), BF16),
                   jax.ShapeDtypeStruct((s_len, pw), BF16), jax.ShapeDtypeStruct((s_len, pw), F32),
                   jax.ShapeDtypeStruct((s_len, d), F32),
                   jax.ShapeDtypeStruct((N_CHIPS,) + next_shard.shape, next_shard.dtype)],
        scratch_shapes=[pltpu.VMEM((N_CHIPS, d, d), BF16), pltpu.VMEM((N_CHIPS, gw, d), BF16),
                        pltpu.VMEM((N_CHIPS, gw, gw), BF16), pltpu.VMEM((N_CHIPS, HALO, gw), F32),
                        pltpu.SemaphoreType.DMA((2 + N_CHIPS,))] + _gather_sems(1),
        compiler_params=_params(("arbitrary",)),
    )(x, g_pre, g_post, scale, wa, wg, next_shard)


def _l0_bwd(dx1, x, y, mixed, z, g_pre, g_post, scale, wa, wg):
    s_len, d = x.shape
    pw, gw = 2 * d, d // 2
    q = gw // N_CHIPS
    nt = s_len // TS

    def body(dx1_ref, x_ref, y_ref, mixed_ref, z_ref, gpre_ref, gpost_ref, sc_ref, wa_ref, wg_ref,
             gx_ref, du_ref, gated_ref, dy_ref, dmm_ref, dgpre_ref, dgpost_ref, dsc_ref,
             win_s, wout_s, wg_s, halo_s, sems):
        i = pl.program_id(0)
        tile = nt - 1 - i

        @pl.when(i == 0)
        def _():
            copies = [pltpu.make_async_copy(wa_ref.at[:, pl.ds(0, d), :], win_s, sems.at[0]),
                      pltpu.make_async_copy(wa_ref.at[:, pl.ds(d, gw), :], wout_s, sems.at[1])]
            copies += [pltpu.make_async_copy(wg_ref.at[j], wg_s.at[:, pl.ds(j * q, q), :], sems.at[2 + j])
                       for j in range(N_CHIPS)]
            _start_weight_copies(copies)
            halo_s[...] = jnp.zeros_like(halo_s)
            dgpre_ref[...] = jnp.zeros_like(dgpre_ref)
            dgpost_ref[...] = jnp.zeros_like(dgpost_ref)
            dsc_ref[...] = jnp.zeros_like(dsc_ref)

        g_in = dx1_ref[...]
        yt = y_ref[...]
        r_y = _rms(yt)
        yh = yt * r_y
        dgpost_ref[...] += jnp.sum(g_in * yh, axis=0, keepdims=True)
        dy = _rms_bwd(g_in * gpost_ref[...], yh, r_y).astype(BF16)
        dy_ref[...] = dy
        dh = jnp.zeros((TS, d), F32)

        for g, w in enumerate(POOL_WINDOWS):
            cols = slice(g * gw, (g + 1) * gw)
            dgated = _dot_nt(dy, wout_s[g])
            mm = _dot(mixed_ref[:, cols], wg_s[g])
            sc = sc_ref[:, cols]
            ms = mm * sc
            z_g = z_ref[:, cols]
            sz, dsz = _silu_parts(z_g)
            gated_ref[:, cols] = (ms * sz).astype(BF16)
            dms = dgated * sz
            dz = (dgated * ms * dsz).astype(BF16)
            dsc_ref[:, cols] += jnp.sum(dms * mm, axis=0, keepdims=True)
            dmm = (dms * sc).astype(BF16)
            dmm_ref[:, cols] = dmm
            dmixed = _dot_nt(dmm, wg_s[g])
            e = dmixed * _inv_count(tile, TS, w)
            ext = jnp.concatenate([e, halo_s[g]], axis=0)
            shift = 1
            while shift < w:
                ext = ext + pltpu.roll(ext, TS + HALO - shift, 0)
                shift *= 2
            da = (ext[:TS] - dmixed).astype(BF16)
            halo_s[g] = e[:HALO]
            du_ref[:, cols] = da
            du_ref[:, pw + g * gw:pw + (g + 1) * gw] = dz
            wa_blk = win_s[g // 2, :, (g % 2) * gw:(g % 2 + 1) * gw]
            wz_blk = win_s[2 + g // 2, :, (g % 2) * gw:(g % 2 + 1) * gw]
            dh = dh + _dot_nt(da, wa_blk) + _dot_nt(dz, wz_blk)
        xt = x_ref[...]
        r_x = _rms(xt)
        xh = xt * r_x
        dgpre_ref[...] += jnp.sum(dh * xh, axis=0, keepdims=True)
        gx_ref[...] = g_in + _rms_bwd(dh * gpre_ref[...], xh, r_x)

    tile_spec = lambda wdt: pl.BlockSpec((TS, wdt), lambda i: (nt - 1 - i, 0))
    row = lambda wdt: pl.BlockSpec((1, wdt), lambda i: (0, 0))
    return pl.pallas_call(
        body, name="l0_bwd", grid=(nt,),
        in_specs=[tile_spec(d), tile_spec(d), tile_spec(d), tile_spec(pw), tile_spec(pw),
                  row(d), row(d), row(pw), ANY, ANY],
        out_specs=[tile_spec(d), tile_spec(2 * pw), tile_spec(pw), tile_spec(d), tile_spec(pw),
                   row(d), row(d), row(pw)],
        out_shape=[jax.ShapeDtypeStruct((s_len, d), F32), jax.ShapeDtypeStruct((s_len, 2 * pw), BF16),
                   jax.ShapeDtypeStruct((s_len, pw), BF16), jax.ShapeDtypeStruct((s_len, d), BF16),
                   jax.ShapeDtypeStruct((s_len, pw), BF16),
                   jax.ShapeDtypeStruct((1, d), F32), jax.ShapeDtypeStruct((1, d), F32),
                   jax.ShapeDtypeStruct((1, pw), F32)],
        scratch_shapes=[pltpu.VMEM((N_CHIPS, d, d), BF16), pltpu.VMEM((N_CHIPS, gw, d), BF16),
                        pltpu.VMEM((N_CHIPS, gw, gw), BF16), pltpu.VMEM((N_CHIPS, HALO, gw), F32),
                        pltpu.SemaphoreType.DMA((2 + N_CHIPS,))],
        compiler_params=_params(("arbitrary",)),
    )(dx1, x, y, mixed, z, g_pre, g_post, scale, wa, wg)


def _l1_inproj(x1, g_pre, wa):
    s_len, d = x1.shape
    nt = s_len // TS1
    npad = PAD // TS1

    def body(x_ref, gpre_ref, wa_ref, h_ref, q_ref, k_ref, v_ref, z_ref, win_s, sem):
        i = pl.program_id(0)

        @pl.when(i == 0)
        def _():
            _start_weight_copies([pltpu.make_async_copy(wa_ref.at[:, pl.ds(0, d), :], win_s, sem.at[0])])

        @pl.when(i < npad)
        def _():
            k_ref[...] = jnp.zeros_like(k_ref)
            v_ref[...] = jnp.zeros_like(v_ref)

        @pl.when(i >= npad)
        def _():
            xt = x_ref[...]
            h = ((xt * _rms(xt)) * gpre_ref[...]).astype(BF16)
            h_ref[...] = h
            q_ref[...] = _dot(h, win_s[0]).astype(BF16)
            k_ref[...] = _dot(h, win_s[1]).astype(BF16)
            v_ref[...] = _dot(h, win_s[2]).astype(BF16)
            z_ref[...] = _dot(h, win_s[3])

    tile = pl.BlockSpec((TS1, d), lambda i: (jnp.maximum(i - npad, 0), 0))
    padded = pl.BlockSpec((TS1, d), lambda i: (i, 0))
    return pl.pallas_call(
        body, name="l1_inproj", grid=(nt + npad,),
        in_specs=[tile, pl.BlockSpec((1, d), lambda i: (0, 0)), ANY],
        out_specs=[tile, tile, padded, padded, tile],
        out_shape=[jax.ShapeDtypeStruct((s_len, d), BF16), jax.ShapeDtypeStruct((s_len, d), BF16),
                   jax.ShapeDtypeStruct((PAD + s_len, d), BF16), jax.ShapeDtypeStruct((PAD + s_len, d), BF16),
                   jax.ShapeDtypeStruct((s_len, d), F32)],
        scratch_shapes=[pltpu.VMEM((N_CHIPS, d, d), BF16), pltpu.SemaphoreType.DMA((1,))],
        compiler_params=_params(("arbitrary",)),
    )(x1, g_pre, wa)


SKEW = QB + KB
N_CLIPPED = QB + PAD - MAX_REL + 1
assert QB <= MAX_REL


def _bias_tiles(rel_bias):
    nh = rel_bias.shape[0]
    by_column = jnp.concatenate([jnp.broadcast_to(rel_bias[:, 2 * MAX_REL:], (nh, N_CLIPPED)),
                                 jnp.flip(rel_bias[:, MAX_REL + 1 - QB:2 * MAX_REL], axis=1)], axis=1).astype(F32)

    def body(col_ref, out_ref):
        rows = jnp.broadcast_to(col_ref[pl.ds(pl.program_id(0), 1), :], (QB, SKEW))
        tile = pltpu.roll(rows, SKEW - QB, 1, stride=1, stride_axis=0)[:, :KB]
        i = lax.broadcasted_iota(jnp.int32, (QB, KB), 0)
        j = lax.broadcasted_iota(jnp.int32, (QB, KB), 1)
        first = (i // CHUNK) * CHUNK
        out_ref[...] = jnp.where((j >= first) & (j < first + PAD + CHUNK), tile, NEG)

    return pl.pallas_call(
        body, name="bias_tiles", grid=(nh,),
        in_specs=[pl.BlockSpec((nh, SKEW), lambda h: (0, 0))],
        out_specs=pl.BlockSpec((None, QB, KB), lambda h: (h, 0, 0)),
        out_shape=jax.ShapeDtypeStruct((nh, QB, KB), F32),
        compiler_params=_params(("arbitrary",)),
    )(by_column)


ROWS = 16


BLOCKS_PER_STEP = 8


def _row_blocks(rows=ROWS):
    return [pl.ds(r * rows, rows) for r in range(QB // rows)]


def _attn_fwd(q, k, v, bias):
    s_len, d = q.shape
    nhp = d // HEAD_PAIR
    per_step = min(BLOCKS_PER_STEP, s_len // QB)
    nq = s_len // (QB * per_step)

    def body(q_ref, k_ref, v_ref, b_ref, o_ref, lse_ref, s_s, p_s, l_s):
        head_of_lane = lax.broadcasted_iota(jnp.int32, (1, HEAD_PAIR), 1) // HEAD_DIM

        units = [(u, hh) for u in range(per_step) for hh in range(2)]

        def band(u):
            return pl.ds(pl.multiple_of((pl.program_id(1) * per_step + u) * QB, QB), KB)

        def issue_scores(n):
            u, hh = units[n]
            qt = q_ref[pl.ds(u * QB, QB), :] * (HEAD_DIM ** -0.5)
            s_s[n % 2] = _dot_nt(jnp.where(head_of_lane == hh, qt, jnp.zeros_like(qt)), k_ref[band(u), :])

        issue_scores(0)
        outs, lses = [], []
        for n, (u, hh) in enumerate(units):
            if n + 1 < len(units):
                issue_scores(n + 1)
            first_key = (pl.program_id(1) * per_step + u) * QB
            colvalid = (first_key + lax.broadcasted_iota(jnp.int32, (1, KB), 1)) >= PAD
            for rows in _row_blocks():
                s = jnp.where(colvalid, s_s[n % 2, rows, :] + b_ref[hh, rows, :], NEG)
                m = jnp.max(s, axis=-1, keepdims=True)
                e = jnp.exp(s - m)
                l = jnp.sum(e, axis=-1, keepdims=True)
                p_s[n % 2, rows, :] = (e * (1.0 / l)).astype(BF16)
                l_s[n % 2, rows, :] = m + jnp.log(l)
            outs.append(_dot(p_s[n % 2], v_ref[band(u), :]))
            lses.append(l_s[n % 2])
            if hh == 1:
                here = pl.ds(u * QB, QB)
                o_ref[here, :] = jnp.where(head_of_lane == 0, outs[-2], outs[-1])
                lse_ref[here, :] = jnp.where(head_of_lane == 0, lses[-2], lses[-1])

    blk = pl.BlockSpec((per_step * QB, HEAD_PAIR), lambda hp, b: (b, hp))
    whole = pl.BlockSpec((PAD + s_len, HEAD_PAIR), lambda hp, b: (0, hp))
    return pl.pallas_call(
        body, name="attn_fwd", grid=(nhp, nq),
        in_specs=[blk, whole, whole, pl.BlockSpec((2, QB, KB), lambda hp, b: (hp, 0, 0))],
        out_specs=[blk, blk],
        out_shape=[jax.ShapeDtypeStruct((s_len, d), F32), jax.ShapeDtypeStruct((s_len, d), F32)],
        scratch_shapes=[pltpu.VMEM((2, QB, KB), F32), pltpu.VMEM((2, QB, KB), BF16), pltpu.VMEM((2, QB, 1), F32)],
        compiler_params=_params(("arbitrary", "arbitrary")),
    )(q, k, v, bias)


def _attn_bwd(q, k, v, bias, d_o, o, lse):
    s_len, d = q.shape
    nhp = d // HEAD_PAIR
    per_step = min(BLOCKS_PER_STEP, s_len // QB)
    nq = s_len // (QB * per_step)
    qk_scale = HEAD_DIM ** -0.5

    def body(q_ref, k_ref, v_ref, b_ref, do_ref, o_ref, lse_ref, dq_ref, dk_out_ref, dv_out_ref, db_ref,
             s_s, dp_s, p_s, ds_s, lse_s, delta_s, dk_ref, dv_ref):
        @pl.when(pl.program_id(1) == 0)
        def _():
            dk_ref[...] = jnp.zeros_like(dk_ref)
            dv_ref[...] = jnp.zeros_like(dv_ref)
            db_ref[...] = jnp.zeros_like(db_ref)

        head_of_lane = lax.broadcasted_iota(jnp.int32, (1, HEAD_PAIR), 1) // HEAD_DIM

        units = [(u, hh) for u in range(per_step) for hh in range(2)]

        def band(u):
            return pl.ds(pl.multiple_of((pl.program_id(1) * per_step + u) * QB, QB), KB)

        def masked(ref, u, hh, factor=None):
            x = ref[pl.ds(u * QB, QB), :]
            x = x if factor is None else x * factor
            return jnp.where(head_of_lane == hh, x, jnp.zeros_like(x))

        def issue_tiles(n):
            u, hh = units[n]
            here = pl.ds(u * QB, QB)
            do_o = do_ref[here, :].astype(F32) * o_ref[here, :]
            delta_s[n % 2] = jnp.sum(jnp.where(head_of_lane == hh, do_o, 0.0), axis=-1, keepdims=True)
            lse_s[n % 2] = lse_ref[here, hh * HEAD_DIM:hh * HEAD_DIM + 1]
            s_s[n % 2] = _dot_nt(masked(q_ref, u, hh, qk_scale), k_ref[band(u), :])
            dp_s[n % 2] = _dot_nt(masked(do_ref, u, hh), v_ref[band(u), :])

        issue_tiles(0)
        dq_heads, dk_band, dv_band = [], None, None
        for n, (u, hh) in enumerate(units):
            if n + 1 < len(units):
                issue_tiles(n + 1)
            first_key = (pl.program_id(1) * per_step + u) * QB
            colvalid = (first_key + lax.broadcasted_iota(jnp.int32, (1, KB), 1)) >= PAD
            for rows in _row_blocks():
                t = jnp.where(colvalid, s_s[n % 2, rows, :] + b_ref[hh, rows, :] - lse_s[n % 2, rows, :], NEG)
                p = jnp.exp(t)
                ds = p * (dp_s[n % 2, rows, :] - delta_s[n % 2, rows, :])
                db_ref[hh, rows, :] += ds
                p_s[n % 2, rows, :] = p.astype(BF16)
                ds_s[n % 2, rows, :] = ds.astype(BF16)
            q_m = masked(q_ref, u, hh, qk_scale)
            dv_unit = _dot_tn(p_s[n % 2], masked(do_ref, u, hh))
            dq_heads.append(_dot(ds_s[n % 2], k_ref[band(u), :]) * qk_scale)
            dk_unit = _dot_tn(ds_s[n % 2], q_m)
            if hh == 0:
                dk_band, dv_band = dk_unit, dv_unit
            else:
                dq_ref[pl.ds(u * QB, QB), :] = jnp.where(head_of_lane == 0, dq_heads[-2], dq_heads[-1]).astype(BF16)
                dk_ref[band(u), :] += dk_band + dk_unit
                dv_ref[band(u), :] += dv_band + dv_unit

        @pl.when(pl.program_id(1) == nq - 1)
        def _():
            dk_out_ref[...] = dk_ref[...].astype(BF16)
            dv_out_ref[...] = dv_ref[...].astype(BF16)

    blk = pl.BlockSpec((per_step * QB, HEAD_PAIR), lambda hp, b: (b, hp))
    whole = pl.BlockSpec((PAD + s_len, HEAD_PAIR), lambda hp, b: (0, hp))
    btile = pl.BlockSpec((2, QB, KB), lambda hp, b: (hp, 0, 0))
    return pl.pallas_call(
        body, name="attn_bwd", grid=(nhp, nq),
        in_specs=[blk, whole, whole, btile, blk, blk, blk],
        out_specs=[blk, whole, whole, btile],
        out_shape=[jax.ShapeDtypeStruct((s_len, d), BF16),
                   jax.ShapeDtypeStruct((PAD + s_len, d), BF16), jax.ShapeDtypeStruct((PAD + s_len, d), BF16),
                   jax.ShapeDtypeStruct(bias.shape, F32)],
        scratch_shapes=[pltpu.VMEM((2, QB, KB), F32), pltpu.VMEM((2, QB, KB), F32),
                        pltpu.VMEM((2, QB, KB), BF16), pltpu.VMEM((2, QB, KB), BF16),
                        pltpu.VMEM((2, QB, 1), F32), pltpu.VMEM((2, QB, 1), F32),
                        pltpu.VMEM((PAD + s_len, HEAD_PAIR), F32), pltpu.VMEM((PAD + s_len, HEAD_PAIR), F32)],
        compiler_params=_params(("arbitrary", "arbitrary")),
    )(q, k, v, bias, d_o, o, lse)


def _rel_bias_grad(db, riders=()):
    nh = db.shape[0]
    n_ride = len(riders)

    def body(*refs):
        db_ref, out_ref = refs[0], refs[1 + n_ride]

        def plans():
            return [rider[0]([refs[1 + r]], [refs[2 + n_ride + r]],
                             *refs[2 + 2 * n_ride + 2 * r:4 + 2 * n_ride + 2 * r]) for r, rider in enumerate(riders)]

        if riders:
            @pl.when(pl.program_id(0) == 0)
            def _():
                for plan in plans():
                    plan.send()

            @pl.when(pl.program_id(0) == nh - 1)
            def _():
                for plan in plans():
                    plan.finish()

        i0 = lax.broadcasted_iota(jnp.int32, (QB, QB), 0)
        i1 = lax.broadcasted_iota(jnp.int32, (QB, QB), 1)
        exchange = jnp.where(i0 + i1 == QB - 1, 1.0, 0.0).astype(BF16)
        rest = db_ref[...]
        flipped = jnp.zeros((QB, KB), F32)
        for _ in range(3):
            piece = rest.astype(BF16)
            flipped = flipped + _dot(exchange, piece)
            rest = rest - piece.astype(F32)
        m = jnp.concatenate([flipped, jnp.zeros((QB, SKEW - KB), F32)], axis=1)
        diag = jnp.sum(pltpu.roll(m, 1, 1, stride=1, stride_axis=0), axis=0, keepdims=True)
        c = lax.broadcasted_iota(jnp.int32, (1, SKEW), 1)
        clipped = jnp.sum(jnp.where(c < N_CLIPPED, diag, 0.0), axis=1, keepdims=True)
        out_ref[...] = jnp.where(c == 0, clipped, diag)

    results = pl.pallas_call(
        body, name="bias_diagonals", grid=(nh,),
        in_specs=[pl.BlockSpec((None, QB, KB), lambda h: (h, 0, 0))] + [ANY] * n_ride,
        out_specs=[pl.BlockSpec((None, 1, SKEW), lambda h: (h, 0, 0))] + [ANY] * n_ride,
        out_shape=[jax.ShapeDtypeStruct((nh, 1, SKEW), F32)]
                  + [jax.ShapeDtypeStruct(rider[2], rider[1].dtype) for rider in riders],
        scratch_shapes=[pltpu.SemaphoreType.DMA((n,)) for rider in riders for n in rider[3]],
        compiler_params=_params(("arbitrary",)),
    )(db, *[rider[1] for rider in riders])
    diag = results[0][:, 0]
    d_rel = jnp.concatenate([jnp.zeros((nh, MAX_REL + 1 - QB), F32), jnp.flip(diag[:, N_CLIPPED:], axis=1),
                             diag[:, :1]], axis=1)
    return (d_rel, *results[1:]) if riders else d_rel


def _l1_out(o, z, x1, target, g_post, wa):
    s_len, d = o.shape
    nt = s_len // TS1
    slab = d // N_CHIPS

    def body(o_ref, z_ref, x1_ref, t_ref, gpost_ref, wa_ref,
             g2_ref, gated_ref, dy_ref, do_ref, dz_ref, dgpost_ref, sq_ref, wout_s, sems):
        i = pl.program_id(0)

        @pl.when(i == 0)
        def _():
            _start_weight_copies([
                pltpu.make_async_copy(wa_ref.at[j, pl.ds(d, slab), :],
                                      wout_s.at[pl.ds(j * slab, slab), :], sems.at[j])
                for j in range(N_CHIPS)])
            dgpost_ref[...] = jnp.zeros_like(dgpost_ref)
            sq_ref[...] = jnp.zeros_like(sq_ref)

        ot = o_ref[...]
        sz, dsz = _silu_parts(z_ref[...])
        gated = (ot * sz).astype(BF16)
        gated_ref[...] = gated
        y = _dot(gated, wout_s[...])
        r_y = _rms(y)
        yh = y * r_y
        err = x1_ref[...] + yh * gpost_ref[...] - t_ref[...]
        sq_ref[...] += jnp.sum(err * err, axis=0, keepdims=True)
        g2 = err * (1.0 / d)
        g2_ref[...] = g2
        dgpost_ref[...] += jnp.sum(g2 * yh, axis=0, keepdims=True)
        dy = _rms_bwd(g2 * gpost_ref[...], yh, r_y).astype(BF16)
        dy_ref[...] = dy
        dgated = _dot_nt(dy, wout_s[...])
        do_ref[...] = (dgated * sz).astype(BF16)
        dz_ref[...] = (dgated * ot * dsz).astype(BF16)

    tile = pl.BlockSpec((TS1, d), lambda i: (i, 0))
    row = pl.BlockSpec((1, d), lambda i: (0, 0))
    return pl.pallas_call(
        body, name="l1_out", grid=(nt,),
        in_specs=[tile, tile, tile, tile, row, ANY],
        out_specs=[tile, tile, tile, tile, tile, row, row],
        out_shape=[jax.ShapeDtypeStruct((s_len, d), F32)] + [jax.ShapeDtypeStruct((s_len, d), BF16)] * 4
                  + [jax.ShapeDtypeStruct((1, d), F32)] * 2,
        scratch_shapes=[pltpu.VMEM((d, d), BF16), pltpu.SemaphoreType.DMA((N_CHIPS,))],
        compiler_params=_params(("arbitrary",)),
    )(o, z, x1, target, g_post, wa)


def _l1_in_bwd(dq, dk, dv, dz, x1, g2, g_pre, wa):
    s_len, d = x1.shape
    nt = s_len // TS1
    npad = PAD // TS1

    def body(dq_ref, dk_ref, dv_ref, dz_ref, x_ref, g2_ref, gpre_ref, wa_ref,
             dx_ref, du_ref, dgpre_ref, win_s, sem):
        i = pl.program_id(0)

        @pl.when(i == 0)
        def _():
            _start_weight_copies([pltpu.make_async_copy(wa_ref.at[:, pl.ds(0, d), :], win_s, sem.at[0])])
            dgpre_ref[...] = jnp.zeros_like(dgpre_ref)

        parts = [dq_ref[...], dk_ref[...], dv_ref[...], dz_ref[...]]
        dh = jnp.zeros((TS1, d), F32)
        for j, part in enumerate(parts):
            du_ref[:, j * d:(j + 1) * d] = part
            dh = dh + _dot_nt(part, win_s[j])
        xt = x_ref[...]
        r_x = _rms(xt)
        xh = xt * r_x
        dgpre_ref[...] += jnp.sum(dh * xh, axis=0, keepdims=True)
        dx_ref[...] = g2_ref[...] + _rms_bwd(dh * gpre_ref[...], xh, r_x)

    tile = pl.BlockSpec((TS1, d), lambda i: (i, 0))
    padded = pl.BlockSpec((TS1, d), lambda i: (i + npad, 0))
    row = pl.BlockSpec((1, d), lambda i: (0, 0))
    return pl.pallas_call(
        body, name="l1_in_bwd", grid=(nt,),
        in_specs=[tile, padded, padded, tile, tile, tile, row, ANY],
        out_specs=[tile, pl.BlockSpec((TS1, 4 * d), lambda i: (i, 0)), row],
        out_shape=[jax.ShapeDtypeStruct((s_len, d), F32), jax.ShapeDtypeStruct((s_len, 4 * d), BF16),
                   jax.ShapeDtypeStruct((1, d), F32)],
        scratch_shapes=[pltpu.VMEM((N_CHIPS, d, d), BF16), pltpu.SemaphoreType.DMA((1,))],
        compiler_params=_params(("arbitrary",)),
    )(dq, dk, dv, dz, x1, g2, g_pre, wa)


def _wgrad(a, b, name, n_out, a_width, b_width, a_block, b_block, out_shape, out_spec, into=None, riders=()):
    s_len = a.shape[0]
    tk = min(2048, s_len)
    nk = s_len // tk
    n_ride = len(riders)
    n_in = 2 + (into is not None) + n_ride

    def body(*refs):
        a_ref, b_ref = refs[:2]
        out_ref = refs[n_in]

        def plans():
            return [rider[0]([refs[n_in - n_ride + r]], [refs[n_in + 1 + r]],
                             *refs[n_in + 1 + n_ride + 2 * r:n_in + 3 + n_ride + 2 * r])
                    for r, rider in enumerate(riders)]

        if riders:
            @pl.when((pl.program_id(0) == 0) & (pl.program_id(1) == 0))
            def _():
                for plan in plans():
                    plan.send()

        @pl.when(pl.program_id(1) == 0)
        def _():
            out_ref[...] = jnp.zeros_like(out_ref)

        out_ref[...] += _dot_tn(a_ref[...], b_ref[...]).reshape(out_ref.shape)

        if riders:
            @pl.when((pl.program_id(0) == n_out - 1) & (pl.program_id(1) == nk - 1))
            def _():
                for plan in plans():
                    plan.finish()

    operands = [a, b] + ([into] if into is not None else []) + [rider[1] for rider in riders]
    results = pl.pallas_call(
        body, name=name, grid=(n_out, nk),
        in_specs=[pl.BlockSpec((tk, a_width), lambda n, kk: (kk, a_block(n))),
                  pl.BlockSpec((tk, b_width), lambda n, kk: (kk, b_block(n)))] + [ANY] * (n_in - 2),
        out_specs=[out_spec] + [ANY] * n_ride,
        out_shape=[jax.ShapeDtypeStruct(out_shape, F32)]
                  + [jax.ShapeDtypeStruct(rider[2], rider[1].dtype) for rider in riders],
        scratch_shapes=[pltpu.SemaphoreType.DMA((n,)) for rider in riders for n in rider[3]],
        input_output_aliases={2: 0} if into is not None else {},
        compiler_params=_params(("arbitrary", "arbitrary")),
    )(*operands)
    return results if riders else results[0]


def _place():
    x, y, c = lax.axis_index("x"), lax.axis_index("y"), lax.axis_index("c")
    others = [(1 - x, y), (x, 1 - y), (1 - x, 1 - y)]
    return x, y, c, others


class _GatherPlan:
    def __init__(self, src, dst, send, recv, fwd_send, fwd_recv, local):
        x, y, c, others = _place()
        me = 2 * x + y
        sibling = (x, y, 1 - c)

        def half(ref, cc):
            rows = ref.shape[0] // 2
            return ref.at[pl.ds(cc * rows, rows), :]

        self.mine = [pltpu.make_async_copy(src[a], dst[a].at[me], local.at[a]) for a in range(len(src))]
        self.first, self.passed, self.arrive, self.arrive_fwd = [], [], [], []
        for a in range(len(src)):
            for k, (ox, oy) in enumerate(others):
                sem = a * 3 + k
                self.first.append(pltpu.make_async_remote_copy(
                    src_ref=half(src[a], c), dst_ref=half(dst[a].at[me], c),
                    send_sem=send.at[sem], recv_sem=recv.at[sem], device_id=(ox, oy, c), device_id_type=MESH))
                theirs = half(dst[a].at[2 * ox + oy], c)
                self.arrive.append(pltpu.make_async_remote_copy(
                    src_ref=theirs, dst_ref=theirs, send_sem=send.at[sem], recv_sem=recv.at[sem],
                    device_id=(ox, oy, c), device_id_type=MESH))
                self.passed.append(pltpu.make_async_remote_copy(
                    src_ref=theirs, dst_ref=theirs, send_sem=fwd_send.at[sem], recv_sem=fwd_recv.at[sem],
                    device_id=sibling, device_id_type=MESH))
                other_half = half(dst[a].at[2 * ox + oy], 1 - c)
                self.arrive_fwd.append(pltpu.make_async_remote_copy(
                    src_ref=other_half, dst_ref=other_half, send_sem=fwd_send.at[sem], recv_sem=fwd_recv.at[sem],
                    device_id=sibling, device_id_type=MESH))

    def send(self):
        for cp in self.mine + self.first:
            cp.start()

    def pass_on(self):
        for got, onward in zip(self.arrive, self.passed):
            got.wait_recv()
            onward.start()

    def finish(self):
        for got in self.arrive_fwd:
            got.wait_recv()
        for cp in self.first + self.passed:
            cp.wait_send()
        for cp in self.mine:
            cp.wait()


def _gather_sems(n):
    return [pltpu.SemaphoreType.DMA((3 * n,))] * 4 + [pltpu.SemaphoreType.DMA((n,))]


def _gather_weights(shards):
    n = len(shards)

    def body(*refs):
        plan = _GatherPlan(refs[:n], refs[n:2 * n], *refs[2 * n:])
        plan.send()
        plan.pass_on()
        plan.finish()

    return pl.pallas_call(
        body, name="gather_weights",
        in_specs=[ANY] * n, out_specs=[ANY] * n,
        out_shape=[jax.ShapeDtypeStruct((N_CHIPS,) + s.shape, s.dtype) for s in shards],
        scratch_shapes=_gather_sems(n),
    )(*shards)


class _SwapPlan:
    def __init__(self, src, dst, send, recv, send_half):
        x, y, c, _ = _place()
        sibling = (x, y, 1 - c)
        self.out_copies, self.in_copies = [], []
        for a in range(len(src)):
            rows = src[a].shape[-2] // 2
            lead = (slice(None),) * (len(src[a].shape) - 2)
            going = lead + (pl.ds(send_half(c) * rows, rows), slice(None))
            coming = lead + (pl.ds((1 - send_half(c)) * rows, rows), slice(None))
            self.out_copies.append(pltpu.make_async_remote_copy(
                src_ref=src[a].at[going], dst_ref=dst[a].at[going], send_sem=send.at[a], recv_sem=recv.at[a],
                device_id=sibling, device_id_type=MESH))
            self.in_copies.append(pltpu.make_async_remote_copy(
                src_ref=src[a].at[coming], dst_ref=dst[a].at[coming], send_sem=send.at[a], recv_sem=recv.at[a],
                device_id=sibling, device_id_type=MESH))

    def send(self):
        for cp in self.out_copies:
            cp.start()

    def finish(self):
        for cp in self.in_copies:
            cp.wait_recv()
        for cp in self.out_copies:
            cp.wait_send()


def _keep_own_half(c):
    return 1 - c


def _swap_halves(arrays, name, send_half, in_place):
    n = len(arrays)

    def body(*refs):
        src, dst = refs[:n], refs[n:2 * n]
        plan = _SwapPlan(dst if in_place else src, dst, *refs[2 * n:], send_half=send_half)
        plan.send()
        plan.finish()

    return pl.pallas_call(
        body, name=name,
        in_specs=[ANY] * n, out_specs=[ANY] * n,
        out_shape=[jax.ShapeDtypeStruct(a.shape, a.dtype) for a in arrays],
        scratch_shapes=[pltpu.SemaphoreType.DMA((n,)), pltpu.SemaphoreType.DMA((n,))],
        input_output_aliases={a: a for a in range(n)} if in_place else {},
    )(*arrays)


class _ScatterPlan:
    def __init__(self, src, dst, send, recv):
        x, y, c, others = _place()
        self.out_copies, self.in_copies = [], []
        for a in range(len(src)):
            rows = src[a].shape[1] // 2
            mine = pl.ds(c * rows, rows)
            for k, (ox, oy) in enumerate(others):
                sem = a * 3 + k
                self.out_copies.append(pltpu.make_async_remote_copy(
                    src_ref=src[a].at[2 * ox + oy, mine, :], dst_ref=dst[a].at[k, mine, :],
                    send_sem=send.at[sem], recv_sem=recv.at[sem], device_id=(ox, oy, c), device_id_type=MESH))
                self.in_copies.append(pltpu.make_async_remote_copy(
                    src_ref=dst[a].at[k, mine, :], dst_ref=dst[a].at[k, mine, :],
                    send_sem=send.at[sem], recv_sem=recv.at[sem], device_id=(ox, oy, c), device_id_type=MESH))

    def send(self):
        for cp in self.out_copies:
            cp.start()

    def finish(self):
        for cp in self.in_copies:
            cp.wait_recv()
        for cp in self.out_copies:
            cp.wait_send()


def _allreduce_small(part):
    rows, width = part.shape

    def body(p_ref, out_ref, all_ref, send, recv):
        x, y, c, _ = _place()
        me = 4 * x + 2 * y + c
        all_ref[me] = p_ref[...]
        copies = []
        for k in range(1, N_DEV):
            px, py, pc = x ^ (k >> 2), y ^ ((k >> 1) & 1), c ^ (k & 1)
            copies.append(pltpu.make_async_remote_copy(
                src_ref=p_ref, dst_ref=all_ref.at[me], send_sem=send.at[k - 1], recv_sem=recv.at[k - 1],
                device_id=(px, py, pc), device_id_type=MESH))
        for cp in copies:
            cp.start()
        for cp in copies:
            cp.wait()
        total = all_ref[0]
        for k in range(1, N_DEV):
            total = total + all_ref[k]
        out_ref[...] = total

    return pl.pallas_call(
        body, name="allreduce_small",
        in_specs=[pl.BlockSpec(memory_space=pltpu.VMEM)],
        out_specs=pl.BlockSpec(memory_space=pltpu.VMEM),
        out_shape=jax.ShapeDtypeStruct((rows, width), F32),
        scratch_shapes=[pltpu.VMEM((N_DEV, rows, width), F32),
                        pltpu.SemaphoreType.DMA((N_DEV - 1,)), pltpu.SemaphoreType.DMA((N_DEV - 1,))],
    )(part)


def _row_tile(rows):
    t = min(rows, 256)
    while rows % t:
        t //= 2
    return t


def _core_and_chip():
    return jnp.stack([lax.axis_index("c"), 2 * lax.axis_index("x") + lax.axis_index("y")]).astype(jnp.int32)


def _sum_siblings(own, got, name):
    _, rows, cols = own.shape
    half = rows // 2
    t = _row_tile(half)
    nb = half // t

    def body(place_ref, own_ref, got_ref, mine_ref, out_ref):
        total = own_ref[...] + got_ref[...]
        out_ref[...] = total.astype(BF16)

        @pl.when(pl.program_id(1) == place_ref[1])
        def _():
            mine_ref[...] = total

    slab_blk = pl.BlockSpec((None, t, cols), lambda r, j, place: (j, place[0] * nb + r, 0))
    return pl.pallas_call(
        body, name=name,
        grid_spec=pltpu.PrefetchScalarGridSpec(
            num_scalar_prefetch=1, grid=(nb, N_CHIPS),
            in_specs=[slab_blk, slab_blk],
            out_specs=[pl.BlockSpec((t, cols), lambda r, j, place: (r, 0)), slab_blk]),
        out_shape=[jax.ShapeDtypeStruct((half, cols), F32), jax.ShapeDtypeStruct(own.shape, BF16)],
        compiler_params=_params(("arbitrary", "arbitrary")),
    )(_core_and_chip(), own, got)


def _sum_chips(mine, got, name):
    half, cols = mine.shape
    t = _row_tile(half)
    nb = half // t

    def body(place_ref, mine_ref, got_ref, out_ref):
        total = mine_ref[...]
        for k in range(3):
            total = total + got_ref[k].astype(F32)
        out_ref[...] = total

    return pl.pallas_call(
        body, name=name,
        grid_spec=pltpu.PrefetchScalarGridSpec(
            num_scalar_prefetch=1, grid=(nb,),
            in_specs=[pl.BlockSpec((t, cols), lambda r, place: (r, 0)),
                      pl.BlockSpec((3, t, cols), lambda r, place: (0, place[0] * nb + r, 0))],
            out_specs=pl.BlockSpec((t, cols), lambda r, place: (place[0] * nb + r, 0))),
        out_shape=jax.ShapeDtypeStruct((2 * half, cols), F32),
        compiler_params=_params(("arbitrary",)),
    )(_core_and_chip(), mine, got)


def _adamw(w, g, m, v, name, g_row0=0):
    rows, cols = w.shape
    t = _row_tile(rows)
    assert g_row0 % t == 0
    off = g_row0 // t

    def body(w_ref, g_ref, m_ref, v_ref, go_ref, d_ref, mo_ref, vo_ref):
        g_t = g_ref[...]
        m_new = ADAM_B1 * m_ref[...] + (1.0 - ADAM_B1) * g_t
        v_new = ADAM_B2 * v_ref[...] + (1.0 - ADAM_B2) * (g_t * g_t)
        m_hat = m_new / (1.0 - ADAM_B1 ** ADAM_STEP)
        v_hat = v_new / (1.0 - ADAM_B2 ** ADAM_STEP)
        go_ref[...] = g_t
        d_ref[...] = -ADAM_LR * (m_hat / (jnp.sqrt(v_hat) + ADAM_EPS) + ADAM_WD * w_ref[...])
        mo_ref[...] = m_new
        vo_ref[...] = v_new

    blk = pl.BlockSpec((t, cols), lambda r: (r, 0))
    return pl.pallas_call(
        body, name=name, grid=(rows // t,),
        in_specs=[blk, pl.BlockSpec((t, cols), lambda r: (r + off, 0)), blk, blk],
        out_specs=[blk] * 4,
        out_shape=[jax.ShapeDtypeStruct((rows, cols), F32)] * 4,
        compiler_params=_params(("arbitrary",)),
    )(w, g, m, v)


def _pack_small(d, vectors):
    rows = []
    for vec in vectors:
        flat = vec.reshape(-1)
        n_rows = -(-flat.shape[0] // d)
        rows.append(jnp.pad(flat, (0, n_rows * d - flat.shape[0])).reshape(n_rows, d))
    return jnp.concatenate(rows, axis=0)


def _unpack_small(packed, d, shapes):
    out, row = [], 0
    for shape in shapes:
        size = 1
        for s in shape:
            size *= s
        n_rows = -(-size // d)
        out.append(packed[row:row + n_rows].reshape(-1)[:size].reshape(shape))
        row += n_rows
    return out


def kernel(x, norm_pre, norm_post, pool_w_in, pool_w_group, pool_scale, pool_w_out, att_w_in, att_rel_bias, att_w_out, loss_target, m_norm_pre, m_norm_post, m_pool_w_in, m_pool_w_group, m_pool_scale, m_pool_w_out, m_att_w_in, m_att_rel_bias, m_att_w_out, v_norm_pre, v_norm_post, v_pool_w_in, v_pool_w_group, v_pool_scale, v_pool_w_out, v_att_w_in, v_att_rel_bias, v_att_w_out):
    _, s_len, d = x.shape
    gw = d // 2
    q = gw // N_CHIPS
    x2d = x.reshape(s_len, d)
    target = loss_target.reshape(s_len, d)

    def pack_g(p_group):
        return p_group.reshape(N_CHIPS * q, gw)

    wp_shard = jnp.concatenate([pool_w_in[0], pool_w_out[0]], axis=0).astype(BF16)
    wt_shard = jnp.concatenate([att_w_in[0], att_w_out[0]], axis=0).astype(BF16)
    wp, wg = _gather_weights([wp_shard, pack_g(pool_w_group).astype(BF16)])
    wg = wg.reshape(N_CHIPS, N_CHIPS, q, gw)

    x1, h0, mixed, z0, y0, wt = _l0_fwd(x2d, norm_pre[0:1], norm_post[0:1], pool_scale, wp, wg, wt_shard)
    h1, q_, k_, v_, z1 = _l1_inproj(x1, norm_pre[1:2], wt)
    bias = _bias_tiles(att_rel_bias[0])
    o, lse = _attn_fwd(q_, k_, v_, bias)

    g2, gated1, dy1, d_o, dz1, dgpost1, sq = _l1_out(o, z1, x1, target, norm_post[1:2], wt)
    dq, dk, dv, dbias = _attn_bwd(q_, k_, v_, bias, d_o, o, lse)
    dx1, du1, dgpre1 = _l1_in_bwd(dq, dk, dv, dz1, x1, g2, norm_pre[1:2], wt)

    blk = lambda n: n
    zero = lambda n: 0
    slab_t = d + d // N_CHIPS
    whole_in = pl.BlockSpec((None, d, d), lambda n, kk: (n, 0, 0))
    gt = _wgrad(h1, du1, "wgrad_att_in", N_CHIPS, d, d, zero, blk, (N_CHIPS, slab_t, d), whole_in)
    gt = _wgrad(gated1, dy1, "wgrad_att_out", 1, d, d, zero, zero, (N_CHIPS, slab_t, d),
                pl.BlockSpec((N_CHIPS, d // N_CHIPS, d), lambda n, kk: (0, N_CHIPS, 0)), into=gt)

    grad_x, du0, gated0, dy0, dmm, dgpre0, dgpost0, dscale = _l0_bwd(
        dx1, x2d, y0, mixed, z0, norm_pre[0:1], norm_post[0:1], pool_scale, wp, wg)

    swap_rider = lambda part: (functools.partial(_SwapPlan, send_half=_keep_own_half), part, part.shape, (1, 1))
    scatter_rider = lambda send: (_ScatterPlan, send, (3,) + send.shape[1:], (3, 3))
    gi, got_t = _wgrad(h0, du0, "wgrad_pool_in", N_CHIPS, d, d, zero, blk, (N_CHIPS, d, d), whole_in,
                       riders=[swap_rider(gt)])
    mine_t, send_t = _sum_siblings(gt, got_t, "sum_siblings_att")
    go, recv_t, got_i = _wgrad(gated0, dy0, "wgrad_pool_out", N_CHIPS, gw, d, blk, zero, (N_CHIPS, gw, d),
                               pl.BlockSpec((None, gw, d), lambda n, kk: (n, 0, 0)),
                               riders=[scatter_rider(send_t), swap_rider(gi)])
    mine_i, send_i = _sum_siblings(gi, got_i, "sum_siblings_pool_in")
    gg, recv_i, got_o = _wgrad(mixed, dmm, "wgrad_pool_group", N_CHIPS, gw, gw, blk, blk, (N_CHIPS, N_CHIPS, q, gw),
                               pl.BlockSpec((N_CHIPS, None, q, gw), lambda n, kk: (0, n, 0, 0)),
                               riders=[scatter_rider(send_i), swap_rider(go)])
    gg = gg.reshape(N_CHIPS, N_CHIPS * q, gw)
    mine_o, send_o = _sum_siblings(go, got_o, "sum_siblings_pool_out")
    (got_g,) = _swap_halves([gg], "swap_core_partials_group", _keep_own_half, in_place=False)
    mine_g, send_g = _sum_siblings(gg, got_g, "sum_siblings_group")
    d_rel, recv_o, recv_g = _rel_bias_grad(dbias, riders=[scatter_rider(send_o), scatter_rider(send_g)])
    red_t = _sum_chips(mine_t, recv_t, "sum_chips_att")
    red_i = _sum_chips(mine_i, recv_i, "sum_chips_pool_in")
    red_o = _sum_chips(mine_o, recv_o, "sum_chips_pool_out")
    red_g = _sum_chips(mine_g, recv_g, "sum_chips_group")
    grad_t, grad_i, grad_o, grad_g = _swap_halves([red_t, red_i, red_o, red_g], "swap_reduced_halves",
                                                  lambda c: c, in_place=True)

    small_shapes = [norm_pre.shape, norm_post.shape, pool_scale.shape, att_rel_bias.shape]
    part = _pack_small(d, [jnp.concatenate([dgpre0, dgpre1], axis=0), jnp.concatenate([dgpost0, dgpost1], axis=0),
                           dscale, d_rel, sq])
    total = _allreduce_small(part)
    loss = (0.5 / d) * jnp.sum(total[-1])
    w_small = _pack_small(d, [norm_pre, norm_post, pool_scale, att_rel_bias, jnp.zeros((d,), F32)])
    m_small = _pack_small(d, [m_norm_pre, m_norm_post, m_pool_scale, m_att_rel_bias, jnp.zeros((d,), F32)])
    v_small = _pack_small(d, [v_norm_pre, v_norm_post, v_pool_scale, v_att_rel_bias, jnp.ones((d,), F32)])
    small_out = [_unpack_small(a, d, small_shapes) for a in _adamw(w_small, total, m_small, v_small, "adamw_small")]

    big = {}
    for name, w, m, v, grad, row0 in [("pool_w_in", pool_w_in, m_pool_w_in, v_pool_w_in, grad_i, 0),
                                      ("pool_w_out", pool_w_out, m_pool_w_out, v_pool_w_out, grad_o, 0),
                                      ("att_w_in", att_w_in, m_att_w_in, v_att_w_in, grad_t, 0),
                                      ("att_w_out", att_w_out, m_att_w_out, v_att_w_out, grad_t, d)]:
        outs = _adamw(w[0], grad, m[0], v[0], "adamw_" + name, g_row0=row0)
        big[name] = [a.reshape(w.shape) for a in outs]
    outs = _adamw(pack_g(pool_w_group), grad_g, pack_g(m_pool_w_group), pack_g(v_pool_w_group), "adamw_pool_w_group")
    big["pool_w_group"] = [a.reshape(pool_w_group.shape) for a in outs]

    def leaf(kind):
        return (small_out[kind][0], small_out[kind][1], big["pool_w_in"][kind], big["pool_w_group"][kind],
                small_out[kind][2], big["pool_w_out"][kind], big["att_w_in"][kind], small_out[kind][3],
                big["att_w_out"][kind])

    return (loss, grad_x.reshape(x.shape), *leaf(0), *leaf(1), *leaf(2), *leaf(3))
```

```python
import functools

import jax
import jax.numpy as jnp
from jax import lax
from jax.experimental import pallas as pl
from jax.experimental.pallas import tpu as pltpu

F32 = jnp.float32
BF16 = jnp.bfloat16

RMS_EPS = 1e-6
CHUNK = 64
HEAD_DIM = 64
LEFT_CHUNKS = 8
PAD = LEFT_CHUNKS * CHUNK
MAX_REL = 256
POOL_WINDOWS = (2, 4, 8, 16)
HALO = 16
TS = 256
TS1 = 512
QB = 256
KB = QB + PAD
HEAD_PAIR = 2 * HEAD_DIM
NEG = -1e30
N_CHIPS = 4
N_DEV = 8

ADAM_LR = 0.001
ADAM_B1 = 0.9
ADAM_B2 = 0.999
ADAM_EPS = 1e-08
ADAM_WD = 0.01
ADAM_STEP = 10

VMEM_LIMIT = 56 * 1024 * 1024
MESH = pl.DeviceIdType.MESH
ANY = pl.BlockSpec(memory_space=pl.ANY)


def _dot(a, b):
    return jnp.dot(a, b, preferred_element_type=F32)


def _dot_nt(a, b):
    return lax.dot_general(a, b, (((1,), (1,)), ((), ())), preferred_element_type=F32)


def _dot_tn(a, b):
    return lax.dot_general(a, b, (((0,), (0,)), ((), ())), preferred_element_type=F32)


def _params(sem, limit=VMEM_LIMIT):
    return pltpu.CompilerParams(dimension_semantics=sem, vmem_limit_bytes=limit)


def _rms(x):
    return lax.rsqrt(jnp.mean(x * x, axis=-1, keepdims=True) + RMS_EPS)


def _rms_bwd(dyh, xh, r):
    return r * (dyh - xh * jnp.mean(dyh * xh, axis=-1, keepdims=True))


def _silu_parts(z):
    sig = jax.nn.sigmoid(z)
    return z * sig, sig * (1.0 + z * (1.0 - sig))


def _inv_count(tile, rows, w):
    t = tile * rows + lax.broadcasted_iota(jnp.int32, (rows, 1), 0)
    return 1.0 / jnp.minimum(t + 1, w).astype(F32)


def _start_weight_copies(copies):
    for c in copies:
        c.start()
    for c in copies:
        c.wait()


def _l0_fwd(x, g_pre, g_post, scale, wa, wg, next_shard):
    s_len, d = x.shape
    pw, gw = 2 * d, d // 2
    q = gw // N_CHIPS
    ts = TS1
    nt = s_len // ts
    assert nt >= 2

    def body(x_ref, gpre_ref, gpost_ref, sc_ref, wa_ref, wg_ref, shard_ref,
             x1_ref, h_ref, mixed_ref, z_ref, y_ref, next_ref,
             win_s, wout_s, wg_s, halo_s, sems, *gather_sems):
        i = pl.program_id(0)

        @pl.when(i == 0)
        def _():
            _GatherPlan([shard_ref], [next_ref], *gather_sems).send()
            copies = [pltpu.make_async_copy(wa_ref.at[:, pl.ds(0, d), :], win_s, sems.at[0]),
                      pltpu.make_async_copy(wa_ref.at[:, pl.ds(d, gw), :], wout_s, sems.at[1])]
            copies += [pltpu.make_async_copy(wg_ref.at[j], wg_s.at[:, pl.ds(j * q, q), :], sems.at[2 + j])
                       for j in range(N_CHIPS)]
            _start_weight_copies(copies)
            halo_s[...] = jnp.zeros_like(halo_s)

        @pl.when(i == nt // 2)
        def _():
            _GatherPlan([shard_ref], [next_ref], *gather_sems).pass_on()

        @pl.when(i == nt - 1)
        def _():
            _GatherPlan([shard_ref], [next_ref], *gather_sems).finish()

        xt = x_ref[...]
        h = ((xt * _rms(xt)) * gpre_ref[...]).astype(BF16)
        h_ref[...] = h
        a_blocks = [_dot(h, win_s[0]), _dot(h, win_s[1])]
        y = jnp.zeros((ts, d), F32)
        for g, w in enumerate(POOL_WINDOWS):
            a_g = a_blocks[g // 2][:, (g % 2) * gw:(g % 2 + 1) * gw]
            ext = jnp.concatenate([halo_s[g], a_g], axis=0)
            shift = 1
            while shift < w:
                ext = ext + pltpu.roll(ext, shift, 0)
                shift *= 2
            mixed = (ext[HALO:] * _inv_count(i, ts, w) - a_g).astype(BF16)
            halo_s[g] = a_g[ts - HALO:]
            mixed_ref[:, g * gw:(g + 1) * gw] = mixed
            z_g = _dot(h, win_s[2 + g // 2, :, (g % 2) * gw:(g % 2 + 1) * gw])
            z_ref[:, g * gw:(g + 1) * gw] = z_g
            ms = _dot(mixed, wg_s[g]) * sc_ref[:, g * gw:(g + 1) * gw]
            gated = (ms * _silu_parts(z_g)[0]).astype(BF16)
            y = y + _dot(gated, wout_s[g])
        y_ref[...] = y
        x1_ref[...] = xt + (y * _rms(y)) * gpost_ref[...]

    tile = lambda wdt: pl.BlockSpec((ts, wdt), lambda i: (i, 0))
    row = lambda wdt: pl.BlockSpec((1, wdt), lambda i: (0, 0))
    return pl.pallas_call(
        body, name="l0_fwd", grid=(nt,),
        in_specs=[tile(d), row(d), row(d), row(pw), ANY, ANY, ANY],
        out_specs=[tile(d), tile(d), tile(pw), tile(pw), tile(d), ANY],
        out_shape=[jax.ShapeDtypeStruct((s_len, d), F32), jax.ShapeDtypeStruct((s_len, d), BF16),
                   jax.ShapeDtypeStruct((s_len, pw), BF16), jax.ShapeDtypeStruct((s_len, pw), F32),
                   jax.ShapeDtypeStruct((s_len, d), F32),
                   jax.ShapeDtypeStruct((N_CHIPS,) + next_shard.shape, next_shard.dtype)],
        scratch_shapes=[pltpu.VMEM((N_CHIPS, d, d), BF16), pltpu.VMEM((N_CHIPS, gw, d), BF16),
                        pltpu.VMEM((N_CHIPS, gw, gw), BF16), pltpu.VMEM((N_CHIPS, HALO, gw), F32),
                        pltpu.SemaphoreType.DMA((2 + N_CHIPS,))] + _gather_sems(1),
        compiler_params=_params(("arbitrary",)),
    )(x, g_pre, g_post, scale, wa, wg, next_shard)


def _in_proj_bwd(du, x, g_res, g_pre, wa):
    s_len, d = x.shape
    nt = s_len // TS1

    def body(du_ref, x_ref, g_ref, gpre_ref, wa_ref, dx_ref, dgpre_ref, win_s, sem):
        @pl.when(pl.program_id(0) == 0)
        def _():
            _start_weight_copies([pltpu.make_async_copy(wa_ref.at[:, pl.ds(0, d), :], win_s, sem.at[0])])
            dgpre_ref[...] = jnp.zeros_like(dgpre_ref)

        dh = jnp.zeros((TS1, d), F32)
        for j in range(N_CHIPS):
            dh = dh + _dot_nt(du_ref[:, j * d:(j + 1) * d], win_s[j])
        xt = x_ref[...]
        r_x = _rms(xt)
        xh = xt * r_x
        dgpre_ref[...] += jnp.sum(dh * xh, axis=0, keepdims=True)
        dx_ref[...] = g_ref[...] + _rms_bwd(dh * gpre_ref[...], xh, r_x)

    tile = pl.BlockSpec((TS1, d), lambda i: (i, 0))
    row = pl.BlockSpec((1, d), lambda i: (0, 0))
    return pl.pallas_call(
        body, name="l0_in_bwd", grid=(nt,),
        in_specs=[pl.BlockSpec((TS1, 4 * d), lambda i: (i, 0)), tile, tile, row, ANY],
        out_specs=[tile, row],
        out_shape=[jax.ShapeDtypeStruct((s_len, d), F32), jax.ShapeDtypeStruct((1, d), F32)],
        scratch_shapes=[pltpu.VMEM((N_CHIPS, d, d), BF16), pltpu.SemaphoreType.DMA((1,))],
        compiler_params=_params(("arbitrary",)),
    )(du, x, g_res, g_pre, wa)


def _l0_bwd(dx1, y, mixed, z, g_post, scale, wa, wg):
    s_len, d = y.shape
    pw, gw = 2 * d, d // 2
    q = gw // N_CHIPS
    TS = TS1
    nt = s_len // TS

    def body(dx1_ref, y_ref, mixed_ref, z_ref, gpost_ref, sc_ref, wa_ref, wg_ref,
             du_ref, gated_ref, dy_ref, dmm_ref, dgpost_ref, dsc_ref,
             wout_s, wg_s, halo_s, sems):
        i = pl.program_id(0)
        tile = nt - 1 - i

        @pl.when(i == 0)
        def _():
            copies = [pltpu.make_async_copy(wa_ref.at[:, pl.ds(d, gw), :], wout_s, sems.at[1])]
            copies += [pltpu.make_async_copy(wg_ref.at[j], wg_s.at[:, pl.ds(j * q, q), :], sems.at[2 + j])
                       for j in range(N_CHIPS)]
            _start_weight_copies(copies)
            halo_s[...] = jnp.zeros_like(halo_s)
            dgpost_ref[...] = jnp.zeros_like(dgpost_ref)
            dsc_ref[...] = jnp.zeros_like(dsc_ref)

        g_in = dx1_ref[...]
        yt = y_ref[...]
        r_y = _rms(yt)
        yh = yt * r_y
        dgpost_ref[...] += jnp.sum(g_in * yh, axis=0, keepdims=True)
        dy = _rms_bwd(g_in * gpost_ref[...], yh, r_y).astype(BF16)
        dy_ref[...] = dy

        for g, w in enumerate(POOL_WINDOWS):
            cols = slice(g * gw, (g + 1) * gw)
            dgated = _dot_nt(dy, wout_s[g])
            mm = _dot(mixed_ref[:, cols], wg_s[g])
            sc = sc_ref[:, cols]
            ms = mm * sc
            z_g = z_ref[:, cols]
            sz, dsz = _silu_parts(z_g)
            gated_ref[:, cols] = (ms * sz).astype(BF16)
            dms = dgated * sz
            dz = (dgated * ms * dsz).astype(BF16)
            dsc_ref[:, cols] += jnp.sum(dms * mm, axis=0, keepdims=True)
            dmm = (dms * sc).astype(BF16)
            dmm_ref[:, cols] = dmm
            dmixed = _dot_nt(dmm, wg_s[g])
            e = dmixed * _inv_count(tile, TS, w)
            ext = jnp.concatenate([e, halo_s[g]], axis=0)
            shift = 1
            while shift < w:
                ext = ext + pltpu.roll(ext, TS + HALO - shift, 0)
                shift *= 2
            da = (ext[:TS] - dmixed).astype(BF16)
            halo_s[g] = e[:HALO]
            du_ref[:, cols] = da
            du_ref[:, pw + g * gw:pw + (g + 1) * gw] = dz

    tile_spec = lambda wdt: pl.BlockSpec((TS, wdt), lambda i: (nt - 1 - i, 0))
    row = lambda wdt: pl.BlockSpec((1, wdt), lambda i: (0, 0))
    return pl.pallas_call(
        body, name="l0_bwd", grid=(nt,),
        in_specs=[tile_spec(d), tile_spec(d), tile_spec(pw), tile_spec(pw), row(d), row(pw), ANY, ANY],
        out_specs=[tile_spec(2 * pw), tile_spec(pw), tile_spec(d), tile_spec(pw), row(d), row(pw)],
        out_shape=[jax.ShapeDtypeStruct((s_len, 2 * pw), BF16),
                   jax.ShapeDtypeStruct((s_len, pw), BF16), jax.ShapeDtypeStruct((s_len, d), BF16),
                   jax.ShapeDtypeStruct((s_len, pw), BF16),
                   jax.ShapeDtypeStruct((1, d), F32), jax.ShapeDtypeStruct((1, pw), F32)],
        scratch_shapes=[pltpu.VMEM((N_CHIPS, gw, d), BF16),
                        pltpu.VMEM((N_CHIPS, gw, gw), BF16), pltpu.VMEM((N_CHIPS, HALO, gw), F32),
                        pltpu.SemaphoreType.DMA((2 + N_CHIPS,))],
        compiler_params=_params(("arbitrary",)),
    )(dx1, y, mixed, z, g_post, scale, wa, wg)


def _l1_inproj(x1, g_pre, wa):
    s_len, d = x1.shape
    nt = s_len // TS1
    npad = PAD // TS1

    def body(x_ref, gpre_ref, wa_ref, h_ref, q_ref, k_ref, v_ref, z_ref, win_s, sem):
        i = pl.program_id(0)

        @pl.when(i == 0)
        def _():
            _start_weight_copies([pltpu.make_async_copy(wa_ref.at[:, pl.ds(0, d), :], win_s, sem.at[0])])

        @pl.when(i < npad)
        def _():
            k_ref[...] = jnp.zeros_like(k_ref)
            v_ref[...] = jnp.zeros_like(v_ref)

        @pl.when(i >= npad)
        def _():
            xt = x_ref[...]
            h = ((xt * _rms(xt)) * gpre_ref[...]).astype(BF16)
            h_ref[...] = h
            q_ref[...] = _dot(h, win_s[0]).astype(BF16)
            k_ref[...] = _dot(h, win_s[1]).astype(BF16)
            v_ref[...] = _dot(h, win_s[2]).astype(BF16)
            z_ref[...] = _dot(h, win_s[3])

    tile = pl.BlockSpec((TS1, d), lambda i: (jnp.maximum(i - npad, 0), 0))
    padded = pl.BlockSpec((TS1, d), lambda i: (i, 0))
    return pl.pallas_call(
        body, name="l1_inproj", grid=(nt + npad,),
        in_specs=[tile, pl.BlockSpec((1, d), lambda i: (0, 0)), ANY],
        out_specs=[tile, tile, padded, padded, tile],
        out_shape=[jax.ShapeDtypeStruct((s_len, d), BF16), jax.ShapeDtypeStruct((s_len, d), BF16),
                   jax.ShapeDtypeStruct((PAD + s_len, d), BF16), jax.ShapeDtypeStruct((PAD + s_len, d), BF16),
                   jax.ShapeDtypeStruct((s_len, d), F32)],
        scratch_shapes=[pltpu.VMEM((N_CHIPS, d, d), BF16), pltpu.SemaphoreType.DMA((1,))],
        compiler_params=_params(("arbitrary",)),
    )(x1, g_pre, wa)


SKEW = QB + KB
N_CLIPPED = QB + PAD - MAX_REL + 1
assert QB <= MAX_REL


def _bias_tiles(rel_bias):
    nh = rel_bias.shape[0]
    by_column = jnp.concatenate([jnp.broadcast_to(rel_bias[:, 2 * MAX_REL:], (nh, N_CLIPPED)),
                                 jnp.flip(rel_bias[:, MAX_REL + 1 - QB:2 * MAX_REL], axis=1)], axis=1).astype(F32)

    def body(col_ref, out_ref):
        rows = jnp.broadcast_to(col_ref[pl.ds(pl.program_id(0), 1), :], (QB, SKEW))
        tile = pltpu.roll(rows, SKEW - QB, 1, stride=1, stride_axis=0)[:, :KB]
        i = lax.broadcasted_iota(jnp.int32, (QB, KB), 0)
        j = lax.broadcasted_iota(jnp.int32, (QB, KB), 1)
        first = (i // CHUNK) * CHUNK
        out_ref[...] = jnp.where((j >= first) & (j < first + PAD + CHUNK), tile, NEG)

    return pl.pallas_call(
        body, name="bias_tiles", grid=(nh,),
        in_specs=[pl.BlockSpec((nh, SKEW), lambda h: (0, 0))],
        out_specs=pl.BlockSpec((None, QB, KB), lambda h: (h, 0, 0)),
        out_shape=jax.ShapeDtypeStruct((nh, QB, KB), F32),
        compiler_params=_params(("arbitrary",)),
    )(by_column)


ROWS = 16


BLOCKS_PER_STEP = 8


def _row_blocks(rows=ROWS):
    return [pl.ds(r * rows, rows) for r in range(QB // rows)]


def _attn_fwd(q, k, v, bias):
    s_len, d = q.shape
    nhp = d // HEAD_PAIR
    per_step = min(BLOCKS_PER_STEP, s_len // QB)
    nq = s_len // (QB * per_step)

    def body(q_ref, k_ref, v_ref, b_ref, o_ref, lse_ref, s_s, p_s, l_s):
        head_of_lane = lax.broadcasted_iota(jnp.int32, (1, HEAD_PAIR), 1) // HEAD_DIM

        units = [(u, hh) for u in range(per_step) for hh in range(2)]

        def band(u):
            return pl.ds(pl.multiple_of((pl.program_id(1) * per_step + u) * QB, QB), KB)

        def issue_scores(n):
            u, hh = units[n]
            qt = q_ref[pl.ds(u * QB, QB), :] * (HEAD_DIM ** -0.5)
            s_s[n % 2] = _dot_nt(jnp.where(head_of_lane == hh, qt, jnp.zeros_like(qt)), k_ref[band(u), :])

        issue_scores(0)
        outs, lses = [], []
        for n, (u, hh) in enumerate(units):
            if n + 1 < len(units):
                issue_scores(n + 1)
            first_key = (pl.program_id(1) * per_step + u) * QB
            colvalid = (first_key + lax.broadcasted_iota(jnp.int32, (1, KB), 1)) >= PAD
            for rows in _row_blocks():
                s = jnp.where(colvalid, s_s[n % 2, rows, :] + b_ref[hh, rows, :], NEG)
                m = jnp.max(s, axis=-1, keepdims=True)
                e = jnp.exp(s - m)
                l = jnp.sum(e, axis=-1, keepdims=True)
                p_s[n % 2, rows, :] = (e * (1.0 / l)).astype(BF16)
                l_s[n % 2, rows, :] = m + jnp.log(l)
            outs.append(_dot(p_s[n % 2], v_ref[band(u), :]))
            lses.append(l_s[n % 2])
            if hh == 1:
                here = pl.ds(u * QB, QB)
                o_ref[here, :] = jnp.where(head_of_lane == 0, outs[-2], outs[-1])
                lse_ref[here, :] = jnp.where(head_of_lane == 0, lses[-2], lses[-1])

    blk = pl.BlockSpec((per_step * QB, HEAD_PAIR), lambda hp, b: (b, hp))
    whole = pl.BlockSpec((PAD + s_len, HEAD_PAIR), lambda hp, b: (0, hp))
    return pl.pallas_call(
        body, name="attn_fwd", grid=(nhp, nq),
        in_specs=[blk, whole, whole, pl.BlockSpec((2, QB, KB), lambda hp, b: (hp, 0, 0))],
        out_specs=[blk, blk],
        out_shape=[jax.ShapeDtypeStruct((s_len, d), F32), jax.ShapeDtypeStruct((s_len, d), F32)],
        scratch_shapes=[pltpu.VMEM((2, QB, KB), F32), pltpu.VMEM((2, QB, KB), BF16), pltpu.VMEM((2, QB, 1), F32)],
        compiler_params=_params(("arbitrary", "arbitrary")),
    )(q, k, v, bias)


def _attn_bwd(q, k, v, bias, d_o, o, lse):
    s_len, d = q.shape
    nhp = d // HEAD_PAIR
    per_step = min(BLOCKS_PER_STEP, s_len // QB)
    nq = s_len // (QB * per_step)
    qk_scale = HEAD_DIM ** -0.5

    def body(q_ref, k_ref, v_ref, b_ref, do_ref, o_ref, lse_ref, dq_ref, dk_out_ref, dv_out_ref, db_ref,
             s_s, dp_s, p_s, ds_s, lse_s, delta_s, dk_ref, dv_ref):
        @pl.when(pl.program_id(1) == 0)
        def _():
            dk_ref[...] = jnp.zeros_like(dk_ref)
            dv_ref[...] = jnp.zeros_like(dv_ref)
            db_ref[...] = jnp.zeros_like(db_ref)

        head_of_lane = lax.broadcasted_iota(jnp.int32, (1, HEAD_PAIR), 1) // HEAD_DIM

        units = [(u, hh) for u in range(per_step) for hh in range(2)]

        def band(u):
            return pl.ds(pl.multiple_of((pl.program_id(1) * per_step + u) * QB, QB), KB)

        def masked(ref, u, hh, factor=None):
            x = ref[pl.ds(u * QB, QB), :]
            x = x if factor is None else x * factor
            return jnp.where(head_of_lane == hh, x, jnp.zeros_like(x))

        def issue_tiles(n):
            u, hh = units[n]
            here = pl.ds(u * QB, QB)
            do_o = do_ref[here, :].astype(F32) * o_ref[here, :]
            delta_s[n % 2] = jnp.sum(jnp.where(head_of_lane == hh, do_o, 0.0), axis=-1, keepdims=True)
            lse_s[n % 2] = lse_ref[here, hh * HEAD_DIM:hh * HEAD_DIM + 1]
            s_s[n % 2] = _dot_nt(masked(q_ref, u, hh, qk_scale), k_ref[band(u), :])
            dp_s[n % 2] = _dot_nt(masked(do_ref, u, hh), v_ref[band(u), :])

        issue_tiles(0)
        dq_heads, dk_band, dv_band = [], None, None
        for n, (u, hh) in enumerate(units):
            if n + 1 < len(units):
                issue_tiles(n + 1)
            first_key = (pl.program_id(1) * per_step + u) * QB
            colvalid = (first_key + lax.broadcasted_iota(jnp.int32, (1, KB), 1)) >= PAD
            for rows in _row_blocks():
                t = jnp.where(colvalid, s_s[n % 2, rows, :] + b_ref[hh, rows, :] - lse_s[n % 2, rows, :], NEG)
                p = jnp.exp(t)
                ds = p * (dp_s[n % 2, rows, :] - delta_s[n % 2, rows, :])
                db_ref[hh, rows, :] += ds
                p_s[n % 2, rows, :] = p.astype(BF16)
                ds_s[n % 2, rows, :] = ds.astype(BF16)
            q_m = masked(q_ref, u, hh, qk_scale)
            dv_unit = _dot_tn(p_s[n % 2], masked(do_ref, u, hh))
            dq_heads.append(_dot(ds_s[n % 2], k_ref[band(u), :]) * qk_scale)
            dk_unit = _dot_tn(ds_s[n % 2], q_m)
            if hh == 0:
                dk_band, dv_band = dk_unit, dv_unit
            else:
                dq_ref[pl.ds(u * QB, QB), :] = jnp.where(head_of_lane == 0, dq_heads[-2], dq_heads[-1]).astype(BF16)
                dk_ref[band(u), :] += dk_band + dk_unit
                dv_ref[band(u), :] += dv_band + dv_unit

        @pl.when(pl.program_id(1) == nq - 1)
        def _():
            dk_out_ref[...] = dk_ref[...].astype(BF16)
            dv_out_ref[...] = dv_ref[...].astype(BF16)

    blk = pl.BlockSpec((per_step * QB, HEAD_PAIR), lambda hp, b: (b, hp))
    whole = pl.BlockSpec((PAD + s_len, HEAD_PAIR), lambda hp, b: (0, hp))
    btile = pl.BlockSpec((2, QB, KB), lambda hp, b: (hp, 0, 0))
    return pl.pallas_call(
        body, name="attn_bwd", grid=(nhp, nq),
        in_specs=[blk, whole, whole, btile, blk, blk, blk],
        out_specs=[blk, whole, whole, btile],
        out_shape=[jax.ShapeDtypeStruct((s_len, d), BF16),
                   jax.ShapeDtypeStruct((PAD + s_len, d), BF16), jax.ShapeDtypeStruct((PAD + s_len, d), BF16),
                   jax.ShapeDtypeStruct(bias.shape, F32)],
        scratch_shapes=[pltpu.VMEM((2, QB, KB), F32), pltpu.VMEM((2, QB, KB), F32),
                        pltpu.VMEM((2, QB, KB), BF16), pltpu.VMEM((2, QB, KB), BF16),
                        pltpu.VMEM((2, QB, 1), F32), pltpu.VMEM((2, QB, 1), F32),
                        pltpu.VMEM((PAD + s_len, HEAD_PAIR), F32), pltpu.VMEM((PAD + s_len, HEAD_PAIR), F32)],
        compiler_params=_params(("arbitrary", "arbitrary")),
    )(q, k, v, bias, d_o, o, lse)


def _rel_bias_grad(db, riders=()):
    nh = db.shape[0]
    n_ride = len(riders)

    def body(*refs):
        db_ref, out_ref = refs[0], refs[1 + n_ride]

        def plans():
            return [rider[0]([refs[1 + r]], [refs[2 + n_ride + r]],
                             *refs[2 + 2 * n_ride + 2 * r:4 + 2 * n_ride + 2 * r]) for r, rider in enumerate(riders)]

        if riders:
            @pl.when(pl.program_id(0) == 0)
            def _():
                for plan in plans():
                    plan.send()

            @pl.when(pl.program_id(0) == nh - 1)
            def _():
                for plan in plans():
                    plan.finish()

        i0 = lax.broadcasted_iota(jnp.int32, (QB, QB), 0)
        i1 = lax.broadcasted_iota(jnp.int32, (QB, QB), 1)
        exchange = jnp.where(i0 + i1 == QB - 1, 1.0, 0.0).astype(BF16)
        rest = db_ref[...]
        flipped = jnp.zeros((QB, KB), F32)
        for _ in range(3):
            piece = rest.astype(BF16)
            flipped = flipped + _dot(exchange, piece)
            rest = rest - piece.astype(F32)
        m = jnp.concatenate([flipped, jnp.zeros((QB, SKEW - KB), F32)], axis=1)
        diag = jnp.sum(pltpu.roll(m, 1, 1, stride=1, stride_axis=0), axis=0, keepdims=True)
        c = lax.broadcasted_iota(jnp.int32, (1, SKEW), 1)
        clipped = jnp.sum(jnp.where(c < N_CLIPPED, diag, 0.0), axis=1, keepdims=True)
        out_ref[...] = jnp.where(c == 0, clipped, diag)

    results = pl.pallas_call(
        body, name="bias_diagonals", grid=(nh,),
        in_specs=[pl.BlockSpec((None, QB, KB), lambda h: (h, 0, 0))] + [ANY] * n_ride,
        out_specs=[pl.BlockSpec((None, 1, SKEW), lambda h: (h, 0, 0))] + [ANY] * n_ride,
        out_shape=[jax.ShapeDtypeStruct((nh, 1, SKEW), F32)]
                  + [jax.ShapeDtypeStruct(rider[2], rider[1].dtype) for rider in riders],
        scratch_shapes=[pltpu.SemaphoreType.DMA((n,)) for rider in riders for n in rider[3]],
        compiler_params=_params(("arbitrary",)),
    )(db, *[rider[1] for rider in riders])
    diag = results[0][:, 0]
    d_rel = jnp.concatenate([jnp.zeros((nh, MAX_REL + 1 - QB), F32), jnp.flip(diag[:, N_CLIPPED:], axis=1),
                             diag[:, :1]], axis=1)
    return (d_rel, *results[1:]) if riders else d_rel


def _l1_out(o, z, x1, target, g_post, wa):
    s_len, d = o.shape
    nt = s_len // TS1
    slab = d // N_CHIPS

    def body(o_ref, z_ref, x1_ref, t_ref, gpost_ref, wa_ref,
             g2_ref, gated_ref, dy_ref, do_ref, dz_ref, dgpost_ref, sq_ref, wout_s, sems):
        i = pl.program_id(0)

        @pl.when(i == 0)
        def _():
            _start_weight_copies([
                pltpu.make_async_copy(wa_ref.at[j, pl.ds(d, slab), :],
                                      wout_s.at[pl.ds(j * slab, slab), :], sems.at[j])
                for j in range(N_CHIPS)])
            dgpost_ref[...] = jnp.zeros_like(dgpost_ref)
            sq_ref[...] = jnp.zeros_like(sq_ref)

        ot = o_ref[...]
        sz, dsz = _silu_parts(z_ref[...])
        gated = (ot * sz).astype(BF16)
        gated_ref[...] = gated
        y = _dot(gated, wout_s[...])
        r_y = _rms(y)
        yh = y * r_y
        err = x1_ref[...] + yh * gpost_ref[...] - t_ref[...]
        sq_ref[...] += jnp.sum(err * err, axis=0, keepdims=True)
        g2 = err * (1.0 / d)
        g2_ref[...] = g2
        dgpost_ref[...] += jnp.sum(g2 * yh, axis=0, keepdims=True)
        dy = _rms_bwd(g2 * gpost_ref[...], yh, r_y).astype(BF16)
        dy_ref[...] = dy
        dgated = _dot_nt(dy, wout_s[...])
        do_ref[...] = (dgated * sz).astype(BF16)
        dz_ref[...] = (dgated * ot * dsz).astype(BF16)

    tile = pl.BlockSpec((TS1, d), lambda i: (i, 0))
    row = pl.BlockSpec((1, d), lambda i: (0, 0))
    return pl.pallas_call(
        body, name="l1_out", grid=(nt,),
        in_specs=[tile, tile, tile, tile, row, ANY],
        out_specs=[tile, tile, tile, tile, tile, row, row],
        out_shape=[jax.ShapeDtypeStruct((s_len, d), F32)] + [jax.ShapeDtypeStruct((s_len, d), BF16)] * 4
                  + [jax.ShapeDtypeStruct((1, d), F32)] * 2,
        scratch_shapes=[pltpu.VMEM((d, d), BF16), pltpu.SemaphoreType.DMA((N_CHIPS,))],
        compiler_params=_params(("arbitrary",)),
    )(o, z, x1, target, g_post, wa)


def _l1_in_bwd(dq, dk, dv, dz, x1, g2, g_pre, wa):
    s_len, d = x1.shape
    nt = s_len // TS1
    npad = PAD // TS1

    def body(dq_ref, dk_ref, dv_ref, dz_ref, x_ref, g2_ref, gpre_ref, wa_ref,
             dx_ref, du_ref, dgpre_ref, win_s, sem):
        i = pl.program_id(0)

        @pl.when(i == 0)
        def _():
            _start_weight_copies([pltpu.make_async_copy(wa_ref.at[:, pl.ds(0, d), :], win_s, sem.at[0])])
            dgpre_ref[...] = jnp.zeros_like(dgpre_ref)

        parts = [dq_ref[...], dk_ref[...], dv_ref[...], dz_ref[...]]
        dh = jnp.zeros((TS1, d), F32)
        for j, part in enumerate(parts):
            du_ref[:, j * d:(j + 1) * d] = part
            dh = dh + _dot_nt(part, win_s[j])
        xt = x_ref[...]
        r_x = _rms(xt)
        xh = xt * r_x
        dgpre_ref[...] += jnp.sum(dh * xh, axis=0, keepdims=True)
        dx_ref[...] = g2_ref[...] + _rms_bwd(dh * gpre_ref[...], xh, r_x)

    tile = pl.BlockSpec((TS1, d), lambda i: (i, 0))
    padded = pl.BlockSpec((TS1, d), lambda i: (i + npad, 0))
    row = pl.BlockSpec((1, d), lambda i: (0, 0))
    return pl.pallas_call(
        body, name="l1_in_bwd", grid=(nt,),
        in_specs=[tile, padded, padded, tile, tile, tile, row, ANY],
        out_specs=[tile, pl.BlockSpec((TS1, 4 * d), lambda i: (i, 0)), row],
        out_shape=[jax.ShapeDtypeStruct((s_len, d), F32), jax.ShapeDtypeStruct((s_len, 4 * d), BF16),
                   jax.ShapeDtypeStruct((1, d), F32)],
        scratch_shapes=[pltpu.VMEM((N_CHIPS, d, d), BF16), pltpu.SemaphoreType.DMA((1,))],
        compiler_params=_params(("arbitrary",)),
    )(dq, dk, dv, dz, x1, g2, g_pre, wa)


def _wgrad(a, b, name, n_out, a_width, b_width, a_block, b_block, out_shape, out_spec, into=None, riders=()):
    s_len = a.shape[0]
    tk = min(2048, s_len)
    nk = s_len // tk
    n_ride = len(riders)
    n_in = 2 + (into is not None) + n_ride

    def body(*refs):
        a_ref, b_ref = refs[:2]
        out_ref = refs[n_in]

        def plans():
            return [rider[0]([refs[n_in - n_ride + r]], [refs[n_in + 1 + r]],
                             *refs[n_in + 1 + n_ride + 2 * r:n_in + 3 + n_ride + 2 * r])
                    for r, rider in enumerate(riders)]

        if riders:
            @pl.when((pl.program_id(0) == 0) & (pl.program_id(1) == 0))
            def _():
                for plan in plans():
                    plan.send()

        @pl.when(pl.program_id(1) == 0)
        def _():
            out_ref[...] = jnp.zeros_like(out_ref)

        out_ref[...] += _dot_tn(a_ref[...], b_ref[...]).reshape(out_ref.shape)

        if riders:
            @pl.when((pl.program_id(0) == n_out - 1) & (pl.program_id(1) == nk - 1))
            def _():
                for plan in plans():
                    plan.finish()

    operands = [a, b] + ([into] if into is not None else []) + [rider[1] for rider in riders]
    results = pl.pallas_call(
        body, name=name, grid=(n_out, nk),
        in_specs=[pl.BlockSpec((tk, a_width), lambda n, kk: (kk, a_block(n))),
                  pl.BlockSpec((tk, b_width), lambda n, kk: (kk, b_block(n)))] + [ANY] * (n_in - 2),
        out_specs=[out_spec] + [ANY] * n_ride,
        out_shape=[jax.ShapeDtypeStruct(out_shape, F32)]
                  + [jax.ShapeDtypeStruct(rider[2], rider[1].dtype) for rider in riders],
        scratch_shapes=[pltpu.SemaphoreType.DMA((n,)) for rider in riders for n in rider[3]],
        input_output_aliases={2: 0} if into is not None else {},
        compiler_params=_params(("arbitrary", "arbitrary")),
    )(*operands)
    return results if riders else results[0]


def _place():
    x, y, c = lax.axis_index("x"), lax.axis_index("y"), lax.axis_index("c")
    others = [(1 - x, y), (x, 1 - y), (1 - x, 1 - y)]
    return x, y, c, others


class _GatherPlan:
    def __init__(self, src, dst, send, recv, fwd_send, fwd_recv, local):
        x, y, c, others = _place()
        me = 2 * x + y
        sibling = (x, y, 1 - c)

        def half(ref, cc):
            rows = ref.shape[0] // 2
            return ref.at[pl.ds(cc * rows, rows), :]

        self.mine = [pltpu.make_async_copy(src[a], dst[a].at[me], local.at[a]) for a in range(len(src))]
        self.first, self.passed, self.arrive, self.arrive_fwd = [], [], [], []
        for a in range(len(src)):
            for k, (ox, oy) in enumerate(others):
                sem = a * 3 + k
                self.first.append(pltpu.make_async_remote_copy(
                    src_ref=half(src[a], c), dst_ref=half(dst[a].at[me], c),
                    send_sem=send.at[sem], recv_sem=recv.at[sem], device_id=(ox, oy, c), device_id_type=MESH))
                theirs = half(dst[a].at[2 * ox + oy], c)
                self.arrive.append(pltpu.make_async_remote_copy(
                    src_ref=theirs, dst_ref=theirs, send_sem=send.at[sem], recv_sem=recv.at[sem],
                    device_id=(ox, oy, c), device_id_type=MESH))
                self.passed.append(pltpu.make_async_remote_copy(
                    src_ref=theirs, dst_ref=theirs, send_sem=fwd_send.at[sem], recv_sem=fwd_recv.at[sem],
                    device_id=sibling, device_id_type=MESH))
                other_half = half(dst[a].at[2 * ox + oy], 1 - c)
                self.arrive_fwd.append(pltpu.make_async_remote_copy(
                    src_ref=other_half, dst_ref=other_half, send_sem=fwd_send.at[sem], recv_sem=fwd_recv.at[sem],
                    device_id=sibling, device_id_type=MESH))

    def send(self):
        for cp in self.mine + self.first:
            cp.start()

    def pass_on(self):
        for got, onward in zip(self.arrive, self.passed):
            got.wait_recv()
            onward.start()

    def finish(self):
        for got in self.arrive_fwd:
            got.wait_recv()
        for cp in self.first + self.passed:
            cp.wait_send()
        for cp in self.mine:
            cp.wait()


def _gather_sems(n):
    return [pltpu.SemaphoreType.DMA((3 * n,))] * 4 + [pltpu.SemaphoreType.DMA((n,))]


def _gather_weights(shards):
    n = len(shards)

    def body(*refs):
        plan = _GatherPlan(refs[:n], refs[n:2 * n], *refs[2 * n:])
        plan.send()
        plan.pass_on()
        plan.finish()

    return pl.pallas_call(
        body, name="gather_weights",
        in_specs=[ANY] * n, out_specs=[ANY] * n,
        out_shape=[jax.ShapeDtypeStruct((N_CHIPS,) + s.shape, s.dtype) for s in shards],
        scratch_shapes=_gather_sems(n),
    )(*shards)


class _SwapPlan:
    def __init__(self, src, dst, send, recv, send_half):
        x, y, c, _ = _place()
        sibling = (x, y, 1 - c)
        self.out_copies, self.in_copies = [], []
        for a in range(len(src)):
            rows = src[a].shape[-2] // 2
            lead = (slice(None),) * (len(src[a].shape) - 2)
            going = lead + (pl.ds(send_half(c) * rows, rows), slice(None))
            coming = lead + (pl.ds((1 - send_half(c)) * rows, rows), slice(None))
            self.out_copies.append(pltpu.make_async_remote_copy(
                src_ref=src[a].at[going], dst_ref=dst[a].at[going], send_sem=send.at[a], recv_sem=recv.at[a],
                device_id=sibling, device_id_type=MESH))
            self.in_copies.append(pltpu.make_async_remote_copy(
                src_ref=src[a].at[coming], dst_ref=dst[a].at[coming], send_sem=send.at[a], recv_sem=recv.at[a],
                device_id=sibling, device_id_type=MESH))

    def send(self):
        for cp in self.out_copies:
            cp.start()

    def finish(self):
        for cp in self.in_copies:
            cp.wait_recv()
        for cp in self.out_copies:
            cp.wait_send()


def _keep_own_half(c):
    return 1 - c


def _swap_halves(arrays, name, send_half, in_place):
    n = len(arrays)

    def body(*refs):
        src, dst = refs[:n], refs[n:2 * n]
        plan = _SwapPlan(dst if in_place else src, dst, *refs[2 * n:], send_half=send_half)
        plan.send()
        plan.finish()

    return pl.pallas_call(
        body, name=name,
        in_specs=[ANY] * n, out_specs=[ANY] * n,
        out_shape=[jax.ShapeDtypeStruct(a.shape, a.dtype) for a in arrays],
        scratch_shapes=[pltpu.SemaphoreType.DMA((n,)), pltpu.SemaphoreType.DMA((n,))],
        input_output_aliases={a: a for a in range(n)} if in_place else {},
    )(*arrays)


class _ScatterPlan:
    def __init__(self, src, dst, send, recv):
        x, y, c, others = _place()
        self.out_copies, self.in_copies = [], []
        for a in range(len(src)):
            rows = src[a].shape[1] // 2
            mine = pl.ds(c * rows, rows)
            for k, (ox, oy) in enumerate(others):
                sem = a * 3 + k
                self.out_copies.append(pltpu.make_async_remote_copy(
                    src_ref=src[a].at[2 * ox + oy, mine, :], dst_ref=dst[a].at[k, mine, :],
                    send_sem=send.at[sem], recv_sem=recv.at[sem], device_id=(ox, oy, c), device_id_type=MESH))
                self.in_copies.append(pltpu.make_async_remote_copy(
                    src_ref=dst[a].at[k, mine, :], dst_ref=dst[a].at[k, mine, :],
                    send_sem=send.at[sem], recv_sem=recv.at[sem], device_id=(ox, oy, c), device_id_type=MESH))

    def send(self):
        for cp in self.out_copies:
            cp.start()

    def finish(self):
        for cp in self.in_copies:
            cp.wait_recv()
        for cp in self.out_copies:
            cp.wait_send()


def _allreduce_small(part):
    rows, width = part.shape

    def body(p_ref, out_ref, all_ref, send, recv):
        x, y, c, _ = _place()
        me = 4 * x + 2 * y + c
        all_ref[me] = p_ref[...]
        copies = []
        for k in range(1, N_DEV):
            px, py, pc = x ^ (k >> 2), y ^ ((k >> 1) & 1), c ^ (k & 1)
            copies.append(pltpu.make_async_remote_copy(
                src_ref=p_ref, dst_ref=all_ref.at[me], send_sem=send.at[k - 1], recv_sem=recv.at[k - 1],
                device_id=(px, py, pc), device_id_type=MESH))
        for cp in copies:
            cp.start()
        for cp in copies:
            cp.wait()
        total = all_ref[0]
        for k in range(1, N_DEV):
            total = total + all_ref[k]
        out_ref[...] = total

    return pl.pallas_call(
        body, name="allreduce_small",
        in_specs=[pl.BlockSpec(memory_space=pltpu.VMEM)],
        out_specs=pl.BlockSpec(memory_space=pltpu.VMEM),
        out_shape=jax.ShapeDtypeStruct((rows, width), F32),
        scratch_shapes=[pltpu.VMEM((N_DEV, rows, width), F32),
                        pltpu.SemaphoreType.DMA((N_DEV - 1,)), pltpu.SemaphoreType.DMA((N_DEV - 1,))],
    )(part)


def _row_tile(rows):
    t = min(rows, 256)
    while rows % t:
        t //= 2
    return t


def _core_and_chip():
    return jnp.stack([lax.axis_index("c"), 2 * lax.axis_index("x") + lax.axis_index("y")]).astype(jnp.int32)


def _sum_siblings(own, got, name):
    _, rows, cols = own.shape
    half = rows // 2
    t = _row_tile(half)
    nb = half // t

    def body(place_ref, own_ref, got_ref, mine_ref, out_ref):
        total = own_ref[...] + got_ref[...]
        out_ref[...] = total.astype(BF16)

        @pl.when(pl.program_id(1) == place_ref[1])
        def _():
            mine_ref[...] = total

    slab_blk = pl.BlockSpec((None, t, cols), lambda r, j, place: (j, place[0] * nb + r, 0))
    return pl.pallas_call(
        body, name=name,
        grid_spec=pltpu.PrefetchScalarGridSpec(
            num_scalar_prefetch=1, grid=(nb, N_CHIPS),
            in_specs=[slab_blk, slab_blk],
            out_specs=[pl.BlockSpec((t, cols), lambda r, j, place: (r, 0)), slab_blk]),
        out_shape=[jax.ShapeDtypeStruct((half, cols), F32), jax.ShapeDtypeStruct(own.shape, BF16)],
        compiler_params=_params(("arbitrary", "arbitrary")),
    )(_core_and_chip(), own, got)


def _sum_chips(mine, got, name):
    half, cols = mine.shape
    t = _row_tile(half)
    nb = half // t

    def body(place_ref, mine_ref, got_ref, out_ref):
        total = mine_ref[...]
        for k in range(3):
            total = total + got_ref[k].astype(F32)
        out_ref[...] = total

    return pl.pallas_call(
        body, name=name,
        grid_spec=pltpu.PrefetchScalarGridSpec(
            num_scalar_prefetch=1, grid=(nb,),
            in_specs=[pl.BlockSpec((t, cols), lambda r, place: (r, 0)),
                      pl.BlockSpec((3, t, cols), lambda r, place: (0, place[0] * nb + r, 0))],
            out_specs=pl.BlockSpec((t, cols), lambda r, place: (place[0] * nb + r, 0))),
        out_shape=jax.ShapeDtypeStruct((2 * half, cols), F32),
        compiler_params=_params(("arbitrary",)),
    )(_core_and_chip(), mine, got)


def _adamw(w, g, m, v, name, g_row0=0):
    rows, cols = w.shape
    t = _row_tile(rows)
    assert g_row0 % t == 0
    off = g_row0 // t

    def body(w_ref, g_ref, m_ref, v_ref, go_ref, d_ref, mo_ref, vo_ref):
        g_t = g_ref[...]
        m_new = ADAM_B1 * m_ref[...] + (1.0 - ADAM_B1) * g_t
        v_new = ADAM_B2 * v_ref[...] + (1.0 - ADAM_B2) * (g_t * g_t)
        m_hat = m_new / (1.0 - ADAM_B1 ** ADAM_STEP)
        v_hat = v_new / (1.0 - ADAM_B2 ** ADAM_STEP)
        go_ref[...] = g_t
        d_ref[...] = -ADAM_LR * (m_hat / (jnp.sqrt(v_hat) + ADAM_EPS) + ADAM_WD * w_ref[...])
        mo_ref[...] = m_new
        vo_ref[...] = v_new

    blk = pl.BlockSpec((t, cols), lambda r: (r, 0))
    return pl.pallas_call(
        body, name=name, grid=(rows // t,),
        in_specs=[blk, pl.BlockSpec((t, cols), lambda r: (r + off, 0)), blk, blk],
        out_specs=[blk] * 4,
        out_shape=[jax.ShapeDtypeStruct((rows, cols), F32)] * 4,
        compiler_params=_params(("arbitrary",)),
    )(w, g, m, v)


def _pack_small(d, vectors):
    rows = []
    for vec in vectors:
        flat = vec.reshape(-1)
        n_rows = -(-flat.shape[0] // d)
        rows.append(jnp.pad(flat, (0, n_rows * d - flat.shape[0])).reshape(n_rows, d))
    return jnp.concatenate(rows, axis=0)


def _unpack_small(packed, d, shapes):
    out, row = [], 0
    for shape in shapes:
        size = 1
        for s in shape:
            size *= s
        n_rows = -(-size // d)
        out.append(packed[row:row + n_rows].reshape(-1)[:size].reshape(shape))
        row += n_rows
    return out


def kernel(x, norm_pre, norm_post, pool_w_in, pool_w_group, pool_scale, pool_w_out, att_w_in, att_rel_bias, att_w_out, loss_target, m_norm_pre, m_norm_post, m_pool_w_in, m_pool_w_group, m_pool_scale, m_pool_w_out, m_att_w_in, m_att_rel_bias, m_att_w_out, v_norm_pre, v_norm_post, v_pool_w_in, v_pool_w_group, v_pool_scale, v_pool_w_out, v_att_w_in, v_att_rel_bias, v_att_w_out):
    _, s_len, d = x.shape
    gw = d // 2
    q = gw // N_CHIPS
    x2d = x.reshape(s_len, d)
    target = loss_target.reshape(s_len, d)

    def pack_g(p_group):
        return p_group.reshape(N_CHIPS * q, gw)

    wp_shard = jnp.concatenate([pool_w_in[0], pool_w_out[0]], axis=0).astype(BF16)
    wt_shard = jnp.concatenate([att_w_in[0], att_w_out[0]], axis=0).astype(BF16)
    wp, wg = _gather_weights([wp_shard, pack_g(pool_w_group).astype(BF16)])
    wg = wg.reshape(N_CHIPS, N_CHIPS, q, gw)

    x1, h0, mixed, z0, y0, wt = _l0_fwd(x2d, norm_pre[0:1], norm_post[0:1], pool_scale, wp, wg, wt_shard)
    h1, q_, k_, v_, z1 = _l1_inproj(x1, norm_pre[1:2], wt)
    bias = _bias_tiles(att_rel_bias[0])
    o, lse = _attn_fwd(q_, k_, v_, bias)

    g2, gated1, dy1, d_o, dz1, dgpost1, sq = _l1_out(o, z1, x1, target, norm_post[1:2], wt)
    dq, dk, dv, dbias = _attn_bwd(q_, k_, v_, bias, d_o, o, lse)
    dx1, du1, dgpre1 = _l1_in_bwd(dq, dk, dv, dz1, x1, g2, norm_pre[1:2], wt)

    blk = lambda n: n
    zero = lambda n: 0
    slab_t = d + d // N_CHIPS
    whole_in = pl.BlockSpec((None, d, d), lambda n, kk: (n, 0, 0))
    gt = _wgrad(h1, du1, "wgrad_att_in", N_CHIPS, d, d, zero, blk, (N_CHIPS, slab_t, d), whole_in)
    gt = _wgrad(gated1, dy1, "wgrad_att_out", 1, d, d, zero, zero, (N_CHIPS, slab_t, d),
                pl.BlockSpec((N_CHIPS, d // N_CHIPS, d), lambda n, kk: (0, N_CHIPS, 0)), into=gt)

    du0, gated0, dy0, dmm, dgpost0, dscale = _l0_bwd(dx1, y0, mixed, z0, norm_post[0:1], pool_scale, wp, wg)
    grad_x, dgpre0 = _in_proj_bwd(du0, x2d, dx1, norm_pre[0:1], wp)

    swap_rider = lambda part: (functools.partial(_SwapPlan, send_half=_keep_own_half), part, part.shape, (1, 1))
    scatter_rider = lambda send: (_ScatterPlan, send, (3,) + send.shape[1:], (3, 3))
    gi, got_t = _wgrad(h0, du0, "wgrad_pool_in", N_CHIPS, d, d, zero, blk, (N_CHIPS, d, d), whole_in,
                       riders=[swap_rider(gt)])
    mine_t, send_t = _sum_siblings(gt, got_t, "sum_siblings_att")
    go, recv_t, got_i = _wgrad(gated0, dy0, "wgrad_pool_out", N_CHIPS, gw, d, blk, zero, (N_CHIPS, gw, d),
                               pl.BlockSpec((None, gw, d), lambda n, kk: (n, 0, 0)),
                               riders=[scatter_rider(send_t), swap_rider(gi)])
    mine_i, send_i = _sum_siblings(gi, got_i, "sum_siblings_pool_in")
    gg, recv_i, got_o = _wgrad(mixed, dmm, "wgrad_pool_group", N_CHIPS, gw, gw, blk, blk, (N_CHIPS, N_CHIPS, q, gw),
                               pl.BlockSpec((N_CHIPS, None, q, gw), lambda n, kk: (0, n, 0, 0)),
                               riders=[scatter_rider(send_i), swap_rider(go)])
    gg = gg.reshape(N_CHIPS, N_CHIPS * q, gw)
    mine_o, send_o = _sum_siblings(go, got_o, "sum_siblings_pool_out")
    (got_g,) = _swap_halves([gg], "swap_core_partials_group", _keep_own_half, in_place=False)
    mine_g, send_g = _sum_siblings(gg, got_g, "sum_siblings_group")
    d_rel, recv_o, recv_g = _rel_bias_grad(dbias, riders=[scatter_rider(send_o), scatter_rider(send_g)])
    red_t = _sum_chips(mine_t, recv_t, "sum_chips_att")
    red_i = _sum_chips(mine_i, recv_i, "sum_chips_pool_in")
    red_o = _sum_chips(mine_o, recv_o, "sum_chips_pool_out")
    red_g = _sum_chips(mine_g, recv_g, "sum_chips_group")
    grad_t, grad_i, grad_o, grad_g = _swap_halves([red_t, red_i, red_o, red_g], "swap_reduced_halves",
                                                  lambda c: c, in_place=True)

    small_shapes = [norm_pre.shape, norm_post.shape, pool_scale.shape, att_rel_bias.shape]
    part = _pack_small(d, [jnp.concatenate([dgpre0, dgpre1], axis=0), jnp.concatenate([dgpost0, dgpost1], axis=0),
                           dscale, d_rel, sq])
    total = _allreduce_small(part)
    loss = (0.5 / d) * jnp.sum(total[-1])
    w_small = _pack_small(d, [norm_pre, norm_post, pool_scale, att_rel_bias, jnp.zeros((d,), F32)])
    m_small = _pack_small(d, [m_norm_pre, m_norm_post, m_pool_scale, m_att_rel_bias, jnp.zeros((d,), F32)])
    v_small = _pack_small(d, [v_norm_pre, v_norm_post, v_pool_scale, v_att_rel_bias, jnp.ones((d,), F32)])
    small_out = [_unpack_small(a, d, small_shapes) for a in _adamw(w_small, total, m_small, v_small, "adamw_small")]

    big = {}
    for name, w, m, v, grad, row0 in [("pool_w_in", pool_w_in, m_pool_w_in, v_pool_w_in, grad_i, 0),
                                      ("pool_w_out", pool_w_out, m_pool_w_out, v_pool_w_out, grad_o, 0),
                                      ("att_w_in", att_w_in, m_att_w_in, v_att_w_in, grad_t, 0),
                                      ("att_w_out", att_w_out, m_att_w_out, v_att_w_out, grad_t, d)]:
        outs = _adamw(w[0], grad, m[0], v[0], "adamw_" + name, g_row0=row0)
        big[name] = [a.reshape(w.shape) for a in outs]
    outs = _adamw(pack_g(pool_w_group), grad_g, pack_g(m_pool_w_group), pack_g(v_pool_w_group), "adamw_pool_w_group")
    big["pool_w_group"] = [a.reshape(pool_w_group.shape) for a in outs]

    def leaf(kind):
        return (small_out[kind][0], small_out[kind][1], big["pool_w_in"][kind], big["pool_w_group"][kind],
                small_out[kind][2], big["pool_w_out"][kind], big["att_w_in"][kind], small_out[kind][3],
                big["att_w_out"][kind])

    return (loss, grad_x.reshape(x.shape), *leaf(0), *leaf(1), *leaf(2), *leaf(3))
```
